```python
import math
import jax
import jax.numpy as jnp
from jax import lax
import numpy as np

D_MODEL = 1024
BATCH = 2
SEQ = 8192
DEPTH = 2
DEC_BATCH = 32
DEC_SEQ = 1
PAST_LEN = 8192
PAGE_SIZE = 128

HEAD_DIM = 64
GDN_HEADS = 4
GDN_DK = 64
GDN_DV = 64
GDN_CHUNK = 64
NSA_HEADS = 8
NSA_KV_HEADS = 2
NSA_GROUP = NSA_HEADS // NSA_KV_HEADS
NSA_BLOCK = 64
NSA_TOPK = 16
NSA_WINDOW = 512
NSA_QBLOCK = 128
SSM_HEADS = 4
SSM_HEADDIM = 64
SSM_GROUPS = 2
SSM_DSTATE = 128
SSM_CHUNK = 64
CONV_W = 4
PEER_KEYS = 128
PEER_EXPERTS = PEER_KEYS * PEER_KEYS
PEER_HEADS = 8
PEER_TOPK = 16
PEER_DKEY = 256
PEER_BLOCK = 128

ROPE_THETA = 10000.0
EPS = 1e-6

GDN_QK = GDN_HEADS * GDN_DK
GDN_V = GDN_HEADS * GDN_DV
SSM_DI = SSM_HEADS * SSM_HEADDIM
SSM_BC = SSM_GROUPS * SSM_DSTATE
GDN_CONV_CH = 2 * GDN_QK + GDN_V
CONV_CH = GDN_CONV_CH + SSM_DI + 2 * SSM_BC
NSA_Q = NSA_HEADS * HEAD_DIM
NSA_KV = NSA_KV_HEADS * HEAD_DIM
MIX_WIDTH = GDN_V + NSA_Q + SSM_DI
IN_LAYOUT = (("conv", CONV_CH), ("gdn_z", GDN_V), ("gdn_b", GDN_HEADS), ("gdn_a", GDN_HEADS),
             ("ssm_z", SSM_DI), ("ssm_dt", SSM_HEADS), ("nsa_q", NSA_Q), ("nsa_kv", 6 * NSA_KV),
             ("nsa_g", 3 * NSA_HEADS))
D_IN = CONV_CH + GDN_V + 2 * GDN_HEADS + SSM_DI + SSM_HEADS + NSA_Q + 6 * NSA_KV + 3 * NSA_HEADS

kernel_name = "hybrid_gdn_nsa_ssd_peer_step"


def _rmsnorm(x, g):
    xf = x.astype(jnp.float32)
    y = xf * lax.rsqrt(jnp.mean(xf * xf, axis=-1, keepdims=True) + EPS)
    return (y * g.astype(jnp.float32)).astype(x.dtype)


def _l2norm(x):
    xf = x.astype(jnp.float32)
    return xf * lax.rsqrt(jnp.sum(xf * xf, axis=-1, keepdims=True) + EPS)


def _rope(x, pos):
    half = HEAD_DIM // 2
    inv = ROPE_THETA ** (-jnp.arange(half, dtype=jnp.float32) * 2.0 / HEAD_DIM)
    ang = pos.astype(jnp.float32)[:, None] * inv[None, :]
    cos = jnp.cos(ang)[None, :, None, :]
    sin = jnp.sin(ang)[None, :, None, :]
    xf = x.astype(jnp.float32)
    x1, x2 = xf[..., :half], xf[..., half:]
    return jnp.concatenate([x1 * cos - x2 * sin, x1 * sin + x2 * cos], axis=-1).astype(x.dtype)


def _masked_softmax(s, mask):
    s = jnp.where(mask, s.astype(jnp.float32), -jnp.inf)
    m = jnp.max(s, axis=-1, keepdims=True)
    m = jnp.where(jnp.isfinite(m), m, 0.0)
    p = jnp.where(mask, jnp.exp(s - m), 0.0)
    return p / jnp.maximum(jnp.sum(p, axis=-1, keepdims=True), 1e-30)


def _split_cols(h):
    out = {}
    off = 0
    for name, w in IN_LAYOUT:
        out[name] = h[..., off:off + w]
        off += w
    return out


def _causal_conv(x_ext, w, b):
    T = x_ext.shape[1] - (CONV_W - 1)
    y = b + x_ext[:, 0:T] * w[0]
    for j in range(1, CONV_W):
        y = y + x_ext[:, j:j + T] * w[j]
    return y


def _gated_delta_chunked(q, k, v, beta, g, s0):
    B, T, H, DK = q.shape
    DV = v.shape[-1]
    C = GDN_CHUNK
    NC = -(-T // C)
    pad = NC * C - T

    def prep(a):
        a = jnp.pad(a, [(0, 0), (0, pad)] + [(0, 0)] * (a.ndim - 2))
        a = a.reshape((B, NC, C) + a.shape[2:])
        return jnp.moveaxis(a, 3, 2)

    q, k, v, beta, g = prep(q), prep(k), prep(v), prep(beta), prep(g)
    gc = jnp.cumsum(g, axis=-1)
    diff = gc[..., :, None] - gc[..., None, :]
    idx = jnp.arange(C)
    strict = idx[:, None] > idx[None, :]
    incl = idx[:, None] >= idx[None, :]
    d_strict = jnp.exp(jnp.where(strict, diff, -jnp.inf))
    d_incl = jnp.exp(jnp.where(incl, diff, -jnp.inf))
    kk = jnp.einsum('bnhrd,bnhjd->bnhrj', k, k) * d_strict
    lmat = jnp.eye(C, dtype=q.dtype) + beta[..., :, None] * kk
    rhs = jnp.concatenate([v * beta[..., None], k * (beta * jnp.exp(gc))[..., None]], axis=-1)
    sol = lax.linalg.triangular_solve(lmat, rhs, left_side=True, lower=True, unit_diagonal=True)
    u, wk = sol[..., :DV], sol[..., DV:]
    aqk = jnp.einsum('bnhrd,bnhjd->bnhrj', q, k) * d_incl
    qg = q * jnp.exp(gc)[..., None]
    kd = k * jnp.exp(gc[..., -1:] - gc)[..., None]
    gend = jnp.exp(gc[..., -1])

    def step(s, xs):
        u_c, wk_c, aqk_c, qg_c, kd_c, ge_c = xs
        w = u_c - jnp.einsum('bhck,bhkv->bhcv', wk_c, s)
        o = jnp.einsum('bhck,bhkv->bhcv', qg_c, s) + jnp.einsum('bhrj,bhjv->bhrv', aqk_c, w)
        s = ge_c[..., None, None] * s + jnp.einsum('bhck,bhcv->bhkv', kd_c, w)
        return s, o

    xs = tuple(jnp.moveaxis(a, 1, 0) for a in (u, wk, aqk, qg, kd, gend))
    s_fin, o = lax.scan(step, s0, xs)
    o = jnp.transpose(o, (1, 0, 3, 2, 4)).reshape(B, NC * C, H, DV)[:, :T]
    return o, s_fin


def _gdn(qkv, z, b, a, s0, a_log, dt_bias, norm_g):
    B, T, _ = qkv.shape
    q = qkv[..., :GDN_QK].reshape(B, T, GDN_HEADS, GDN_DK)
    k = qkv[..., GDN_QK:2 * GDN_QK].reshape(B, T, GDN_HEADS, GDN_DK)
    v = qkv[..., 2 * GDN_QK:].reshape(B, T, GDN_HEADS, GDN_DV).astype(jnp.float32)
    q = _l2norm(q) * (GDN_DK ** -0.5)
    k = _l2norm(k)
    beta = jax.nn.sigmoid(b.astype(jnp.float32))
    g = -jnp.exp(a_log.astype(jnp.float32)) * jax.nn.softplus(a.astype(jnp.float32) + dt_bias.astype(jnp.float32))
    o, s = _gated_delta_chunked(q, k, v, beta, g, s0.astype(jnp.float32))
    o = _rmsnorm(o, norm_g) * jax.nn.silu(z.reshape(B, T, GDN_HEADS, GDN_DV).astype(jnp.float32))
    return o.reshape(B, T, GDN_V).astype(qkv.dtype), s.astype(s0.dtype)


def _ssd_chunked(x, dt, a_head, bm, cm, h0):
    B, T, H, P = x.shape
    C = SSM_CHUNK
    NC = -(-T // C)
    pad = NC * C - T

    def prep(a):
        a = jnp.pad(a, [(0, 0), (0, pad)] + [(0, 0)] * (a.ndim - 2))
        return a.reshape((B, NC, C) + a.shape[2:])

    x, dt, bm, cm = prep(x), prep(dt), prep(bm), prep(cm)
    cs = jnp.cumsum(dt * a_head, axis=2)
    cst = jnp.moveaxis(cs, 2, 3)
    idx = jnp.arange(C)
    incl = idx[:, None] >= idx[None, :]
    seg = jnp.exp(jnp.where(incl, cst[..., :, None] - cst[..., None, :], -jnp.inf))
    xdt = x * dt[..., None]
    cb = jnp.einsum('bzrhn,bzjhn->bzhrj', cm, bm) * seg
    y_diag = jnp.einsum('bzhrj,bzjhp->bzrhp', cb, xdt)
    dec_end = jnp.exp(cs[:, :, -1:, :] - cs)
    states = jnp.einsum('bzjhn,bzjhp->bzhpn', bm, xdt * dec_end[..., None])
    chunk_dec = jnp.exp(cs[:, :, -1, :])

    def step(h, xs):
        st, cd = xs
        return cd[..., None, None] * h + st, h

    h_fin, h_in = lax.scan(step, h0, (jnp.moveaxis(states, 1, 0), jnp.moveaxis(chunk_dec, 1, 0)))
    h_in = jnp.moveaxis(h_in, 0, 1)
    y_off = jnp.einsum('bzrhn,bzhpn->bzrhp', cm * jnp.exp(cs)[..., None], h_in)
    y = (y_diag + y_off).reshape(B, NC * C, H, P)[:, :T]
    return y, h_fin


def _ssm(xbc, z, dt_raw, h0, a_log, dt_bias, d_skip, norm_g):
    B, T, _ = xbc.shape
    xs = xbc[..., :SSM_DI].reshape(B, T, SSM_HEADS, SSM_HEADDIM).astype(jnp.float32)
    bm = xbc[..., SSM_DI:SSM_DI + SSM_BC].reshape(B, T, SSM_GROUPS, SSM_DSTATE).astype(jnp.float32)
    cm = xbc[..., SSM_DI + SSM_BC:].reshape(B, T, SSM_GROUPS, SSM_DSTATE).astype(jnp.float32)
    rep = SSM_HEADS // SSM_GROUPS
    bm = jnp.repeat(bm, rep, axis=2)
    cm = jnp.repeat(cm, rep, axis=2)
    dt = jax.nn.softplus(dt_raw.astype(jnp.float32) + dt_bias.astype(jnp.float32))
    a_head = -jnp.exp(a_log.astype(jnp.float32))
    y, h = _ssd_chunked(xs, dt, a_head, bm, cm, h0.astype(jnp.float32))
    y = y + d_skip.astype(jnp.float32)[:, None] * xs
    y = y.reshape(B, T, SSM_DI) * jax.nn.silu(z.astype(jnp.float32))
    gs = SSM_DI // SSM_GROUPS
    y = _rmsnorm(y.reshape(B, T, SSM_GROUPS, gs), norm_g.reshape(SSM_GROUPS, gs)).reshape(B, T, SSM_DI)
    return y.astype(xbc.dtype), h.astype(h0.dtype)


def _nsa_project(q_raw, kv_raw, g_raw, pos, qk_norm):
    B, T, _ = q_raw.shape
    q = _rope(_rmsnorm(q_raw.reshape(B, T, NSA_HEADS, HEAD_DIM), qk_norm[0]), pos) * (HEAD_DIM ** -0.5)
    kv = kv_raw.reshape(B, T, 6, NSA_KV_HEADS, HEAD_DIM)
    kc = _rope(_rmsnorm(kv[:, :, 0], qk_norm[1]), pos)
    ks = _rope(_rmsnorm(kv[:, :, 2], qk_norm[2]), pos)
    kw = _rope(_rmsnorm(kv[:, :, 4], qk_norm[3]), pos)
    rows = jnp.stack([kc, kv[:, :, 1], ks, kv[:, :, 3]], axis=2)
    win = jnp.stack([kw, kv[:, :, 5]], axis=2)
    gates = jax.nn.sigmoid(g_raw.astype(jnp.float32)).reshape(B, T, NSA_HEADS, 3)
    return q, rows, win, gates


def _nsa_compress(kc, vc, pe, phi):
    B, L = kc.shape[:2]
    nb = L // NSA_BLOCK
    kb = kc.reshape(B, nb, NSA_BLOCK, NSA_KV_HEADS, HEAD_DIM) + pe[0][None, None, :, None, :]
    vb = vc.reshape(B, nb, NSA_BLOCK, NSA_KV_HEADS, HEAD_DIM) + pe[1][None, None, :, None, :]
    ck = jnp.einsum('bnlhd,lde->bnhe', kb, phi[0])
    cv = jnp.einsum('bnlhd,lde->bnhe', vb, phi[1])
    return ck, cv


def _to_blocks(r):
    B, L = r.shape[:2]
    return jnp.transpose(r.reshape(B, L // NSA_BLOCK, NSA_BLOCK, NSA_KV_HEADS, HEAD_DIM), (0, 3, 1, 2, 4))


def _gather_blocks(bt, idx):
    it = jnp.transpose(idx, (0, 2, 1, 3))
    g = jax.vmap(jax.vmap(lambda b_, i_: b_[i_]))(bt, it)
    return jnp.transpose(g, (0, 2, 1, 3, 4, 5))


def _nsa_core(q, qpos, gates, ck, cv, ks_bt, vs_bt, kw, vw, kwpos):
    B, Q = q.shape[:2]
    nb = ck.shape[1]
    qg = q.reshape(B, Q, NSA_KV_HEADS, NSA_GROUP, HEAD_DIM)
    blk = jnp.arange(nb, dtype=jnp.int32)
    mask_c = ((blk + 1) * NSA_BLOCK - 1)[None, :] <= qpos[:, None]
    p_c = _masked_softmax(jnp.einsum('bqhgd,bnhd->bqhgn', qg, ck), mask_c[None, :, None, None, :])
    o_c = jnp.einsum('bqhgn,bnhd->bqhgd', p_c, cv.astype(jnp.float32))
    cur = qpos // NSA_BLOCK
    forced = (blk[None, :] == 0) | (blk[None, :] == cur[:, None]) | (blk[None, :] == cur[:, None] - 1)
    causal = blk[None, :] <= cur[:, None]
    imp = jnp.sum(p_c, axis=3)
    score = jnp.where(forced[None, :, None, :], jnp.inf, jnp.where(causal[None, :, None, :], imp, -jnp.inf))
    top_s, top_i = lax.top_k(score, min(NSA_TOPK, nb))
    kk = top_i.shape[-1]
    k_sel = _gather_blocks(ks_bt, top_i)
    v_sel = _gather_blocks(vs_bt, top_i)
    kpos = top_i[..., None] * NSA_BLOCK + jnp.arange(NSA_BLOCK, dtype=jnp.int32)
    mask_s = (top_s > -jnp.inf)[..., None] & (kpos <= qpos[None, :, None, None, None])
    s_s = jnp.einsum('bqhgd,bqhkld->bqhgkl', qg, k_sel).reshape(B, Q, NSA_KV_HEADS, NSA_GROUP, kk * NSA_BLOCK)
    p_s = _masked_softmax(s_s, mask_s.reshape(B, Q, NSA_KV_HEADS, 1, kk * NSA_BLOCK))
    o_s = jnp.einsum('bqhgs,bqhsd->bqhgd', p_s,
                     v_sel.reshape(B, Q, NSA_KV_HEADS, kk * NSA_BLOCK, HEAD_DIM).astype(jnp.float32))
    delta = qpos[:, None] - kwpos[None, :]
    mask_w = (delta >= 0) & (delta < NSA_WINDOW) & (kwpos[None, :] >= 0)
    p_w = _masked_softmax(jnp.einsum('bqhgd,bshd->bqhgs', qg, kw), mask_w[None, :, None, None, :])
    o_w = jnp.einsum('bqhgs,bshd->bqhgd', p_w, vw.astype(jnp.float32))
    g = gates.reshape(B, Q, NSA_KV_HEADS, NSA_GROUP, 3)
    o = g[..., 0:1] * o_c + g[..., 1:2] * o_s + g[..., 2:3] * o_w
    return o.reshape(B, Q, NSA_HEADS, HEAD_DIM)


def _nsa_prompt(q, rows, win, gates, pe, phi):
    B, T = q.shape[:2]
    nb = -(-T // NSA_BLOCK)
    rows_p = jnp.pad(rows, ((0, 0), (0, nb * NSA_BLOCK - T), (0, 0), (0, 0), (0, 0)))
    ck, cv = _nsa_compress(rows_p[:, :, 0], rows_p[:, :, 1], pe, phi)
    ks_bt = _to_blocks(rows_p[:, :, 2])
    vs_bt = _to_blocks(rows_p[:, :, 3])
    nqb = -(-T // NSA_QBLOCK)
    tq = nqb * NSA_QBLOCK
    qp = jnp.pad(q, ((0, 0), (0, tq - T), (0, 0), (0, 0)))
    gp = jnp.pad(gates, ((0, 0), (0, tq - T), (0, 0), (0, 0)))
    kw_p = jnp.pad(win, ((0, 0), (NSA_WINDOW, tq - T), (0, 0), (0, 0), (0, 0)))

    def block(i):
        s = i * NSA_QBLOCK
        qb = lax.dynamic_slice_in_dim(qp, s, NSA_QBLOCK, axis=1)
        gb = lax.dynamic_slice_in_dim(gp, s, NSA_QBLOCK, axis=1)
        wb = lax.dynamic_slice_in_dim(kw_p, s, NSA_WINDOW + NSA_QBLOCK, axis=1)
        qpos = s + jnp.arange(NSA_QBLOCK, dtype=jnp.int32)
        kwpos = s - NSA_WINDOW + jnp.arange(NSA_WINDOW + NSA_QBLOCK, dtype=jnp.int32)
        return _nsa_core(qb, qpos, gb, ck, cv, ks_bt, vs_bt, wb[:, :, 0], wb[:, :, 1], kwpos)

    o = lax.map(block, jnp.arange(nqb, dtype=jnp.int32))
    o = jnp.moveaxis(o, 0, 1).reshape(B, tq, NSA_HEADS, HEAD_DIM)[:, :T]
    return o, win[:, T - min(NSA_WINDOW, T):]


def _nsa_sample(q, rows, win, gates, past_rows, win_buf, pe, phi):
    B, T = q.shape[:2]
    past = past_rows.shape[1]
    allr = jnp.concatenate([past_rows.astype(rows.dtype), rows], axis=1)
    L = past + T
    nb = -(-L // NSA_BLOCK)
    allr = jnp.pad(allr, ((0, 0), (0, nb * NSA_BLOCK - L), (0, 0), (0, 0), (0, 0)))
    ck, cv = _nsa_compress(allr[:, :, 0], allr[:, :, 1], pe, phi)
    ks_bt = _to_blocks(allr[:, :, 2])
    vs_bt = _to_blocks(allr[:, :, 3])
    wbl = win_buf.shape[1]
    kw_all = jnp.concatenate([win_buf.astype(win.dtype), win], axis=1)
    kwpos = past - wbl + jnp.arange(wbl + T, dtype=jnp.int32)
    qpos = past + jnp.arange(T, dtype=jnp.int32)
    o = _nsa_core(q, qpos, gates, ck, cv, ks_bt, vs_bt, kw_all[:, :, 0], kw_all[:, :, 1], kwpos)
    keep = min(NSA_WINDOW, past + T)
    return o, kw_all[:, wbl + T - keep:]


def _peer(xn, wq, subkeys, u, v):
    B, T, D = xn.shape
    n = B * T
    nb = -(-n // PEER_BLOCK)
    xb = jnp.pad(xn.reshape(n, D), ((0, nb * PEER_BLOCK - n), (0, 0))).reshape(nb, PEER_BLOCK, D)
    half = PEER_DKEY // 2
    sk1 = subkeys[0].astype(jnp.float32)
    sk2 = subkeys[1].astype(jnp.float32)

    def block(xt):
        qh = (xt @ wq).reshape(PEER_BLOCK, PEER_HEADS, PEER_DKEY).astype(jnp.float32)
        s1 = jnp.einsum('thd,kd->thk', qh[..., :half], sk1)
        s2 = jnp.einsum('thd,kd->thk', qh[..., half:], sk2)
        v1, i1 = lax.top_k(s1, PEER_TOPK)
        v2, i2 = lax.top_k(s2, PEER_TOPK)
        cand = (v1[..., :, None] + v2[..., None, :]).reshape(PEER_BLOCK, PEER_HEADS, PEER_TOPK * PEER_TOPK)
        cidx = (i1[..., :, None] * PEER_KEYS + i2[..., None, :]).reshape(PEER_BLOCK, PEER_HEADS, PEER_TOPK * PEER_TOPK)
        sv, si = lax.top_k(cand, PEER_TOPK)
        eidx = jnp.take_along_axis(cidx, si, axis=-1)
        gate = jax.nn.softmax(sv, axis=-1)
        ue = u[eidx]
        ve = v[eidx]
        act = jax.nn.gelu(jnp.einsum('td,thkd->thk', xt, ue).astype(jnp.float32))
        return jnp.einsum('thk,thkd->td', gate * act, ve.astype(jnp.float32))

    y = lax.map(block, xb).reshape(nb * PEER_BLOCK, D)[:n]
    return y.reshape(B, T, D).astype(xn.dtype)


def _layer(x, pos, conv_hist, gdn_s0, ssm_s0, nsa_cache, lp):
    B, T, _ = x.shape
    xn = _rmsnorm(x, lp["norm_mix"])
    parts = _split_cols(xn @ lp["w_in"])
    x_ext = jnp.concatenate([conv_hist.astype(x.dtype), parts["conv"]], axis=1)
    c = jax.nn.silu(_causal_conv(x_ext, lp["conv_w"], lp["conv_b"]))
    new_conv = x_ext[:, T:]
    o_gdn, s_gdn = _gdn(c[..., :GDN_CONV_CH], parts["gdn_z"], parts["gdn_b"], parts["gdn_a"], gdn_s0,
                        lp["gdn_a_log"], lp["gdn_dt_bias"], lp["gdn_norm"])
    o_ssm, s_ssm = _ssm(c[..., GDN_CONV_CH:], parts["ssm_z"], parts["ssm_dt"], ssm_s0,
                        lp["ssm_a_log"], lp["ssm_dt_bias"], lp["ssm_d"], lp["ssm_norm"])
    q, rows, win, gates = _nsa_project(parts["nsa_q"], parts["nsa_kv"], parts["nsa_g"], pos, lp["nsa_qk_norm"])
    if nsa_cache is None:
        o_nsa, new_win = _nsa_prompt(q, rows, win, gates, lp["nsa_pe"], lp["nsa_phi"])
    else:
        o_nsa, new_win = _nsa_sample(q, rows, win, gates, nsa_cache[0], nsa_cache[1], lp["nsa_pe"], lp["nsa_phi"])
    mix = jnp.concatenate([o_gdn, o_nsa.reshape(B, T, NSA_Q).astype(x.dtype), o_ssm], axis=-1)
    x = x + mix @ lp["w_out"]
    x = x + _peer(_rmsnorm(x, lp["norm_ffn"]), lp["peer_wq"], lp["peer_subkeys"], lp["peer_u"], lp["peer_v"])
    return x, (rows, new_win, new_conv, s_gdn, s_ssm)


def setup_inputs(seed: int = 0) -> dict:
    key = jax.random.key(seed)
    ks = jax.random.split(key, 32)
    f32 = jnp.float32
    n_pages = PAST_LEN // PAGE_SIZE
    n_pool = (DEC_BATCH * n_pages * 5) // 4
    wbl = min(NSA_WINDOW, PAST_LEN)

    def nrm(k, shape, scale):
        return jax.random.normal(k, shape, f32) * scale

    def dt_bias_init(k, shape):
        dt = jnp.exp(jax.random.uniform(k, shape, f32, math.log(1e-3), math.log(1e-1)))
        return dt + jnp.log(-jnp.expm1(-dt))

    perm = jax.random.permutation(ks[7], n_pool)[:DEC_BATCH * n_pages]
    return {
        "x_prompt": nrm(ks[0], (BATCH, SEQ, D_MODEL), 1.0),
        "x_sample": nrm(ks[1], (DEC_BATCH, DEC_SEQ, D_MODEL), 1.0),
        "cache_nsa_kv": nrm(ks[2], (DEPTH, n_pool, PAGE_SIZE, 4, NSA_KV_HEADS, HEAD_DIM), 1.0),
        "cache_nsa_win": nrm(ks[3], (DEPTH, DEC_BATCH, wbl, 2, NSA_KV_HEADS, HEAD_DIM), 1.0),
        "state_conv": nrm(ks[4], (DEPTH, DEC_BATCH, CONV_W - 1, CONV_CH), 1.0),
        "state_gdn": nrm(ks[5], (DEPTH, DEC_BATCH, GDN_HEADS, GDN_DK, GDN_DV), 0.1),
        "state_ssm": nrm(ks[6], (DEPTH, DEC_BATCH, SSM_HEADS, SSM_HEADDIM, SSM_DSTATE), 0.1),
        "page_table": perm.reshape(DEC_BATCH, n_pages).astype(jnp.int32),
        "norm_mix": 1.0 + nrm(ks[8], (DEPTH, D_MODEL), 0.02),
        "w_in": nrm(ks[9], (DEPTH, D_MODEL, D_IN), D_MODEL ** -0.5),
        "conv_w": nrm(ks[10], (DEPTH, CONV_W, CONV_CH), CONV_W ** -0.5),
        "conv_b": nrm(ks[11], (DEPTH, CONV_CH), 0.02),
        "gdn_a_log": jnp.log(jax.random.uniform(ks[12], (DEPTH, GDN_HEADS), f32, 1.0, 16.0)),
        "gdn_dt_bias": dt_bias_init(ks[13], (DEPTH, GDN_HEADS)),
        "gdn_norm": 1.0 + nrm(ks[14], (DEPTH, GDN_DV), 0.02),
        "ssm_a_log": jnp.log(jax.random.uniform(ks[15], (DEPTH, SSM_HEADS), f32, 1.0, 16.0)),
        "ssm_dt_bias": dt_bias_init(ks[16], (DEPTH, SSM_HEADS)),
        "ssm_d": 1.0 + nrm(ks[17], (DEPTH, SSM_HEADS), 0.02),
        "ssm_norm": 1.0 + nrm(ks[18], (DEPTH, SSM_DI), 0.02),
        "nsa_qk_norm": 1.0 + nrm(ks[19], (DEPTH, 4, HEAD_DIM), 0.02),
        "nsa_pe": nrm(ks[20], (DEPTH, 2, NSA_BLOCK, HEAD_DIM), 0.1),
        "nsa_phi": nrm(ks[21], (DEPTH, 2, NSA_BLOCK, HEAD_DIM, HEAD_DIM), (NSA_BLOCK * HEAD_DIM) ** -0.5),
        "w_out": nrm(ks[22], (DEPTH, MIX_WIDTH, D_MODEL), MIX_WIDTH ** -0.5),
        "norm_ffn": 1.0 + nrm(ks[23], (DEPTH, D_MODEL), 0.02),
        "peer_wq": nrm(ks[24], (DEPTH, D_MODEL, PEER_HEADS * PEER_DKEY), D_MODEL ** -0.5),
        "peer_subkeys": nrm(ks[25], (DEPTH, 2, PEER_KEYS, PEER_DKEY // 2), (PEER_DKEY // 2) ** -0.5),
        "peer_u": nrm(ks[26], (DEPTH, PEER_EXPERTS, D_MODEL), D_MODEL ** -0.5),
        "peer_v": nrm(ks[27], (DEPTH, PEER_EXPERTS, D_MODEL), PEER_HEADS ** -0.5),
    }


def reference(x_prompt, x_sample, cache_nsa_kv, cache_nsa_win, state_conv, state_gdn, state_ssm, page_table,
              norm_mix, w_in, conv_w, conv_b, gdn_a_log, gdn_dt_bias, gdn_norm, ssm_a_log, ssm_dt_bias, ssm_d,
              ssm_norm, nsa_qk_norm, nsa_pe, nsa_phi, w_out, norm_ffn, peer_wq, peer_subkeys, peer_u, peer_v):
    bp, tp, _ = x_prompt.shape
    bs, ts, _ = x_sample.shape
    past_len = page_table.shape[1] * cache_nsa_kv.shape[2]
    pos_p = jnp.arange(tp, dtype=jnp.int32)
    pos_s = past_len + jnp.arange(ts, dtype=jnp.int32)
    conv0 = jnp.zeros((bp, CONV_W - 1, CONV_CH), x_prompt.dtype)
    gdn0 = jnp.zeros((bp, GDN_HEADS, GDN_DK, GDN_DV), jnp.float32)
    ssm0 = jnp.zeros((bp, SSM_HEADS, SSM_HEADDIM, SSM_DSTATE), jnp.float32)
    yp, ys = x_prompt, x_sample
    outs_p, outs_s = [], []
    for l in range(DEPTH):
        lp = {
            "norm_mix": norm_mix[l], "w_in": w_in[l], "conv_w": conv_w[l], "conv_b": conv_b[l],
            "gdn_a_log": gdn_a_log[l], "gdn_dt_bias": gdn_dt_bias[l], "gdn_norm": gdn_norm[l],
            "ssm_a_log": ssm_a_log[l], "ssm_dt_bias": ssm_dt_bias[l], "ssm_d": ssm_d[l], "ssm_norm": ssm_norm[l],
            "nsa_qk_norm": nsa_qk_norm[l], "nsa_pe": nsa_pe[l], "nsa_phi": nsa_phi[l], "w_out": w_out[l],
            "norm_ffn": norm_ffn[l], "peer_wq": peer_wq[l], "peer_subkeys": peer_subkeys[l],
            "peer_u": peer_u[l], "peer_v": peer_v[l],
        }
        yp, st_p = _layer(yp, pos_p, conv0, gdn0, ssm0, None, lp)
        past = cache_nsa_kv[l][page_table].reshape(bs, past_len, 4, NSA_KV_HEADS, HEAD_DIM)
        ys, st_s = _layer(ys, pos_s, state_conv[l], state_gdn[l], state_ssm[l], (past, cache_nsa_win[l]), lp)
        outs_p.append(st_p)
        outs_s.append(st_s)
    return (yp, ys,
            jnp.stack([o[0] for o in outs_p]), jnp.stack([o[1] for o in outs_p]), jnp.stack([o[2] for o in outs_p]),
            jnp.stack([o[3] for o in outs_p]), jnp.stack([o[4] for o in outs_p]),
            jnp.stack([o[0] for o in outs_s]), jnp.stack([o[1] for o in outs_s]), jnp.stack([o[2] for o in outs_s]),
            jnp.stack([o[3] for o in outs_s]), jnp.stack([o[4] for o in outs_s]))
```

```python
import functools
import math

import jax
import jax.numpy as jnp
from jax import lax
from jax.experimental import pallas as pl
from jax.experimental.pallas import tpu as pltpu

F32 = jnp.float32
BF16 = jnp.bfloat16
HI = lax.Precision.HIGHEST

D_MODEL = 1024
HEAD_DIM = 64
GDN_HEADS = 4
GDN_DK = 64
GDN_DV = 64
CHUNK = 64
NSA_HEADS = 8
NSA_KV_HEADS = 2
NSA_GROUP = 4
NSA_BLOCK = 64
NSA_TOPK = 16
NSA_WINDOW = 512
SSM_HEADS = 4
SSM_HEADDIM = 64
SSM_GROUPS = 2
SSM_DSTATE = 128
CONV_W = 4
PEER_KEYS = 128
PEER_HEADS = 8
PEER_TOPK = 16
PEER_DKEY = 256
ROPE_THETA = 10000.0
EPS = 1e-6

GDN_QK = GDN_HEADS * GDN_DK
GDN_V = GDN_HEADS * GDN_DV
SSM_DI = SSM_HEADS * SSM_HEADDIM
SSM_BC = SSM_GROUPS * SSM_DSTATE
GDN_CONV_CH = 2 * GDN_QK + GDN_V
CONV_CH = GDN_CONV_CH + SSM_DI + 2 * SSM_BC
NSA_Q = NSA_HEADS * HEAD_DIM
NSA_KV = NSA_KV_HEADS * HEAD_DIM
SMALL_W = 128
LANE = 128
VMEM_LIMIT = 48 * 1024 * 1024


def _cparams(*sem):
    return pltpu.CompilerParams(dimension_semantics=sem, vmem_limit_bytes=VMEM_LIMIT)


def _dot(a, b):
    return jnp.dot(a, b, preferred_element_type=F32)


def _dot_hi(a, b):
    return jnp.dot(a, b, preferred_element_type=F32, precision=HI)


def _dot_nt(a, b, precision=None):
    return lax.dot_general(a, b, (((1,), (1,)), ((), ())), preferred_element_type=F32, precision=precision)


def _dot_tn(a, b, precision=None):
    return lax.dot_general(a, b, (((0,), (0,)), ((), ())), preferred_element_type=F32, precision=precision)


def _sigmoid(x):
    return 1.0 / (1.0 + jnp.exp(-x))


def _softplus(x):
    return jnp.maximum(x, 0.0) + jnp.log1p(jnp.exp(-jnp.abs(x)))


def _iota(shape, dim):
    return lax.broadcasted_iota(jnp.int32, shape, dim)


IN_GROUPS = (("conv", CONV_CH), ("z", GDN_V + SSM_DI), ("q", NSA_Q), ("kv", 6 * NSA_KV), ("small", SMALL_W))
IN_CAT = sum(w for _, w in IN_GROUPS)


def _inproj_kernel(x_ref, g_ref, w_ref, conv_ref, z_ref, q_ref, kv_ref, small_ref):
    x = x_ref[...]
    ms = jnp.mean(x * x, axis=-1, keepdims=True)
    xn = (x * lax.rsqrt(ms + EPS) * g_ref[...]).astype(BF16)
    off = 0
    for ref, (_, w) in zip((conv_ref, z_ref, q_ref, kv_ref, small_ref), IN_GROUPS):
        ref[...] = _dot(xn, w_ref[:, off:off + w])
        off += w


def _inproj(x2d, norm_g, w_cat):
    n = x2d.shape[0]
    tm = min(n, 512)
    return pl.pallas_call(
        _inproj_kernel,
        grid=(n // tm,),
        in_specs=[pl.BlockSpec((tm, D_MODEL), lambda i: (i, 0)),
                  pl.BlockSpec((1, D_MODEL), lambda i: (0, 0)),
                  pl.BlockSpec((D_MODEL, IN_CAT), lambda i: (0, 0))],
        out_specs=[pl.BlockSpec((tm, w), lambda i: (i, 0)) for _, w in IN_GROUPS],
        out_shape=[jax.ShapeDtypeStruct((n, w), F32) for _, w in IN_GROUPS],
        compiler_params=_cparams("parallel"),
        name="inproj",
    )(x2d, norm_g.reshape(1, D_MODEL), w_cat)


def _cat_w_in(w_in):
    o = 0
    parts = {}
    for name, w in (("conv", CONV_CH), ("gdn_z", GDN_V), ("gdn_b", GDN_HEADS), ("gdn_a", GDN_HEADS),
                    ("ssm_z", SSM_DI), ("ssm_dt", SSM_HEADS), ("nsa_q", NSA_Q), ("nsa_kv", 6 * NSA_KV),
                    ("nsa_g", 3 * NSA_HEADS)):
        parts[name] = w_in[:, o:o + w]
        o += w
    small = jnp.concatenate([parts["gdn_b"], parts["gdn_a"], parts["ssm_dt"], parts["nsa_g"]], axis=1)
    small = jnp.pad(small, ((0, 0), (0, SMALL_W - small.shape[1])))
    return jnp.concatenate([parts["conv"], parts["gdn_z"], parts["ssm_z"], parts["nsa_q"], parts["nsa_kv"], small],
                           axis=1).astype(BF16)


HALO = 8


def _conv_kernel(x_ref, hist_ref, w_ref, b_ref, c_ref, nc_ref, xe, *, tm, nt):
    t = pl.program_id(1)

    @pl.when(t == 0)
    def _():
        xe[HALO - 3:HALO, :] = hist_ref[0]

    xe[HALO:HALO + tm, :] = x_ref[0]
    y = b_ref[...] + xe[HALO - 3:HALO - 3 + tm, :] * w_ref[0:1, :]
    for j in range(1, CONV_W):
        y = y + xe[HALO - 3 + j:HALO - 3 + j + tm, :] * w_ref[j:j + 1, :]
    c_ref[0] = y * _sigmoid(y)
    last = xe[HALO + tm - 3:HALO + tm, :]

    @pl.when(t == nt - 1)
    def _():
        nc_ref[0] = last

    xe[HALO - 3:HALO, :] = last


def _conv(x, hist, w, b):
    bsz, t, ch = x.shape
    tm = min(t, 512)
    nt = t // tm
    return pl.pallas_call(
        functools.partial(_conv_kernel, tm=tm, nt=nt),
        grid=(bsz, nt),
        in_specs=[pl.BlockSpec((1, tm, ch), lambda i, j: (i, j, 0)),
                  pl.BlockSpec((1, CONV_W - 1, ch), lambda i, j: (i, 0, 0)),
                  pl.BlockSpec((CONV_W, ch), lambda i, j: (0, 0)),
                  pl.BlockSpec((1, ch), lambda i, j: (0, 0))],
        out_specs=[pl.BlockSpec((1, tm, ch), lambda i, j: (i, j, 0)),
                   pl.BlockSpec((1, CONV_W - 1, ch), lambda i, j: (i, 0, 0))],
        out_shape=[jax.ShapeDtypeStruct((bsz, t, ch), F32),
                   jax.ShapeDtypeStruct((bsz, CONV_W - 1, ch), F32)],
        scratch_shapes=[pltpu.VMEM((HALO + tm, ch), F32)],
        compiler_params=_cparams("parallel", "arbitrary"),
        name="conv",
    )(x, hist, w, b.reshape(1, ch))


def _tri(n, strict=False):
    r, c = _iota((n, n), 0), _iota((n, n), 1)
    return (r > c) if strict else (r >= c)


def _row_of(col):
    n = col.shape[0]
    eye = (_iota((n, n), 0) == _iota((n, n), 1)).astype(F32)
    return _dot_hi(jnp.ones((n, n), F32), eye * col)


def _gdn_kernel(q_ref, k_ref, v_ref, z_ref, small_ref, s0_ref, alog_ref, dtb_ref, ng_ref, o_ref, s_ref, st,
                *, nc, t_valid):
    c = pl.program_id(1)

    @pl.when(c == 0)
    def _():
        st[...] = s0_ref[0]

    sm = small_ref[0]
    beta_all = _sigmoid(sm[:, 0:GDN_HEADS])
    g_all = -jnp.exp(alog_ref[...]) * _softplus(sm[:, GDN_HEADS:2 * GDN_HEADS] + dtb_ref[...])
    if t_valid is not None:
        valid = (c * CHUNK + _iota((CHUNK, 1), 0)) < t_valid
        beta_all = jnp.where(valid, beta_all, 0.0)
        g_all = jnp.where(valid, g_all, 0.0)
    incl = _tri(CHUNK)
    strict = _tri(CHUNK, strict=True)
    gc_all = _dot_hi(incl.astype(F32), g_all)
    outs = []
    for h in range(GDN_HEADS):
        sl = slice(h * GDN_DK, (h + 1) * GDN_DK)
        q = q_ref[0][:, sl]
        k = k_ref[0][:, sl]
        v = v_ref[0][:, sl]
        q = q * lax.rsqrt(jnp.sum(q * q, axis=-1, keepdims=True) + EPS) * (GDN_DK ** -0.5)
        k = k * lax.rsqrt(jnp.sum(k * k, axis=-1, keepdims=True) + EPS)
        if t_valid is not None:
            q = jnp.where(valid, q, 0.0)
            k = jnp.where(valid, k, 0.0)
            v = jnp.where(valid, v, 0.0)
        beta = beta_all[:, h:h + 1]
        gc = gc_all[:, h:h + 1]
        diff = gc - _row_of(gc)
        d_strict = jnp.where(strict, jnp.exp(jnp.where(strict, diff, 0.0)), 0.0)
        d_incl = jnp.where(incl, jnp.exp(jnp.where(incl, diff, 0.0)), 0.0)
        egc = jnp.exp(gc)
        p = -(beta * _dot_nt(k, k, HI) * d_strict)
        x = jnp.concatenate([v * beta, k * (beta * egc)], axis=-1)
        for it in range(6):
            x = x + _dot_hi(p, x)
            if it < 5:
                p = _dot_hi(p, p)
        u, wk = x[:, :GDN_DV], x[:, GDN_DV:]
        aqk = _dot_nt(q, k, HI) * d_incl
        gl = gc[CHUNK - 1:CHUNK, :]
        kd = k * jnp.exp(gl - gc)
        s = st[h]
        w = u - _dot_hi(wk, s)
        o = _dot_hi(q * egc, s) + _dot_hi(aqk, w)
        st[h] = jnp.exp(gl) * s + _dot_tn(kd, w, HI)
        o = o * lax.rsqrt(jnp.mean(o * o, axis=-1, keepdims=True) + EPS) * ng_ref[...]
        zh = z_ref[0][:, sl]
        outs.append(o * (zh * _sigmoid(zh)))
    o_ref[0] = jnp.concatenate(outs, axis=-1)

    @pl.when(c == nc - 1)
    def _():
        s_ref[0] = st[...]


def _gdn(c, z, small, s0, a_log, dt_bias, norm_g, t_valid=None):
    bsz, t, _ = c.shape
    nc = t // CHUNK
    return pl.pallas_call(
        functools.partial(_gdn_kernel, nc=nc, t_valid=t_valid),
        grid=(bsz, nc),
        in_specs=[pl.BlockSpec((1, CHUNK, GDN_QK), lambda i, j: (i, j, 0)),
                  pl.BlockSpec((1, CHUNK, GDN_QK), lambda i, j: (i, j, 1)),
                  pl.BlockSpec((1, CHUNK, GDN_V), lambda i, j: (i, j, 2)),
                  pl.BlockSpec((1, CHUNK, GDN_V), lambda i, j: (i, j, 0)),
                  pl.BlockSpec((1, CHUNK, SMALL_W), lambda i, j: (i, j, 0)),
                  pl.BlockSpec((1, GDN_HEADS, GDN_DK, GDN_DV), lambda i, j: (i, 0, 0, 0)),
                  pl.BlockSpec((1, GDN_HEADS), lambda i, j: (0, 0)),
                  pl.BlockSpec((1, GDN_HEADS), lambda i, j: (0, 0)),
                  pl.BlockSpec((1, GDN_DV), lambda i, j: (0, 0))],
        out_specs=[pl.BlockSpec((1, CHUNK, GDN_V), lambda i, j: (i, j, 0)),
                   pl.BlockSpec((1, GDN_HEADS, GDN_DK, GDN_DV), lambda i, j: (i, 0, 0, 0))],
        out_shape=[jax.ShapeDtypeStruct((bsz, t, GDN_V), F32),
                   jax.ShapeDtypeStruct((bsz, GDN_HEADS, GDN_DK, GDN_DV), F32)],
        scratch_shapes=[pltpu.VMEM((GDN_HEADS, GDN_DK, GDN_DV), F32)],
        compiler_params=_cparams("parallel", "arbitrary"),
        name="gdn",
    )(c, c, c, z, small, s0, a_log.reshape(1, -1), dt_bias.reshape(1, -1), norm_g.reshape(1, -1))


def _ssd_kernel(x_ref, b_ref, c_ref, z_ref, small_ref, h0_ref, alog_ref, dtb_ref, dsk_ref, ng_ref, o_ref, h_ref, st,
                *, nc, t_valid):
    ci = pl.program_id(1)

    @pl.when(ci == 0)
    def _():
        st[...] = h0_ref[0]

    sm = small_ref[0]
    dt_all = _softplus(sm[:, 2 * GDN_HEADS:2 * GDN_HEADS + SSM_HEADS] + dtb_ref[...])
    xs = x_ref[0]
    bm_all = b_ref[0]
    cm_all = c_ref[0]
    if t_valid is not None:
        valid = (ci * CHUNK + _iota((CHUNK, 1), 0)) < t_valid
        dt_all = jnp.where(valid, dt_all, 0.0)
        xs = jnp.where(valid, xs, 0.0)
        bm_all = jnp.where(valid, bm_all, 0.0)
        cm_all = jnp.where(valid, cm_all, 0.0)
    incl = _tri(CHUNK)
    cs_all = _dot_hi(incl.astype(F32), dt_all * (-jnp.exp(alog_ref[...])))
    rep = SSM_HEADS // SSM_GROUPS
    ys = []
    for h in range(SSM_HEADS):
        g = h // rep
        x = xs[:, h * SSM_HEADDIM:(h + 1) * SSM_HEADDIM]
        bm = bm_all[:, g * SSM_DSTATE:(g + 1) * SSM_DSTATE]
        cm = cm_all[:, g * SSM_DSTATE:(g + 1) * SSM_DSTATE]
        dt = dt_all[:, h:h + 1]
        cs = cs_all[:, h:h + 1]
        diff = cs - _row_of(cs)
        seg = jnp.where(incl, jnp.exp(jnp.where(incl, diff, 0.0)), 0.0)
        xdt = x * dt
        cb = _dot_nt(cm, bm, HI) * seg
        cl = cs[CHUNK - 1:CHUNK, :]
        hs = st[h]
        y = _dot_hi(cb, xdt) + _dot_nt(cm * jnp.exp(cs), hs, HI)
        st[h] = jnp.exp(cl) * hs + _dot_tn(xdt * jnp.exp(cl - cs), bm, HI)
        ys.append(y + dsk_ref[0:1, h:h + 1] * x)
    y = jnp.concatenate(ys, axis=-1)
    zz = z_ref[0]
    y = y * (zz * _sigmoid(zz))
    gs = SSM_DI // SSM_GROUPS
    outs = []
    for g in range(SSM_GROUPS):
        yg = y[:, g * gs:(g + 1) * gs]
        outs.append(yg * lax.rsqrt(jnp.mean(yg * yg, axis=-1, keepdims=True) + EPS) * ng_ref[0:1, g * gs:(g + 1) * gs])
    o_ref[0] = jnp.concatenate(outs, axis=-1)

    @pl.when(ci == nc - 1)
    def _():
        h_ref[0] = st[...]


def _ssd(c, z, small, h0, a_log, dt_bias, d_skip, norm_g, t_valid=None):
    bsz, t, _ = c.shape
    nc = t // CHUNK
    w = SSM_DI
    return pl.pallas_call(
        functools.partial(_ssd_kernel, nc=nc, t_valid=t_valid),
        grid=(bsz, nc),
        in_specs=[pl.BlockSpec((1, CHUNK, w), lambda i, j: (i, j, 3)),
                  pl.BlockSpec((1, CHUNK, w), lambda i, j: (i, j, 4)),
                  pl.BlockSpec((1, CHUNK, w), lambda i, j: (i, j, 5)),
                  pl.BlockSpec((1, CHUNK, w), lambda i, j: (i, j, 1)),
                  pl.BlockSpec((1, CHUNK, SMALL_W), lambda i, j: (i, j, 0)),
                  pl.BlockSpec((1, SSM_HEADS, SSM_HEADDIM, SSM_DSTATE), lambda i, j: (i, 0, 0, 0)),
                  pl.BlockSpec((1, SSM_HEADS), lambda i, j: (0, 0)),
                  pl.BlockSpec((1, SSM_HEADS), lambda i, j: (0, 0)),
                  pl.BlockSpec((1, SSM_HEADS), lambda i, j: (0, 0)),
                  pl.BlockSpec((1, SSM_DI), lambda i, j: (0, 0))],
        out_specs=[pl.BlockSpec((1, CHUNK, SSM_DI), lambda i, j: (i, j, 0)),
                   pl.BlockSpec((1, SSM_HEADS, SSM_HEADDIM, SSM_DSTATE), lambda i, j: (i, 0, 0, 0))],
        out_shape=[jax.ShapeDtypeStruct((bsz, t, SSM_DI), F32),
                   jax.ShapeDtypeStruct((bsz, SSM_HEADS, SSM_HEADDIM, SSM_DSTATE), F32)],
        scratch_shapes=[pltpu.VMEM((SSM_HEADS, SSM_HEADDIM, SSM_DSTATE), F32)],
        compiler_params=_cparams("parallel", "arbitrary"),
        name="ssd",
    )(c, c, c, z, small, h0, a_log.reshape(1, -1), dt_bias.reshape(1, -1), d_skip.reshape(1, -1),
      norm_g.reshape(1, -1))


PAGE = 128
NEG = -1e30


def _rope_tables(pos):
    half = HEAD_DIM // 2
    inv = ROPE_THETA ** (-jnp.arange(half, dtype=F32) * 2.0 / HEAD_DIM)
    ang = pos.astype(F32)[:, None] * inv[None, :]
    cos, sin = jnp.cos(ang), jnp.sin(ang)
    return jnp.tile(cos, (1, 4)), jnp.tile(jnp.concatenate([-sin, sin], axis=1), (1, 2))


def _nsaproj_kernel(q_ref, kv_ref, small_ref, cos_ref, sin_ref, qn_ref, bd_ref,
                    qo_ref, rows_ref, win_ref, gates_ref, ks_ref, vs_ref, kw_ref, vw_ref):
    cosf = cos_ref[...]
    sins = sin_ref[...]
    bd = bd_ref[...]
    first_half = (_iota((1, LANE), 1) % HEAD_DIM) < (HEAD_DIM // 2)

    def normrope(x, g):
        y = x * lax.rsqrt(_dot_hi(x * x, bd) + EPS) * g
        rot = jnp.where(first_half, pltpu.roll(y, LANE - HEAD_DIM // 2, 1), pltpu.roll(y, HEAD_DIM // 2, 1))
        return y * cosf + rot * sins

    qr = q_ref[0]
    qo_ref[0] = jnp.concatenate(
        [normrope(qr[:, i * LANE:(i + 1) * LANE], qn_ref[0:1, :]) * (HEAD_DIM ** -0.5) for i in range(NSA_Q // LANE)],
        axis=-1)
    kv = kv_ref[0]
    kc = normrope(kv[:, 0:LANE], qn_ref[1:2, :])
    vc = kv[:, LANE:2 * LANE]
    ks = normrope(kv[:, 2 * LANE:3 * LANE], qn_ref[2:3, :])
    vs = kv[:, 3 * LANE:4 * LANE]
    kw = normrope(kv[:, 4 * LANE:5 * LANE], qn_ref[3:4, :])
    vw = kv[:, 5 * LANE:6 * LANE]
    rows_ref[0] = jnp.concatenate([kc, vc, ks, vs], axis=-1)
    win_ref[0] = jnp.concatenate([kw, vw], axis=-1)
    gates_ref[0] = _sigmoid(small_ref[0])
    for h in range(NSA_KV_HEADS):
        sl = slice(h * HEAD_DIM, (h + 1) * HEAD_DIM)
        ks_ref[0, h] = ks[:, sl].astype(BF16)
        vs_ref[0, h] = vs[:, sl].astype(BF16)
        kw_ref[0, h] = kw[:, sl].astype(BF16)
        vw_ref[0, h] = vw[:, sl].astype(BF16)


def _nsaproj(q_raw, kv_raw, small, pos, qk_norm):
    bsz, t, _ = q_raw.shape
    tm = min(t, 512)
    cos, sin = _rope_tables(pos)
    qn = jnp.tile(qk_norm, (1, 2))
    bd = jnp.kron(jnp.eye(2, dtype=F32), jnp.full((HEAD_DIM, HEAD_DIM), 1.0 / HEAD_DIM, F32))
    tok = lambda w: pl.BlockSpec((1, tm, w), lambda i, j: (i, j, 0))
    headed = pl.BlockSpec((1, NSA_KV_HEADS, tm, HEAD_DIM), lambda i, j: (i, 0, j, 0))
    return pl.pallas_call(
        _nsaproj_kernel,
        grid=(bsz, t // tm),
        in_specs=[tok(NSA_Q), tok(6 * NSA_KV), tok(SMALL_W),
                  pl.BlockSpec((tm, LANE), lambda i, j: (j, 0)),
                  pl.BlockSpec((tm, LANE), lambda i, j: (j, 0)),
                  pl.BlockSpec((4, LANE), lambda i, j: (0, 0)),
                  pl.BlockSpec((LANE, LANE), lambda i, j: (0, 0))],
        out_specs=[tok(NSA_Q), tok(4 * NSA_KV), tok(2 * NSA_KV), tok(SMALL_W), headed, headed, headed, headed],
        out_shape=[jax.ShapeDtypeStruct((bsz, t, NSA_Q), F32),
                   jax.ShapeDtypeStruct((bsz, t, 4 * NSA_KV), F32),
                   jax.ShapeDtypeStruct((bsz, t, 2 * NSA_KV), F32),
                   jax.ShapeDtypeStruct((bsz, t, SMALL_W), F32)]
                  + [jax.ShapeDtypeStruct((bsz, NSA_KV_HEADS, t, HEAD_DIM), BF16)] * 4,
        compiler_params=_cparams("parallel", "parallel"),
        name="nsaproj",
    )(q_raw, kv_raw, small, cos, sin, qn, bd)


CMP_PAGES = 16


def _compress_kernel(pt_ref, *refs):
    del pt_ref
    page_refs = refs[:CMP_PAGES]
    pe_ref, phi_ref, ck_ref, cv_ref, bufk, bufv = refs[CMP_PAGES:]
    for i, r in enumerate(page_refs):
        bufk[i * PAGE:(i + 1) * PAGE, :] = r[0, :, 0:LANE]
        bufv[i * PAGE:(i + 1) * PAGE, :] = r[0, :, LANE:2 * LANE]
    nblk = CMP_PAGES * PAGE // NSA_BLOCK

    def body(l, carry):
        ak, av = carry
        kl = bufk[pl.ds(l, nblk, stride=NSA_BLOCK), :] + pe_ref[0, pl.ds(l, 1), :]
        vl = bufv[pl.ds(l, nblk, stride=NSA_BLOCK), :] + pe_ref[1, pl.ds(l, 1), :]
        return ak + _dot_hi(kl, phi_ref[0, l]), av + _dot_hi(vl, phi_ref[1, l])

    zero = jnp.zeros((nblk, LANE), F32)
    ak, av = lax.fori_loop(0, NSA_BLOCK, body, (zero, zero))
    ck_ref[0] = ak
    cv_ref[0] = av


def _compress(pages, page_table, pe, phi):
    bsz, npages = page_table.shape
    steps = npages // CMP_PAGES
    nblk = CMP_PAGES * PAGE // NSA_BLOCK
    pe2 = jnp.tile(pe, (1, 1, 2))
    eye2 = jnp.eye(2, dtype=F32)
    phi_bd = jnp.einsum('ab,klde->kladbe', eye2, phi).reshape(2, NSA_BLOCK, LANE, LANE)
    page_specs = [pl.BlockSpec((1, PAGE, 2 * LANE), lambda b, j, pt, i=i: (pt[b, j * CMP_PAGES + i], 0, 0))
                  for i in range(CMP_PAGES)]
    return pl.pallas_call(
        _compress_kernel,
        grid_spec=pltpu.PrefetchScalarGridSpec(
            num_scalar_prefetch=1,
            grid=(bsz, steps),
            in_specs=page_specs + [pl.BlockSpec((2, NSA_BLOCK, LANE), lambda b, j, pt: (0, 0, 0)),
                                   pl.BlockSpec((2, NSA_BLOCK, LANE, LANE), lambda b, j, pt: (0, 0, 0, 0))],
            out_specs=[pl.BlockSpec((1, nblk, LANE), lambda b, j, pt: (b, j, 0)),
                       pl.BlockSpec((1, nblk, LANE), lambda b, j, pt: (b, j, 0))],
            scratch_shapes=[pltpu.VMEM((CMP_PAGES * PAGE, LANE), F32)] * 2),
        out_shape=[jax.ShapeDtypeStruct((bsz, steps * nblk, LANE), F32)] * 2,
        compiler_params=_cparams("parallel", "parallel"),
        name="compress",
    )(page_table, *([pages] * CMP_PAGES), pe2, phi_bd)


QBLK = 128
KTILE = 512


def _colmax(x):
    return jnp.max(x, axis=0, keepdims=True)


def _colsum(x):
    return jnp.sum(x, axis=0, keepdims=True)


def _tile4(x):
    return jnp.concatenate([x] * NSA_GROUP, axis=1)


def _select_blocks(imp, q0, nb):
    nio = _iota(imp.shape, 0)
    cur = (q0 + _iota((1, imp.shape[1]), 1)) // NSA_BLOCK
    forced = (nio == 0) | (nio == cur) | (nio == cur - 1)
    causal = nio <= cur
    score = jnp.where(forced, jnp.inf, jnp.where(causal, imp, -jnp.inf))
    avail = jnp.ones(imp.shape, jnp.bool_)
    for _ in range(min(NSA_TOPK, nb)):
        m = _colmax(jnp.where(avail, score, -jnp.inf))
        eq = avail & (score == m)
        idx = jnp.min(jnp.where(eq, nio, nb), axis=0, keepdims=True)
        avail = avail & (nio != idx)
    return jnp.where(causal & jnp.logical_not(avail), 1.0, 0.0)


def _nsa_prompt_kernel(q_ref, ck_ref, cv_ref, ks_ref, vs_ref, kw_ref, vw_ref, gt_ref, o_ref, sel_sc, *, t_total):
    nb = t_total // NSA_BLOCK
    kt = min(KTILE, t_total)
    wk = min(NSA_WINDOW + QBLK, t_total)
    bpt = kt // NSA_BLOCK
    qi = pl.program_id(1)
    q0 = qi * QBLK
    qall = q_ref[0]
    gt = gt_ref[0]
    qpos = _tile4(q0 + _iota((1, QBLK), 1))
    eye = (_iota((HEAD_DIM, HEAD_DIM), 0) == _iota((HEAD_DIM, HEAD_DIM), 1)).astype(F32)
    outs = []
    for h in range(NSA_KV_HEADS):
        qs = jnp.concatenate([qall[:, (h * NSA_GROUP + g) * HEAD_DIM:(h * NSA_GROUP + g + 1) * HEAD_DIM]
                              for g in range(NSA_GROUP)], axis=0)
        qsb = qs.astype(BF16)
        sl = slice(h * HEAD_DIM, (h + 1) * HEAD_DIM)
        s = _dot_nt(ck_ref[0][:, sl], qs, HI)
        mask = ((_iota((nb, 1), 0) + 1) * NSA_BLOCK - 1) <= qpos
        m = _colmax(jnp.where(mask, s, NEG))
        p = jnp.where(mask, jnp.exp(s - m), 0.0)
        p = p / jnp.maximum(_colsum(p), 1e-30)
        oc = _dot_tn(cv_ref[0][:, sl], p, HI)
        imp = p[:, 0:QBLK]
        for g in range(1, NSA_GROUP):
            imp = imp + p[:, g * QBLK:(g + 1) * QBLK]
        sel_sc[...] = _select_blocks(imp, q0, nb)

        def body(i, carry):
            m_i, l_i, acc = carry
            k0 = pl.multiple_of(i * kt, kt)
            kk = ks_ref[0, h, pl.ds(k0, kt), :]
            vv = vs_ref[0, h, pl.ds(k0, kt), :]
            sc = _dot_nt(kk, qsb)
            selrows = sel_sc[pl.ds(pl.multiple_of(i * bpt, bpt), bpt), :]
            selx = jnp.concatenate([jnp.broadcast_to(selrows[j:j + 1, :], (NSA_BLOCK, QBLK)) for j in range(bpt)],
                                   axis=0)
            kpos = k0 + _iota((kt, 1), 0)
            msk = (_tile4(selx) > 0.5) & (kpos <= qpos)
            m_n = jnp.maximum(m_i, _colmax(jnp.where(msk, sc, NEG)))
            alpha = jnp.exp(m_i - m_n)
            pp = jnp.where(msk, jnp.exp(sc - m_n), 0.0)
            return m_n, l_i * alpha + _colsum(pp), acc * alpha + _dot_tn(vv, pp.astype(BF16))

        ntile = (q0 + QBLK + kt - 1) // kt
        init = (jnp.full((1, NSA_GROUP * QBLK), NEG, F32), jnp.zeros((1, NSA_GROUP * QBLK), F32),
                jnp.zeros((HEAD_DIM, NSA_GROUP * QBLK), F32))
        _, l_s, acc_s = lax.fori_loop(0, ntile, body, init)
        osel = acc_s / jnp.maximum(l_s, 1e-30)
        w0 = pl.multiple_of(jnp.maximum(q0 + QBLK - wk, 0), QBLK)
        sw = _dot_nt(kw_ref[0, h, pl.ds(w0, wk), :], qsb)
        delta = qpos - (w0 + _iota((wk, 1), 0))
        mw = (delta >= 0) & (delta < NSA_WINDOW)
        m = _colmax(jnp.where(mw, sw, NEG))
        pw = jnp.where(mw, jnp.exp(sw - m), 0.0)
        ow = _dot_tn(vw_ref[0, h, pl.ds(w0, wk), :], pw.astype(BF16)) / jnp.maximum(_colsum(pw), 1e-30)
        gate = [jnp.concatenate([gt[(h * NSA_GROUP + g) * 3 + j:(h * NSA_GROUP + g) * 3 + j + 1, :]
                                 for g in range(NSA_GROUP)], axis=1) for j in range(3)]
        ot = gate[0] * oc + gate[1] * osel + gate[2] * ow
        for g in range(NSA_GROUP):
            outs.append(_dot_tn(ot[:, g * QBLK:(g + 1) * QBLK], eye, HI))
    o_ref[0] = jnp.concatenate(outs, axis=-1)


def _nsa_prompt(q, ck, cv, ks, vs, kw, vw, gates_t):
    bsz, t, _ = q.shape
    nb = t // NSA_BLOCK
    full_kv = pl.BlockSpec((1, NSA_KV_HEADS, t, HEAD_DIM), lambda i, j: (i, 0, 0, 0))
    return pl.pallas_call(
        functools.partial(_nsa_prompt_kernel, t_total=t),
        grid=(bsz, t // QBLK),
        in_specs=[pl.BlockSpec((1, QBLK, NSA_Q), lambda i, j: (i, j, 0)),
                  pl.BlockSpec((1, nb, LANE), lambda i, j: (i, 0, 0)),
                  pl.BlockSpec((1, nb, LANE), lambda i, j: (i, 0, 0)),
                  full_kv, full_kv, full_kv, full_kv,
                  pl.BlockSpec((1, 3 * NSA_HEADS, QBLK), lambda i, j: (i, 0, j))],
        out_specs=pl.BlockSpec((1, QBLK, NSA_Q), lambda i, j: (i, j, 0)),
        out_shape=jax.ShapeDtypeStruct((bsz, t, NSA_Q), F32),
        scratch_shapes=[pltpu.VMEM((nb, QBLK), F32)],
        compiler_params=_cparams("parallel", "arbitrary"),
        name="nsa_prompt",
    )(q, ck, cv, ks, vs, kw, vw, gates_t)


def _nsa_sample_cmp_kernel(q_ref, ck_ref, cv_ref, oc_ref, idx_ref, *, past):
    nbc = ck_ref.shape[1]
    cur = past // NSA_BLOCK
    q = q_ref[0]
    lane = _iota((1, nbc), 1)
    mask = ((lane + 1) * NSA_BLOCK - 1) <= past
    ocs, idxs = [], []
    for h in range(NSA_KV_HEADS):
        qh = jnp.concatenate([q[:, (h * NSA_GROUP + g) * HEAD_DIM:(h * NSA_GROUP + g + 1) * HEAD_DIM]
                              for g in range(NSA_GROUP)] + [jnp.zeros((8 - NSA_GROUP, HEAD_DIM), F32)], axis=0)
        sl = slice(h * HEAD_DIM, (h + 1) * HEAD_DIM)
        s = _dot_nt(qh, ck_ref[0][:, sl], HI)
        m = jnp.max(jnp.where(mask, s, NEG), axis=1, keepdims=True)
        p = jnp.where(mask, jnp.exp(s - m), 0.0)
        p = p / jnp.maximum(jnp.sum(p, axis=1, keepdims=True), 1e-30)
        ocs.append(_dot_hi(p, cv_ref[0][:, sl])[0:NSA_GROUP])
        imp = p[0:1]
        for g in range(1, NSA_GROUP):
            imp = imp + p[g:g + 1]
        forced = (lane == 0) | (lane == cur) | (lane == cur - 1)
        score = jnp.where(forced, jnp.inf, jnp.where(lane <= cur, imp, -jnp.inf))
        avail = jnp.ones((1, nbc), jnp.bool_)
        slot = _iota((1, NSA_TOPK), 1)
        picked = jnp.full((1, NSA_TOPK), cur, jnp.int32)
        for r in range(NSA_TOPK - 1):
            mm = jnp.max(jnp.where(avail, score, -jnp.inf), axis=1, keepdims=True)
            eq = avail & (score == mm)
            idx = jnp.min(jnp.where(eq, lane, nbc), axis=1, keepdims=True)
            avail = avail & (lane != idx)
            picked = jnp.where(slot == r, idx, picked)
        idxs.append(picked)
    oc_ref[0] = jnp.concatenate(ocs, axis=0)
    idx_ref[0] = jnp.concatenate(idxs, axis=1)


def _nsa_sample_cmp(q3, ck, cv, past):
    bsz = q3.shape[0]
    nbc = ck.shape[1]
    return pl.pallas_call(
        functools.partial(_nsa_sample_cmp_kernel, past=past),
        grid=(bsz,),
        in_specs=[pl.BlockSpec((1, 1, NSA_Q), lambda b: (b, 0, 0)),
                  pl.BlockSpec((1, nbc, LANE), lambda b: (b, 0, 0)),
                  pl.BlockSpec((1, nbc, LANE), lambda b: (b, 0, 0))],
        out_specs=[pl.BlockSpec((1, NSA_HEADS, HEAD_DIM), lambda b: (b, 0, 0)),
                   pl.BlockSpec((1, 1, NSA_KV_HEADS * NSA_TOPK), lambda b: (b, 0, 0))],
        out_shape=[jax.ShapeDtypeStruct((bsz, NSA_HEADS, HEAD_DIM), F32),
                   jax.ShapeDtypeStruct((bsz, 1, NSA_KV_HEADS * NSA_TOPK), jnp.int32)],
        compiler_params=_cparams("parallel"),
        name="nsa_sample_cmp",
    )(q3, ck, cv)


def _nsa_sample_attn_kernel(pt_ref, idx_ref, *refs, past):
    del pt_ref, idx_ref
    nsel = NSA_KV_HEADS * NSA_TOPK
    blk_refs = refs[:nsel]
    q_ref, row_ref, nwin_ref, wc_ref, g_ref, oc_ref, o_ref, win_ref = refs[nsel:]
    wbl = wc_ref.shape[1]
    q = q_ref[0]
    row = row_ref[0]
    nwin = nwin_ref[0]
    gates = g_ref[0]
    oc = oc_ref[0]
    outs = []
    for h in range(NSA_KV_HEADS):
        sl = slice(h * HEAD_DIM, (h + 1) * HEAD_DIM)
        qh = jnp.concatenate([q[:, (h * NSA_GROUP + g) * HEAD_DIM:(h * NSA_GROUP + g + 1) * HEAD_DIM]
                              for g in range(NSA_GROUP)] + [jnp.zeros((8 - NSA_GROUP, HEAD_DIM), F32)], axis=0)
        kk = jnp.concatenate([blk_refs[h * NSA_TOPK + k][0][:, sl] for k in range(NSA_TOPK)], axis=0)
        vv = jnp.concatenate([blk_refs[h * NSA_TOPK + k][0][:, LANE + h * HEAD_DIM:LANE + (h + 1) * HEAD_DIM]
                              for k in range(NSA_TOPK)], axis=0)
        s = _dot_nt(qh, kk, HI)
        ms = _iota((1, NSA_TOPK * NSA_BLOCK), 1) < (NSA_TOPK - 1) * NSA_BLOCK
        ks_new = row[:, 2 * LANE + h * HEAD_DIM:2 * LANE + (h + 1) * HEAD_DIM]
        vs_new = row[:, 3 * LANE + h * HEAD_DIM:3 * LANE + (h + 1) * HEAD_DIM]
        s_new = jnp.sum(qh * ks_new, axis=1, keepdims=True)
        m = jnp.maximum(jnp.max(jnp.where(ms, s, NEG), axis=1, keepdims=True), s_new)
        p = jnp.where(ms, jnp.exp(s - m), 0.0)
        p_new = jnp.exp(s_new - m)
        osel = (_dot_hi(p, vv) + p_new * vs_new) / jnp.maximum(jnp.sum(p, axis=1, keepdims=True) + p_new, 1e-30)
        wc = wc_ref[0]
        sw = _dot_nt(qh, wc[:, sl], HI)
        delta = wbl - _iota((1, wbl), 1)
        mw = (delta >= 0) & (delta < NSA_WINDOW) & ((past - delta) >= 0)
        kw_new = nwin[:, sl]
        vw_new = nwin[:, LANE + h * HEAD_DIM:LANE + (h + 1) * HEAD_DIM]
        sw_new = jnp.sum(qh * kw_new, axis=1, keepdims=True)
        m = jnp.maximum(jnp.max(jnp.where(mw, sw, NEG), axis=1, keepdims=True), sw_new)
        pw = jnp.where(mw, jnp.exp(sw - m), 0.0)
        pw_new = jnp.exp(sw_new - m)
        ow = (_dot_hi(pw, wc[:, LANE + h * HEAD_DIM:LANE + (h + 1) * HEAD_DIM]) + pw_new * vw_new) / jnp.maximum(
            jnp.sum(pw, axis=1, keepdims=True) + pw_new, 1e-30)
        for g in range(NSA_GROUP):
            c0 = 12 + (h * NSA_GROUP + g) * 3
            outs.append(gates[:, c0:c0 + 1] * oc[h * NSA_GROUP + g:h * NSA_GROUP + g + 1, :]
                        + gates[:, c0 + 1:c0 + 2] * osel[g:g + 1, :] + gates[:, c0 + 2:c0 + 3] * ow[g:g + 1, :])
    o_ref[0] = jnp.concatenate(outs, axis=1)
    win_ref[0, 0:wbl - 1, :] = wc_ref[0, 1:wbl, :]
    win_ref[0, wbl - 1:wbl, :] = nwin


def _nsa_sample_attn(pages2, page_table, idx, q3, row3, nwin3, wincache, gates3, oc, past):
    bsz = q3.shape[0]
    wbl = wincache.shape[1]
    ncomplete = past // NSA_BLOCK

    def blk_map(b, pt, ix, j):
        n = jnp.minimum(ix[b, j], ncomplete - 1)
        return (pt[b, n // 2] * 2 + n % 2, 0, 1)

    blk_specs = [pl.BlockSpec((1, NSA_BLOCK, 2 * LANE), functools.partial(blk_map, j=j))
                 for j in range(NSA_KV_HEADS * NSA_TOPK)]
    one = lambda w: pl.BlockSpec((1, 1, w), lambda b, pt, ix: (b, 0, 0))
    return pl.pallas_call(
        functools.partial(_nsa_sample_attn_kernel, past=past),
        grid_spec=pltpu.PrefetchScalarGridSpec(
            num_scalar_prefetch=2,
            grid=(bsz,),
            in_specs=blk_specs + [one(NSA_Q), one(4 * NSA_KV), one(2 * NSA_KV),
                                  pl.BlockSpec((1, wbl, 2 * NSA_KV), lambda b, pt, ix: (b, 0, 0)),
                                  one(SMALL_W),
                                  pl.BlockSpec((1, NSA_HEADS, HEAD_DIM), lambda b, pt, ix: (b, 0, 0))],
            out_specs=[one(NSA_Q), pl.BlockSpec((1, wbl, 2 * NSA_KV), lambda b, pt, ix: (b, 0, 0))]),
        out_shape=[jax.ShapeDtypeStruct((bsz, 1, NSA_Q), F32),
                   jax.ShapeDtypeStruct((bsz, wbl, 2 * NSA_KV), F32)],
        compiler_params=_cparams("parallel"),
        name="nsa_sample_attn",
    )(page_table, idx, *([pages2] * (NSA_KV_HEADS * NSA_TOPK)), q3, row3, nwin3, wincache, gates3, oc)


def _outproj_kernel(og_ref, on_ref, os_ref, x_ref, wo_ref, g_ref, wq_ref, x1_ref, xn_ref, qh_ref):
    x1 = (x_ref[...] + _dot(og_ref[...].astype(BF16), wo_ref[0:GDN_V, :])
          + _dot(on_ref[...].astype(BF16), wo_ref[GDN_V:GDN_V + NSA_Q, :])
          + _dot(os_ref[...].astype(BF16), wo_ref[GDN_V + NSA_Q:, :]))
    x1_ref[...] = x1
    xn = (x1 * lax.rsqrt(jnp.mean(x1 * x1, axis=-1, keepdims=True) + EPS) * g_ref[...]).astype(BF16)
    xn_ref[...] = xn
    qh_ref[...] = _dot(xn, wq_ref[...])


def _outproj(og, on, os_, x, w_out, norm_g, wq):
    n = x.shape[0]
    tm = min(n, 512)
    dq = PEER_HEADS * PEER_DKEY
    tok = lambda w: pl.BlockSpec((tm, w), lambda i: (i, 0))
    const = lambda a, b: pl.BlockSpec((a, b), lambda i: (0, 0))
    return pl.pallas_call(
        _outproj_kernel,
        grid=(n // tm,),
        in_specs=[tok(GDN_V), tok(NSA_Q), tok(SSM_DI), tok(D_MODEL), const(D_MODEL, D_MODEL), const(1, D_MODEL),
                  const(D_MODEL, dq)],
        out_specs=[tok(D_MODEL), tok(D_MODEL), tok(dq)],
        out_shape=[jax.ShapeDtypeStruct((n, D_MODEL), F32), jax.ShapeDtypeStruct((n, D_MODEL), BF16),
                   jax.ShapeDtypeStruct((n, dq), F32)],
        compiler_params=_cparams("parallel"),
        name="outproj",
    )(og, on, os_, x, w_out.astype(BF16), norm_g.reshape(1, D_MODEL), wq.astype(BF16))


PEER_PAIRS = tuple((a, b) for a in range(PEER_TOPK) for b in range(PEER_TOPK) if (a + 1) * (b + 1) <= PEER_TOPK)


def _topk_rows(s, k):
    n = s.shape[0]
    nio = _iota(s.shape, 0)
    avail = jnp.ones(s.shape, jnp.bool_)
    rank = jnp.full(s.shape, float(k), F32)
    vals = []
    for r in range(k):
        m = _colmax(jnp.where(avail, s, -jnp.inf))
        eq = avail & (s == m)
        idx = jnp.min(jnp.where(eq, nio, n), axis=0, keepdims=True)
        pick = nio == idx
        rank = jnp.where(pick, float(r), rank)
        avail = avail & jnp.logical_not(pick)
        vals.append(m)
    return rank, vals


def _peer_keys_kernel(qh_ref, sk_ref, r2_ref, e2_ref, lx_ref, e1_ref):
    half = PEER_DKEY // 2
    for h in range(PEER_HEADS):
        q1 = qh_ref[:, h * PEER_DKEY:h * PEER_DKEY + half]
        q2 = qh_ref[:, h * PEER_DKEY + half:(h + 1) * PEER_DKEY]
        s1 = _dot_nt(sk_ref[0], q1, HI)
        s2 = _dot_nt(sk_ref[1], q2, HI)
        rank1, v1 = _topk_rows(s1, PEER_TOPK)
        rank2, v2 = _topk_rows(s2, PEER_TOPK)
        npad = -len(PEER_PAIRS) % 8
        cand = jnp.concatenate([v1[a] + v2[b] for a, b in PEER_PAIRS]
                               + [jnp.full((npad, s1.shape[1]), -jnp.inf, F32)], axis=0)
        crank, _ = _topk_rows(cand, PEER_TOPK)
        chosen = crank < PEER_TOPK
        z = _colsum(jnp.where(chosen, jnp.exp(cand - cand[0:1]), 0.0))
        chosen_f = jnp.where(chosen, 1.0, 0.0)
        lx = jnp.zeros(s1.shape, F32)
        for a in range(PEER_TOPK):
            rows = [i for i, (pa, _) in enumerate(PEER_PAIRS) if pa == a]
            cnt = chosen_f[rows[0]:rows[0] + 1]
            for i in rows[1:]:
                cnt = cnt + chosen_f[i:i + 1]
            lx = jnp.where(rank1 == float(a), cnt, lx)
        r2_ref[h] = rank2
        e2_ref[h] = jnp.exp(s2 - v2[0]) / z
        lx_ref[h] = lx
        e1_ref[h] = jnp.where(rank1 < PEER_TOPK, jnp.exp(s1 - v1[0]), 0.0)


def _peer_keys(qh, subkeys):
    n = qh.shape[0]
    tk = min(n, 256)
    dq = PEER_HEADS * PEER_DKEY
    half = PEER_DKEY // 2
    out = pl.BlockSpec((PEER_HEADS, PEER_KEYS, tk), lambda i: (0, 0, i))
    return pl.pallas_call(
        _peer_keys_kernel,
        grid=(n // tk,),
        in_specs=[pl.BlockSpec((tk, dq), lambda i: (i, 0)),
                  pl.BlockSpec((2, PEER_KEYS, half), lambda i: (0, 0, 0))],
        out_specs=[out] * 4,
        out_shape=[jax.ShapeDtypeStruct((PEER_HEADS, PEER_KEYS, n), F32)] * 4,
        compiler_params=_cparams("parallel"),
        name="peer_keys",
    )(qh, subkeys)


PEER_AC = 4


def _gelu_tanh(x):
    return 0.5 * x * (1.0 + jnp.tanh(math.sqrt(2.0 / math.pi) * (x + 0.044715 * (x * x * x))))


def _peer_kernel(xn_ref, u_ref, vt_ref, r2_ref, e2_ref, lx_ref, e1_ref, x1_ref, o_ref, acc, *, nsteps):
    j = pl.program_id(1)

    @pl.when(j == 0)
    def _():
        acc[...] = jnp.zeros_like(acc)

    ht = _dot_nt(u_ref[...], xn_ref[...])
    ws = []
    for a in range(PEER_AC):
        row = j * PEER_AC + a
        gate = None
        for h in range(PEER_HEADS):
            term = jnp.where(r2_ref[h] < lx_ref[h, pl.ds(row, 1), :], e2_ref[h], 0.0) * e1_ref[h, pl.ds(row, 1), :]
            gate = term if gate is None else gate + term
        ws.append((gate * _gelu_tanh(ht[a * PEER_KEYS:(a + 1) * PEER_KEYS, :])).astype(BF16))
    acc[...] += _dot(vt_ref[...], jnp.concatenate(ws, axis=0))

    @pl.when(j == nsteps - 1)
    def _():
        o_ref[...] = x1_ref[...] + acc[...].T


def _peer(xn, u_bf, vt_bf, r2, e2, lx, e1, x1):
    n = xn.shape[0]
    tm = min(n, 512)
    ec = PEER_AC * PEER_KEYS
    nsteps = PEER_KEYS // PEER_AC
    tab = pl.BlockSpec((PEER_HEADS, PEER_KEYS, tm), lambda i, j: (0, 0, i))
    return pl.pallas_call(
        functools.partial(_peer_kernel, nsteps=nsteps),
        grid=(n // tm, nsteps),
        in_specs=[pl.BlockSpec((tm, D_MODEL), lambda i, j: (i, 0)),
                  pl.BlockSpec((ec, D_MODEL), lambda i, j: (j, 0)),
                  pl.BlockSpec((D_MODEL, ec), lambda i, j: (0, j)),
                  tab, tab, tab, tab,
                  pl.BlockSpec((tm, D_MODEL), lambda i, j: (i, 0))],
        out_specs=pl.BlockSpec((tm, D_MODEL), lambda i, j: (i, 0)),
        out_shape=jax.ShapeDtypeStruct((n, D_MODEL), F32),
        scratch_shapes=[pltpu.VMEM((D_MODEL, tm), F32)],
        compiler_params=_cparams("parallel", "arbitrary"),
        name="peer",
    )(xn, u_bf, vt_bf, r2, e2, lx, e1, x1)


def _pad_axis(a, axis, size):
    if a.shape[axis] == size:
        return a
    pad = [(0, 0)] * a.ndim
    pad[axis] = (0, size - a.shape[axis])
    return jnp.pad(a, pad)


def _mixers(x2d, bsz, t, pos, conv_hist, gdn0, ssm0, lp, tokens_as_time=False):
    conv_in, z, q, kv, small = _inproj(x2d, lp["norm_mix"], lp["w_cat"])
    r3 = lambda a: a.reshape(bsz, t, -1)
    c, new_conv = _conv(r3(conv_in), conv_hist, lp["conv_w"], lp["conv_b"])
    tpad = -(-t // CHUNK) * CHUNK
    t_valid = None if tpad == t else t
    cp, zp, sp = (_pad_axis(a, 1, tpad) for a in (c, r3(z), r3(small)))
    og, s_gdn = _gdn(cp, zp, sp, gdn0, lp["gdn_a_log"], lp["gdn_dt_bias"], lp["gdn_norm"], t_valid=t_valid)
    os_, s_ssm = _ssd(cp, zp, sp, ssm0, lp["ssm_a_log"], lp["ssm_dt_bias"], lp["ssm_d"], lp["ssm_norm"],
                      t_valid=t_valid)
    lead = (1, bsz * t) if tokens_as_time else (bsz, t)
    qn, rows, win, gates, ks, vs, kw, vw = _nsaproj(q.reshape(*lead, -1), kv.reshape(*lead, -1),
                                                    small.reshape(*lead, -1), pos, lp["nsa_qk_norm"])
    return dict(og=og[:, :t], os=os_[:, :t], new_conv=new_conv, s_gdn=s_gdn, s_ssm=s_ssm, q=r3(qn), rows=r3(rows),
                win=r3(win), gates=r3(gates), ks=ks, vs=vs, kw=kw, vw=vw)


def _channel_mix(og, on, os_, x, lp):
    x1, xn, qh = _outproj(og, on, os_, x, lp["w_out"], lp["norm_ffn"], lp["peer_wq"])
    n = x.shape[0]
    npad = -(-n // LANE) * LANE
    x1, xn, qh = (_pad_axis(a, 0, npad) for a in (x1, xn, qh))
    r2, e2, lx, e1 = _peer_keys(qh, lp["peer_subkeys"])
    return _peer(xn, lp["peer_u"], lp["peer_vt"], r2, e2, lx, e1, x1)[:n]


def kernel(x_prompt, x_sample, cache_nsa_kv, cache_nsa_win, state_conv, state_gdn, state_ssm, page_table, norm_mix, w_in, conv_w, conv_b, gdn_a_log, gdn_dt_bias, gdn_norm, ssm_a_log, ssm_dt_bias, ssm_d, ssm_norm, nsa_qk_norm, nsa_pe, nsa_phi, w_out, norm_ffn, peer_wq, peer_subkeys, peer_u, peer_v):
    bp, tp, _ = x_prompt.shape
    bs, ts, _ = x_sample.shape
    past = page_table.shape[1] * cache_nsa_kv.shape[2]
    n_pool = cache_nsa_kv.shape[1]
    yp = x_prompt.reshape(bp * tp, D_MODEL)
    ys = x_sample.reshape(bs * ts, D_MODEL)
    outs_p, outs_s = [], []
    for l in range(norm_mix.shape[0]):
        lp = dict(norm_mix=norm_mix[l], w_cat=_cat_w_in(w_in[l]), conv_w=conv_w[l], conv_b=conv_b[l],
                  gdn_a_log=gdn_a_log[l], gdn_dt_bias=gdn_dt_bias[l], gdn_norm=gdn_norm[l],
                  ssm_a_log=ssm_a_log[l], ssm_dt_bias=ssm_dt_bias[l], ssm_d=ssm_d[l], ssm_norm=ssm_norm[l],
                  nsa_qk_norm=nsa_qk_norm[l], nsa_pe=nsa_pe[l], nsa_phi=nsa_phi[l], w_out=w_out[l],
                  norm_ffn=norm_ffn[l], peer_wq=peer_wq[l], peer_subkeys=peer_subkeys[l],
                  peer_u=peer_u[l].astype(BF16), peer_vt=peer_v[l].T.astype(BF16))
        m = _mixers(yp, bp, tp, jnp.arange(tp, dtype=jnp.int32), jnp.zeros((bp, CONV_W - 1, CONV_CH), F32),
                    jnp.zeros((bp, GDN_HEADS, GDN_DK, GDN_DV), F32),
                    jnp.zeros((bp, SSM_HEADS, SSM_HEADDIM, SSM_DSTATE), F32), lp)
        pages = m["rows"].reshape(bp * tp // PAGE, PAGE, 4 * NSA_KV)
        ident = jnp.arange(bp * tp // PAGE, dtype=jnp.int32).reshape(bp, tp // PAGE)
        ck, cv = _compress(pages, ident, lp["nsa_pe"], lp["nsa_phi"])
        gates_t = jnp.transpose(m["gates"][..., 12:12 + 3 * NSA_HEADS], (0, 2, 1))
        o_nsa = _nsa_prompt(m["q"], ck, cv, m["ks"], m["vs"], m["kw"], m["vw"], gates_t)
        yp = _channel_mix(m["og"].reshape(bp * tp, -1), o_nsa.reshape(bp * tp, -1), m["os"].reshape(bp * tp, -1),
                          yp, lp)
        keep = min(NSA_WINDOW, tp)
        outs_p.append((m["rows"].reshape(bp, tp, 4, NSA_KV_HEADS, HEAD_DIM),
                       m["win"][:, tp - keep:].reshape(bp, keep, 2, NSA_KV_HEADS, HEAD_DIM),
                       m["new_conv"], m["s_gdn"], m["s_ssm"]))
        m = _mixers(ys, bs, ts, jnp.full((bs * ts,), past, jnp.int32), state_conv[l], state_gdn[l], state_ssm[l], lp,
                    tokens_as_time=True)
        ck, cv = _compress(cache_nsa_kv[l].reshape(n_pool, PAGE, 4 * NSA_KV), page_table, lp["nsa_pe"],
                           lp["nsa_phi"])
        q3 = m["q"].reshape(bs, 1, NSA_Q)
        oc, idx = _nsa_sample_cmp(q3, ck, cv, past)
        wbl = cache_nsa_win.shape[2]
        o_nsa, new_win = _nsa_sample_attn(
            cache_nsa_kv[l].reshape(n_pool * 2, NSA_BLOCK, 4 * NSA_KV), page_table,
            idx.reshape(bs, NSA_KV_HEADS * NSA_TOPK), q3, m["rows"].reshape(bs, 1, 4 * NSA_KV),
            m["win"].reshape(bs, 1, 2 * NSA_KV), cache_nsa_win[l].reshape(bs, wbl, 2 * NSA_KV),
            m["gates"].reshape(bs, 1, SMALL_W), oc, past)
        ys = _channel_mix(m["og"].reshape(bs, -1), o_nsa.reshape(bs, -1), m["os"].reshape(bs, -1), ys, lp)
        outs_s.append((m["rows"].reshape(bs, ts, 4, NSA_KV_HEADS, HEAD_DIM),
                       new_win.reshape(bs, wbl, 2, NSA_KV_HEADS, HEAD_DIM), m["new_conv"], m["s_gdn"], m["s_ssm"]))
    stack = lambda outs, i: jnp.stack([o[i] for o in outs])
    return (yp.reshape(bp, tp, D_MODEL), ys.reshape(bs, ts, D_MODEL),
            stack(outs_p, 0), stack(outs_p, 1), stack(outs_p, 2), stack(outs_p, 3), stack(outs_p, 4),
            stack(outs_s, 0), stack(outs_s, 1), stack(outs_s, 2), stack(outs_s, 3), stack(outs_s, 4))
```

```python
import functools
import math

import jax
import jax.numpy as jnp
from jax import lax
from jax.experimental import pallas as pl
from jax.experimental.pallas import tpu as pltpu

F32 = jnp.float32
BF16 = jnp.bfloat16
HI = lax.Precision.HIGHEST

D_MODEL = 1024
HEAD_DIM = 64
GDN_HEADS = 4
GDN_DK = 64
GDN_DV = 64
CHUNK = 64
NSA_HEADS = 8
NSA_KV_HEADS = 2
NSA_GROUP = 4
NSA_BLOCK = 64
NSA_TOPK = 16
NSA_WINDOW = 512
SSM_HEADS = 4
SSM_HEADDIM = 64
SSM_GROUPS = 2
SSM_DSTATE = 128
CONV_W = 4
PEER_KEYS = 128
PEER_HEADS = 8
PEER_TOPK = 16
PEER_DKEY = 256
ROPE_THETA = 10000.0
EPS = 1e-6

GDN_QK = GDN_HEADS * GDN_DK
GDN_V = GDN_HEADS * GDN_DV
SSM_DI = SSM_HEADS * SSM_HEADDIM
SSM_BC = SSM_GROUPS * SSM_DSTATE
GDN_CONV_CH = 2 * GDN_QK + GDN_V
CONV_CH = GDN_CONV_CH + SSM_DI + 2 * SSM_BC
NSA_Q = NSA_HEADS * HEAD_DIM
NSA_KV = NSA_KV_HEADS * HEAD_DIM
SMALL_W = 128
LANE = 128
VMEM_LIMIT = 48 * 1024 * 1024


def _cparams(*sem):
    return pltpu.CompilerParams(dimension_semantics=sem, vmem_limit_bytes=VMEM_LIMIT)


def _dot(a, b):
    return jnp.dot(a, b, preferred_element_type=F32)


def _split_bf16(a):
    hi = a.astype(BF16)
    return hi, (a - hi.astype(F32)).astype(BF16)


def _dot3_general(a, b, dims):
    a_hi, a_lo = _split_bf16(a)
    b_hi, b_lo = _split_bf16(b)
    dg = lambda x, y: lax.dot_general(x, y, (dims, ((), ())), preferred_element_type=F32)
    return dg(a_hi, b_hi) + dg(a_lo, b_hi) + dg(a_hi, b_lo)


def _dot_hi(a, b):
    return _dot3_general(a, b, ((1,), (0,)))


def _split3(a):
    hi = a.astype(BF16)
    r = a - hi.astype(F32)
    mid = r.astype(BF16)
    return hi, mid, (r - mid.astype(F32)).astype(BF16)


def _dot_sel(sel, x, dims=((1,), (0,)), sel_right=False):
    out = None
    for piece in _split3(x):
        ops = (piece, sel.astype(BF16)) if sel_right else (sel.astype(BF16), piece)
        t = lax.dot_general(*ops, (dims, ((), ())), preferred_element_type=F32)
        out = t if out is None else out + t
    return out


def _dot_nt(a, b, precision=None):
    if precision is not None:
        return _dot3_general(a, b, ((1,), (1,)))
    return lax.dot_general(a, b, (((1,), (1,)), ((), ())), preferred_element_type=F32)


def _dot_tn(a, b, precision=None):
    if precision is not None:
        return _dot3_general(a, b, ((0,), (0,)))
    return lax.dot_general(a, b, (((0,), (0,)), ((), ())), preferred_element_type=F32)


def _sigmoid(x):
    return 1.0 / (1.0 + jnp.exp(-x))


def _softplus(x):
    return jnp.maximum(x, 0.0) + jnp.log1p(jnp.exp(-jnp.abs(x)))


def _iota(shape, dim):
    return lax.broadcasted_iota(jnp.int32, shape, dim)


IN_GROUPS = (("conv", CONV_CH), ("z", GDN_V + SSM_DI), ("q", NSA_Q), ("kv", 6 * NSA_KV), ("small", SMALL_W))
IN_CAT = sum(w for _, w in IN_GROUPS)


def _inproj_kernel(x_ref, g_ref, w_ref, conv_ref, z_ref, q_ref, kv_ref, small_ref):
    x = x_ref[...]
    ms = jnp.mean(x * x, axis=-1, keepdims=True)
    xn = (x * lax.rsqrt(ms + EPS) * g_ref[...]).astype(BF16)
    off = 0
    for ref, (_, w) in zip((conv_ref, z_ref, q_ref, kv_ref, small_ref), IN_GROUPS):
        ref[...] = _dot(xn, w_ref[:, off:off + w])
        off += w


def _inproj(x2d, norm_g, w_cat):
    n = x2d.shape[0]
    tm = min(n, 512)
    return pl.pallas_call(
        _inproj_kernel,
        grid=(n // tm,),
        in_specs=[pl.BlockSpec((tm, D_MODEL), lambda i: (i, 0)),
                  pl.BlockSpec((1, D_MODEL), lambda i: (0, 0)),
                  pl.BlockSpec((D_MODEL, IN_CAT), lambda i: (0, 0))],
        out_specs=[pl.BlockSpec((tm, w), lambda i: (i, 0)) for _, w in IN_GROUPS],
        out_shape=[jax.ShapeDtypeStruct((n, w), F32) for _, w in IN_GROUPS],
        compiler_params=_cparams("parallel"),
        name="inproj",
    )(x2d, norm_g.reshape(1, D_MODEL), w_cat)


def _cat_w_in(w_in):
    o = 0
    parts = {}
    for name, w in (("conv", CONV_CH), ("gdn_z", GDN_V), ("gdn_b", GDN_HEADS), ("gdn_a", GDN_HEADS),
                    ("ssm_z", SSM_DI), ("ssm_dt", SSM_HEADS), ("nsa_q", NSA_Q), ("nsa_kv", 6 * NSA_KV),
                    ("nsa_g", 3 * NSA_HEADS)):
        parts[name] = w_in[:, o:o + w]
        o += w
    small = jnp.concatenate([parts["gdn_b"], parts["gdn_a"], parts["ssm_dt"], parts["nsa_g"]], axis=1)
    small = jnp.pad(small, ((0, 0), (0, SMALL_W - small.shape[1])))
    return jnp.concatenate([parts["conv"], parts["gdn_z"], parts["ssm_z"], parts["nsa_q"], parts["nsa_kv"], small],
                           axis=1).astype(BF16)


HALO = 8


def _conv_kernel(x_ref, hist_ref, w_ref, b_ref, c_ref, nc_ref, xe, *, tm, nt):
    t = pl.program_id(1)

    @pl.when(t == 0)
    def _():
        xe[HALO - 3:HALO, :] = hist_ref[0]

    xe[HALO:HALO + tm, :] = x_ref[0]
    y = b_ref[...] + xe[HALO - 3:HALO - 3 + tm, :] * w_ref[0:1, :]
    for j in range(1, CONV_W):
        y = y + xe[HALO - 3 + j:HALO - 3 + j + tm, :] * w_ref[j:j + 1, :]
    c_ref[0] = y * _sigmoid(y)
    last = xe[HALO + tm - 3:HALO + tm, :]

    @pl.when(t == nt - 1)
    def _():
        nc_ref[0] = last

    xe[HALO - 3:HALO, :] = last


def _conv(x, hist, w, b):
    bsz, t, ch = x.shape
    tm = min(t, 512)
    nt = t // tm
    return pl.pallas_call(
        functools.partial(_conv_kernel, tm=tm, nt=nt),
        grid=(bsz, nt),
        in_specs=[pl.BlockSpec((1, tm, ch), lambda i, j: (i, j, 0)),
                  pl.BlockSpec((1, CONV_W - 1, ch), lambda i, j: (i, 0, 0)),
                  pl.BlockSpec((CONV_W, ch), lambda i, j: (0, 0)),
                  pl.BlockSpec((1, ch), lambda i, j: (0, 0))],
        out_specs=[pl.BlockSpec((1, tm, ch), lambda i, j: (i, j, 0)),
                   pl.BlockSpec((1, CONV_W - 1, ch), lambda i, j: (i, 0, 0))],
        out_shape=[jax.ShapeDtypeStruct((bsz, t, ch), F32),
                   jax.ShapeDtypeStruct((bsz, CONV_W - 1, ch), F32)],
        scratch_shapes=[pltpu.VMEM((HALO + tm, ch), F32)],
        compiler_params=_cparams("parallel", "arbitrary"),
        name="conv",
    )(x, hist, w, b.reshape(1, ch))


def _tri(n, strict=False):
    r, c = _iota((n, n), 0), _iota((n, n), 1)
    return (r > c) if strict else (r >= c)


def _row_of(col):
    n = col.shape[0]
    eye = (_iota((n, n), 0) == _iota((n, n), 1)).astype(F32)
    return _dot_sel(jnp.ones((n, n), F32), eye * col)


def _gdn_kernel(q_ref, k_ref, v_ref, z_ref, small_ref, s0_ref, alog_ref, dtb_ref, ng_ref, o_ref, s_ref, st,
                *, nc, t_valid, bb):
    c = pl.program_id(1)

    @pl.when(c == 0)
    def _():
        st[...] = s0_ref[...]

    incl = _tri(CHUNK)
    strict = _tri(CHUNK, strict=True)
    if t_valid is not None:
        valid = (c * CHUNK + _iota((CHUNK, 1), 0)) < t_valid
    for b in range(bb):
        sm = small_ref[b]
        beta_all = _sigmoid(sm[:, 0:GDN_HEADS])
        g_all = -jnp.exp(alog_ref[...]) * _softplus(sm[:, GDN_HEADS:2 * GDN_HEADS] + dtb_ref[...])
        if t_valid is not None:
            beta_all = jnp.where(valid, beta_all, 0.0)
            g_all = jnp.where(valid, g_all, 0.0)
        gc_all = _dot_sel(incl.astype(F32), g_all)
        outs = []
        for h in range(GDN_HEADS):
            sl = slice(h * GDN_DK, (h + 1) * GDN_DK)
            q = q_ref[b][:, sl]
            k = k_ref[b][:, sl]
            v = v_ref[b][:, sl]
            q = q * lax.rsqrt(jnp.sum(q * q, axis=-1, keepdims=True) + EPS) * (GDN_DK ** -0.5)
            k = k * lax.rsqrt(jnp.sum(k * k, axis=-1, keepdims=True) + EPS)
            if t_valid is not None:
                q = jnp.where(valid, q, 0.0)
                k = jnp.where(valid, k, 0.0)
                v = jnp.where(valid, v, 0.0)
            beta = beta_all[:, h:h + 1]
            gc = gc_all[:, h:h + 1]
            diff = gc - _row_of(gc)
            d_strict = jnp.where(strict, jnp.exp(jnp.where(strict, diff, 0.0)), 0.0)
            d_incl = jnp.where(incl, jnp.exp(jnp.where(incl, diff, 0.0)), 0.0)
            egc = jnp.exp(gc)
            p = -(beta * _dot_nt(k, k, HI) * d_strict)
            x = jnp.concatenate([v * beta, k * (beta * egc)], axis=-1)
            for it in range(6):
                x = x + _dot_hi(p, x)
                if it < 5:
                    p = _dot_hi(p, p)
            u, wk = x[:, :GDN_DV], x[:, GDN_DV:]
            aqk = _dot_nt(q, k, HI) * d_incl
            gl = gc[CHUNK - 1:CHUNK, :]
            kd = k * jnp.exp(gl - gc)
            s = st[b, h]
            w = u - _dot_hi(wk, s)
            o = _dot_hi(q * egc, s) + _dot_hi(aqk, w)
            st[b, h] = jnp.exp(gl) * s + _dot_tn(kd, w, HI)
            o = o * lax.rsqrt(jnp.mean(o * o, axis=-1, keepdims=True) + EPS) * ng_ref[...]
            zh = z_ref[b][:, sl]
            outs.append(o * (zh * _sigmoid(zh)))
        o_ref[b] = jnp.concatenate(outs, axis=-1)

    @pl.when(c == nc - 1)
    def _():
        s_ref[...] = st[...]


GDN_BB = 2


def _gdn(c, z, small, s0, a_log, dt_bias, norm_g, t_valid=None):
    bsz, t, _ = c.shape
    nc = t // CHUNK
    bb = GDN_BB
    return pl.pallas_call(
        functools.partial(_gdn_kernel, nc=nc, t_valid=t_valid, bb=bb),
        grid=(bsz // bb, nc),
        in_specs=[pl.BlockSpec((bb, CHUNK, GDN_QK), lambda i, j: (i, j, 0)),
                  pl.BlockSpec((bb, CHUNK, GDN_QK), lambda i, j: (i, j, 1)),
                  pl.BlockSpec((bb, CHUNK, GDN_V), lambda i, j: (i, j, 2)),
                  pl.BlockSpec((bb, CHUNK, GDN_V), lambda i, j: (i, j, 0)),
                  pl.BlockSpec((bb, CHUNK, SMALL_W), lambda i, j: (i, j, 0)),
                  pl.BlockSpec((bb, GDN_HEADS, GDN_DK, GDN_DV), lambda i, j: (i, 0, 0, 0)),
                  pl.BlockSpec((1, GDN_HEADS), lambda i, j: (0, 0)),
                  pl.BlockSpec((1, GDN_HEADS), lambda i, j: (0, 0)),
                  pl.BlockSpec((1, GDN_DV), lambda i, j: (0, 0))],
        out_specs=[pl.BlockSpec((bb, CHUNK, GDN_V), lambda i, j: (i, j, 0)),
                   pl.BlockSpec((bb, GDN_HEADS, GDN_DK, GDN_DV), lambda i, j: (i, 0, 0, 0))],
        out_shape=[jax.ShapeDtypeStruct((bsz, t, GDN_V), F32),
                   jax.ShapeDtypeStruct((bsz, GDN_HEADS, GDN_DK, GDN_DV), F32)],
        scratch_shapes=[pltpu.VMEM((bb, GDN_HEADS, GDN_DK, GDN_DV), F32)],
        compiler_params=_cparams("parallel", "arbitrary"),
        name="gdn",
    )(c, c, c, z, small, s0, a_log.reshape(1, -1), dt_bias.reshape(1, -1), norm_g.reshape(1, -1))


def _ssd_kernel(x_ref, b_ref, c_ref, z_ref, small_ref, h0_ref, alog_ref, dtb_ref, dsk_ref, ng_ref, o_ref, h_ref, st,
                *, nc, t_valid):
    ci = pl.program_id(1)

    @pl.when(ci == 0)
    def _():
        st[...] = h0_ref[0]

    sm = small_ref[0]
    dt_all = _softplus(sm[:, 2 * GDN_HEADS:2 * GDN_HEADS + SSM_HEADS] + dtb_ref[...])
    xs = x_ref[0]
    bm_all = b_ref[0]
    cm_all = c_ref[0]
    if t_valid is not None:
        valid = (ci * CHUNK + _iota((CHUNK, 1), 0)) < t_valid
        dt_all = jnp.where(valid, dt_all, 0.0)
        xs = jnp.where(valid, xs, 0.0)
        bm_all = jnp.where(valid, bm_all, 0.0)
        cm_all = jnp.where(valid, cm_all, 0.0)
    incl = _tri(CHUNK)
    cs_all = _dot_sel(incl.astype(F32), dt_all * (-jnp.exp(alog_ref[...])))
    rep = SSM_HEADS // SSM_GROUPS
    ys = []
    for h in range(SSM_HEADS):
        g = h // rep
        x = xs[:, h * SSM_HEADDIM:(h + 1) * SSM_HEADDIM]
        bm = bm_all[:, g * SSM_DSTATE:(g + 1) * SSM_DSTATE]
        cm = cm_all[:, g * SSM_DSTATE:(g + 1) * SSM_DSTATE]
        dt = dt_all[:, h:h + 1]
        cs = cs_all[:, h:h + 1]
        diff = cs - _row_of(cs)
        seg = jnp.where(incl, jnp.exp(jnp.where(incl, diff, 0.0)), 0.0)
        xdt = x * dt
        cb = _dot_nt(cm, bm, HI) * seg
        cl = cs[CHUNK - 1:CHUNK, :]
        hs = st[h]
        y = _dot_hi(cb, xdt) + _dot_nt(cm * jnp.exp(cs), hs, HI)
        st[h] = jnp.exp(cl) * hs + _dot_tn(xdt * jnp.exp(cl - cs), bm, HI)
        ys.append(y + dsk_ref[0:1, h:h + 1] * x)
    y = jnp.concatenate(ys, axis=-1)
    zz = z_ref[0]
    y = y * (zz * _sigmoid(zz))
    gs = SSM_DI // SSM_GROUPS
    outs = []
    for g in range(SSM_GROUPS):
        yg = y[:, g * gs:(g + 1) * gs]
        outs.append(yg * lax.rsqrt(jnp.mean(yg * yg, axis=-1, keepdims=True) + EPS) * ng_ref[0:1, g * gs:(g + 1) * gs])
    o_ref[0] = jnp.concatenate(outs, axis=-1)

    @pl.when(ci == nc - 1)
    def _():
        h_ref[0] = st[...]


def _ssd(c, z, small, h0, a_log, dt_bias, d_skip, norm_g, t_valid=None):
    bsz, t, _ = c.shape
    nc = t // CHUNK
    w = SSM_DI
    return pl.pallas_call(
        functools.partial(_ssd_kernel, nc=nc, t_valid=t_valid),
        grid=(bsz, nc),
        in_specs=[pl.BlockSpec((1, CHUNK, w), lambda i, j: (i, j, 3)),
                  pl.BlockSpec((1, CHUNK, w), lambda i, j: (i, j, 4)),
                  pl.BlockSpec((1, CHUNK, w), lambda i, j: (i, j, 5)),
                  pl.BlockSpec((1, CHUNK, w), lambda i, j: (i, j, 1)),
                  pl.BlockSpec((1, CHUNK, SMALL_W), lambda i, j: (i, j, 0)),
                  pl.BlockSpec((1, SSM_HEADS, SSM_HEADDIM, SSM_DSTATE), lambda i, j: (i, 0, 0, 0)),
                  pl.BlockSpec((1, SSM_HEADS), lambda i, j: (0, 0)),
                  pl.BlockSpec((1, SSM_HEADS), lambda i, j: (0, 0)),
                  pl.BlockSpec((1, SSM_HEADS), lambda i, j: (0, 0)),
                  pl.BlockSpec((1, SSM_DI), lambda i, j: (0, 0))],
        out_specs=[pl.BlockSpec((1, CHUNK, SSM_DI), lambda i, j: (i, j, 0)),
                   pl.BlockSpec((1, SSM_HEADS, SSM_HEADDIM, SSM_DSTATE), lambda i, j: (i, 0, 0, 0))],
        out_shape=[jax.ShapeDtypeStruct((bsz, t, SSM_DI), F32),
                   jax.ShapeDtypeStruct((bsz, SSM_HEADS, SSM_HEADDIM, SSM_DSTATE), F32)],
        scratch_shapes=[pltpu.VMEM((SSM_HEADS, SSM_HEADDIM, SSM_DSTATE), F32)],
        compiler_params=_cparams("parallel", "arbitrary"),
        name="ssd",
    )(c, c, c, z, small, h0, a_log.reshape(1, -1), dt_bias.reshape(1, -1), d_skip.reshape(1, -1),
      norm_g.reshape(1, -1))


PAGE = 128
NEG = -1e30


def _rope_tables(pos):
    half = HEAD_DIM // 2
    inv = ROPE_THETA ** (-jnp.arange(half, dtype=F32) * 2.0 / HEAD_DIM)
    ang = pos.astype(F32)[:, None] * inv[None, :]
    cos, sin = jnp.cos(ang), jnp.sin(ang)
    return jnp.tile(cos, (1, 4)), jnp.tile(jnp.concatenate([-sin, sin], axis=1), (1, 2))


def _nsaproj_kernel(q_ref, kv_ref, small_ref, cos_ref, sin_ref, qn_ref, bd_ref,
                    qo_ref, rows_ref, win_ref, gates_ref, ks_ref, vs_ref, kw_ref, vw_ref):
    cosf = cos_ref[...]
    sins = sin_ref[...]
    bd = bd_ref[...]
    first_half = (_iota((1, LANE), 1) % HEAD_DIM) < (HEAD_DIM // 2)

    def normrope(x, g):
        y = x * lax.rsqrt(_dot_hi(x * x, bd) + EPS) * g
        rot = jnp.where(first_half, pltpu.roll(y, LANE - HEAD_DIM // 2, 1), pltpu.roll(y, HEAD_DIM // 2, 1))
        return y * cosf + rot * sins

    qr = q_ref[0]
    qo_ref[0] = jnp.concatenate(
        [normrope(qr[:, i * LANE:(i + 1) * LANE], qn_ref[0:1, :]) * (HEAD_DIM ** -0.5) for i in range(NSA_Q // LANE)],
        axis=-1)
    kv = kv_ref[0]
    kc = normrope(kv[:, 0:LANE], qn_ref[1:2, :])
    vc = kv[:, LANE:2 * LANE]
    ks = normrope(kv[:, 2 * LANE:3 * LANE], qn_ref[2:3, :])
    vs = kv[:, 3 * LANE:4 * LANE]
    kw = normrope(kv[:, 4 * LANE:5 * LANE], qn_ref[3:4, :])
    vw = kv[:, 5 * LANE:6 * LANE]
    rows_ref[0] = jnp.concatenate([kc, vc, ks, vs], axis=-1)
    win_ref[0] = jnp.concatenate([kw, vw], axis=-1)
    gates_ref[0] = _sigmoid(small_ref[0])
    for h in range(NSA_KV_HEADS):
        sl = slice(h * HEAD_DIM, (h + 1) * HEAD_DIM)
        ks_ref[0, h] = ks[:, sl].astype(BF16)
        vs_ref[0, h] = vs[:, sl].astype(BF16)
        kw_ref[0, h] = kw[:, sl].astype(BF16)
        vw_ref[0, h] = vw[:, sl].astype(BF16)


def _nsaproj(q_raw, kv_raw, small, pos, qk_norm):
    bsz, t, _ = q_raw.shape
    tm = min(t, 512)
    cos, sin = _rope_tables(pos)
    qn = jnp.tile(qk_norm, (1, 2))
    bd = jnp.kron(jnp.eye(2, dtype=F32), jnp.full((HEAD_DIM, HEAD_DIM), 1.0 / HEAD_DIM, F32))
    tok = lambda w: pl.BlockSpec((1, tm, w), lambda i, j: (i, j, 0))
    headed = pl.BlockSpec((1, NSA_KV_HEADS, tm, HEAD_DIM), lambda i, j: (i, 0, j, 0))
    return pl.pallas_call(
        _nsaproj_kernel,
        grid=(bsz, t // tm),
        in_specs=[tok(NSA_Q), tok(6 * NSA_KV), tok(SMALL_W),
                  pl.BlockSpec((tm, LANE), lambda i, j: (j, 0)),
                  pl.BlockSpec((tm, LANE), lambda i, j: (j, 0)),
                  pl.BlockSpec((4, LANE), lambda i, j: (0, 0)),
                  pl.BlockSpec((LANE, LANE), lambda i, j: (0, 0))],
        out_specs=[tok(NSA_Q), tok(4 * NSA_KV), tok(2 * NSA_KV), tok(SMALL_W), headed, headed, headed, headed],
        out_shape=[jax.ShapeDtypeStruct((bsz, t, NSA_Q), F32),
                   jax.ShapeDtypeStruct((bsz, t, 4 * NSA_KV), F32),
                   jax.ShapeDtypeStruct((bsz, t, 2 * NSA_KV), F32),
                   jax.ShapeDtypeStruct((bsz, t, SMALL_W), F32)]
                  + [jax.ShapeDtypeStruct((bsz, NSA_KV_HEADS, t, HEAD_DIM), BF16)] * 4,
        compiler_params=_cparams("parallel", "parallel"),
        name="nsaproj",
    )(q_raw, kv_raw, small, cos, sin, qn, bd)


CMP_PAGES = 16


def _compress_kernel(pt_ref, *refs):
    del pt_ref
    page_refs = refs[:CMP_PAGES]
    pe_ref, phi_ref, ck_ref, cv_ref, bufk, bufv = refs[CMP_PAGES:]
    for i, r in enumerate(page_refs):
        bufk[i * PAGE:(i + 1) * PAGE, :] = r[0, :, 0:LANE]
        bufv[i * PAGE:(i + 1) * PAGE, :] = r[0, :, LANE:2 * LANE]
    nblk = CMP_PAGES * PAGE // NSA_BLOCK

    def body(l, carry):
        ak, av = carry
        kl = bufk[pl.ds(l, nblk, stride=NSA_BLOCK), :] + pe_ref[0, pl.ds(l, 1), :]
        vl = bufv[pl.ds(l, nblk, stride=NSA_BLOCK), :] + pe_ref[1, pl.ds(l, 1), :]
        return ak + _dot_hi(kl, phi_ref[0, l]), av + _dot_hi(vl, phi_ref[1, l])

    zero = jnp.zeros((nblk, LANE), F32)
    ak, av = lax.fori_loop(0, NSA_BLOCK, body, (zero, zero))
    ck_ref[0] = ak
    cv_ref[0] = av


def _compress(pages, page_table, pe, phi):
    bsz, npages = page_table.shape
    steps = npages // CMP_PAGES
    nblk = CMP_PAGES * PAGE // NSA_BLOCK
    pe2 = jnp.tile(pe, (1, 1, 2))
    eye2 = jnp.eye(2, dtype=F32)
    phi_bd = jnp.einsum('ab,klde->kladbe', eye2, phi).reshape(2, NSA_BLOCK, LANE, LANE)
    page_specs = [pl.BlockSpec((1, PAGE, 2 * LANE), lambda b, j, pt, i=i: (pt[b, j * CMP_PAGES + i], 0, 0))
                  for i in range(CMP_PAGES)]
    return pl.pallas_call(
        _compress_kernel,
        grid_spec=pltpu.PrefetchScalarGridSpec(
            num_scalar_prefetch=1,
            grid=(bsz, steps),
            in_specs=page_specs + [pl.BlockSpec((2, NSA_BLOCK, LANE), lambda b, j, pt: (0, 0, 0)),
                                   pl.BlockSpec((2, NSA_BLOCK, LANE, LANE), lambda b, j, pt: (0, 0, 0, 0))],
            out_specs=[pl.BlockSpec((1, nblk, LANE), lambda b, j, pt: (b, j, 0)),
                       pl.BlockSpec((1, nblk, LANE), lambda b, j, pt: (b, j, 0))],
            scratch_shapes=[pltpu.VMEM((CMP_PAGES * PAGE, LANE), F32)] * 2),
        out_shape=[jax.ShapeDtypeStruct((bsz, steps * nblk, LANE), F32)] * 2,
        compiler_params=_cparams("parallel", "parallel"),
        name="compress",
    )(page_table, *([pages] * CMP_PAGES), pe2, phi_bd)


QBLK = 128
KTILE = 512


def _colmax(x):
    return jnp.max(x, axis=0, keepdims=True)


def _colsum(x):
    return jnp.sum(x, axis=0, keepdims=True)


def _tile4(x):
    return jnp.concatenate([x] * NSA_GROUP, axis=1)


def _select_blocks(imp, q0, nb):
    nio = _iota(imp.shape, 0)
    nio_f = nio.astype(F32)
    cur = (q0 + _iota((1, imp.shape[1]), 1)) // NSA_BLOCK
    forced = (nio == 0) | (nio == cur) | (nio == cur - 1)
    causal = nio <= cur
    work = jnp.where(forced, jnp.inf, jnp.where(causal, imp, -jnp.inf))
    sel = jnp.zeros(imp.shape, F32)
    for _ in range(min(NSA_TOPK, nb)):
        m = _colmax(work)
        idx = jnp.min(jnp.where(work == m, nio_f, float(nb)), axis=0, keepdims=True)
        pick = nio_f == idx
        sel = jnp.where(pick, 1.0, sel)
        work = jnp.where(pick, -jnp.inf, work)
    return jnp.where(causal, sel, 0.0)


def _nsa_prompt_kernel(q_ref, ck_ref, cv_ref, ks_ref, vs_ref, kw_ref, vw_ref, gt_ref, o_ref, sel_sc, *, t_total):
    nb = t_total // NSA_BLOCK
    kt = min(KTILE, t_total)
    wk = min(NSA_WINDOW + QBLK, t_total)
    bpt = kt // NSA_BLOCK
    qi = pl.program_id(1)
    q0 = qi * QBLK
    qall = q_ref[0]
    gt = gt_ref[0]
    qpos = _tile4(q0 + _iota((1, QBLK), 1))
    eye = (_iota((HEAD_DIM, HEAD_DIM), 0) == _iota((HEAD_DIM, HEAD_DIM), 1)).astype(BF16)
    outs = []
    for h in range(NSA_KV_HEADS):
        qs = jnp.concatenate([qall[:, (h * NSA_GROUP + g) * HEAD_DIM:(h * NSA_GROUP + g + 1) * HEAD_DIM]
                              for g in range(NSA_GROUP)], axis=0)
        qsb = qs.astype(BF16)
        sl = slice(h * HEAD_DIM, (h + 1) * HEAD_DIM)
        s = _dot_nt(ck_ref[0][:, sl], qs, HI)
        mask = ((_iota((nb, 1), 0) + 1) * NSA_BLOCK - 1) <= qpos
        m = _colmax(jnp.where(mask, s, NEG))
        p = jnp.where(mask, jnp.exp(s - m), 0.0)
        p = p / jnp.maximum(_colsum(p), 1e-30)
        oc = _dot_tn(cv_ref[0][:, sl], p, HI)
        imp = p[:, 0:QBLK]
        for g in range(1, NSA_GROUP):
            imp = imp + p[:, g * QBLK:(g + 1) * QBLK]
        sel_sc[...] = _select_blocks(imp, q0, nb)

        def body(i, carry):
            m_i, l_i, acc = carry
            k0 = pl.multiple_of(i * kt, kt)
            kk = ks_ref[0, h, pl.ds(k0, kt), :]
            vv = vs_ref[0, h, pl.ds(k0, kt), :]
            sc = _dot_nt(kk, qsb)
            selrows = sel_sc[pl.ds(pl.multiple_of(i * bpt, bpt), bpt), :]
            selx = jnp.concatenate([jnp.broadcast_to(selrows[j:j + 1, :], (NSA_BLOCK, QBLK)) for j in range(bpt)],
                                   axis=0)
            kpos = k0 + _iota((kt, 1), 0)
            msk = (_tile4(selx) > 0.5) & (kpos <= qpos)
            m_n = jnp.maximum(m_i, _colmax(jnp.where(msk, sc, NEG)))
            alpha = jnp.exp(m_i - m_n)
            pp = jnp.where(msk, jnp.exp(sc - m_n), 0.0)
            return m_n, l_i * alpha + _colsum(pp), acc * alpha + _dot_tn(vv, pp.astype(BF16))

        ntile = (q0 + QBLK + kt - 1) // kt
        init = (jnp.full((1, NSA_GROUP * QBLK), NEG, F32), jnp.zeros((1, NSA_GROUP * QBLK), F32),
                jnp.zeros((HEAD_DIM, NSA_GROUP * QBLK), F32))
        _, l_s, acc_s = lax.fori_loop(0, ntile, body, init)
        osel = acc_s / jnp.maximum(l_s, 1e-30)
        w0 = pl.multiple_of(jnp.maximum(q0 + QBLK - wk, 0), QBLK)
        sw = _dot_nt(kw_ref[0, h, pl.ds(w0, wk), :], qsb)
        delta = qpos - (w0 + _iota((wk, 1), 0))
        mw = (delta >= 0) & (delta < NSA_WINDOW)
        m = _colmax(jnp.where(mw, sw, NEG))
        pw = jnp.where(mw, jnp.exp(sw - m), 0.0)
        ow = _dot_tn(vw_ref[0, h, pl.ds(w0, wk), :], pw.astype(BF16)) / jnp.maximum(_colsum(pw), 1e-30)
        gate = [jnp.concatenate([gt[(h * NSA_GROUP + g) * 3 + j:(h * NSA_GROUP + g) * 3 + j + 1, :]
                                 for g in range(NSA_GROUP)], axis=1) for j in range(3)]
        ot = gate[0] * oc + gate[1] * osel + gate[2] * ow
        for g in range(NSA_GROUP):
            outs.append(_dot_tn(ot[:, g * QBLK:(g + 1) * QBLK].astype(BF16), eye))
    o_ref[0] = jnp.concatenate(outs, axis=-1)


def _nsa_prompt(q, ck, cv, ks, vs, kw, vw, gates_t):
    bsz, t, _ = q.shape
    nb = t // NSA_BLOCK
    full_kv = pl.BlockSpec((1, NSA_KV_HEADS, t, HEAD_DIM), lambda i, j: (i, 0, 0, 0))
    return pl.pallas_call(
        functools.partial(_nsa_prompt_kernel, t_total=t),
        grid=(bsz, t // QBLK),
        in_specs=[pl.BlockSpec((1, QBLK, NSA_Q), lambda i, j: (i, j, 0)),
                  pl.BlockSpec((1, nb, LANE), lambda i, j: (i, 0, 0)),
                  pl.BlockSpec((1, nb, LANE), lambda i, j: (i, 0, 0)),
                  full_kv, full_kv, full_kv, full_kv,
                  pl.BlockSpec((1, 3 * NSA_HEADS, QBLK), lambda i, j: (i, 0, j))],
        out_specs=pl.BlockSpec((1, QBLK, NSA_Q), lambda i, j: (i, j, 0)),
        out_shape=jax.ShapeDtypeStruct((bsz, t, NSA_Q), F32),
        scratch_shapes=[pltpu.VMEM((nb, QBLK), F32)],
        compiler_params=_cparams("parallel", "arbitrary"),
        name="nsa_prompt",
    )(q, ck, cv, ks, vs, kw, vw, gates_t)


def _group_queries(q, h):
    return jnp.concatenate([q[:, (h * NSA_GROUP + g) * HEAD_DIM:(h * NSA_GROUP + g + 1) * HEAD_DIM]
                            for g in range(NSA_GROUP)] + [jnp.zeros((8 - NSA_GROUP, HEAD_DIM), F32)], axis=0)


def _dot_x3(a, w_hi, w_lo):
    a_hi, a_lo = _split_bf16(a)
    return _dot(a_hi, w_hi) + _dot(a_lo, w_hi) + _dot(a_hi, w_lo)


def _compress_t_kernel(pt_ref, *refs, npg):
    del pt_ref
    page_refs = refs[:npg]
    pe_ref, whi_ref, wlo_ref, ck_ref, cv_ref, buf = refs[npg:]
    rows = 2 * LANE
    for i, r in enumerate(page_refs):
        buf[i * rows:(i + 1) * rows, :] = r[0]

    def body(d, carry):
        accs = list(carry)
        for kv in range(2):
            a = jnp.concatenate([buf[pl.ds(kv * LANE + h * HEAD_DIM + d, npg, stride=rows), :]
                                 for h in range(NSA_KV_HEADS)], axis=0) + pe_ref[kv, pl.ds(d, 1), :]
            accs[kv] = accs[kv] + _dot_x3(a, whi_ref[kv, d], wlo_ref[kv, d])
        return tuple(accs)

    zero = jnp.zeros((NSA_KV_HEADS * npg, LANE), F32)
    ak, av = lax.fori_loop(0, HEAD_DIM, body, (zero, zero))
    ck_ref[0] = ak
    cv_ref[0] = av


def _compress_t(cache_t, page_table, pe, phi):
    bsz, npg = page_table.shape
    eye2 = jnp.eye(2, dtype=F32)
    w = jnp.einsum('ab,klde->kdalbe', eye2, phi).reshape(2, HEAD_DIM, LANE, LANE)
    w_hi = w.astype(BF16)
    w_lo = (w - w_hi.astype(F32)).astype(BF16)
    pe_t = jnp.tile(jnp.transpose(pe, (0, 2, 1)), (1, 1, 2))
    page_specs = [pl.BlockSpec((1, 2 * LANE, PAGE), lambda b, pt, i=i: (pt[b, i], 0, 0)) for i in range(npg)]
    const = lambda shape: pl.BlockSpec(shape, lambda b, pt: (0,) * len(shape))
    out = pl.BlockSpec((1, NSA_KV_HEADS * npg, LANE), lambda b, pt: (b, 0, 0))
    return pl.pallas_call(
        functools.partial(_compress_t_kernel, npg=npg),
        grid_spec=pltpu.PrefetchScalarGridSpec(
            num_scalar_prefetch=1,
            grid=(bsz,),
            in_specs=page_specs + [const((2, HEAD_DIM, LANE)), const((2, HEAD_DIM, LANE, LANE)),
                                   const((2, HEAD_DIM, LANE, LANE))],
            out_specs=[out, out],
            scratch_shapes=[pltpu.VMEM((npg * 2 * LANE, PAGE), F32)]),
        out_shape=[jax.ShapeDtypeStruct((bsz, NSA_KV_HEADS * npg, LANE), F32)] * 2,
        compiler_params=_cparams("parallel"),
        name="compress_t",
    )(page_table, *([cache_t] * npg), pe_t, w_hi, w_lo)


def _nsa_sample_cmp_kernel(q_ref, ck_ref, cv_ref, oc_ref, idx_ref, *, past, npg):
    nbc = 2 * npg
    cur = past // NSA_BLOCK
    q = q_ref[0]
    lane = _iota((1, nbc), 1)
    blk = 2 * (lane % npg) + lane // npg
    mask = ((blk + 1) * NSA_BLOCK - 1) <= past
    ocs, idxs = [], []
    for h in range(NSA_KV_HEADS):
        qh = _group_queries(q, h)
        ckh = ck_ref[0][h * npg:(h + 1) * npg, :]
        cvh = cv_ref[0][h * npg:(h + 1) * npg, :]
        s = jnp.concatenate([_dot_nt(qh, ckh[:, i * HEAD_DIM:(i + 1) * HEAD_DIM], HI) for i in range(2)], axis=1)
        m = jnp.max(jnp.where(mask, s, NEG), axis=1, keepdims=True)
        p = jnp.where(mask, jnp.exp(s - m), 0.0)
        p = p / jnp.maximum(jnp.sum(p, axis=1, keepdims=True), 1e-30)
        oc = _dot_hi(p[:, 0:npg], cvh[:, 0:HEAD_DIM]) + _dot_hi(p[:, npg:nbc], cvh[:, HEAD_DIM:LANE])
        ocs.append(oc[0:NSA_GROUP])
        imp = p[0:1]
        for g in range(1, NSA_GROUP):
            imp = imp + p[g:g + 1]
        forced = (blk == 0) | (blk == cur) | (blk == cur - 1)
        score = jnp.where(forced, jnp.inf, jnp.where(blk <= cur, imp, -jnp.inf))
        avail = jnp.ones((1, nbc), jnp.bool_)
        slot = _iota((1, NSA_TOPK), 1)
        picked = jnp.full((1, NSA_TOPK), cur, jnp.int32)
        for r in range(NSA_TOPK - 1):
            mm = jnp.max(jnp.where(avail, score, -jnp.inf), axis=1, keepdims=True)
            eq = avail & (score == mm)
            idx = jnp.min(jnp.where(eq, blk, nbc), axis=1, keepdims=True)
            avail = avail & (blk != idx)
            picked = jnp.where(slot == r, idx, picked)
        idxs.append(picked)
    oc_ref[0] = jnp.concatenate(ocs, axis=0)
    idx_ref[0] = jnp.concatenate(idxs, axis=1)


def _nsa_sample_cmp(q3, ck, cv, past):
    bsz = q3.shape[0]
    npg = ck.shape[1] // NSA_KV_HEADS
    return pl.pallas_call(
        functools.partial(_nsa_sample_cmp_kernel, past=past, npg=npg),
        grid=(bsz,),
        in_specs=[pl.BlockSpec((1, 1, NSA_Q), lambda b: (b, 0, 0)),
                  pl.BlockSpec((1, NSA_KV_HEADS * npg, LANE), lambda b: (b, 0, 0)),
                  pl.BlockSpec((1, NSA_KV_HEADS * npg, LANE), lambda b: (b, 0, 0))],
        out_specs=[pl.BlockSpec((1, NSA_HEADS, HEAD_DIM), lambda b: (b, 0, 0)),
                   pl.BlockSpec((1, 1, NSA_KV_HEADS * NSA_TOPK), lambda b: (b, 0, 0))],
        out_shape=[jax.ShapeDtypeStruct((bsz, NSA_HEADS, HEAD_DIM), F32),
                   jax.ShapeDtypeStruct((bsz, 1, NSA_KV_HEADS * NSA_TOPK), jnp.int32)],
        compiler_params=_cparams("parallel"),
        name="nsa_sample_cmp",
    )(q3, ck, cv)


def _nsa_sample_attn_kernel(pt_ref, idx_ref, *refs, past):
    del pt_ref
    nsel = NSA_KV_HEADS * NSA_TOPK
    blk_refs = refs[:nsel]
    q_ref, row_ref, nwin_ref, wc_ref, g_ref, oc_ref, o_ref, win_ref = refs[nsel:]
    b = pl.program_id(0)
    wbl = wc_ref.shape[2]
    q = q_ref[0]
    row = row_ref[0]
    nwin = nwin_ref[0]
    gates = g_ref[0]
    oc = oc_ref[0]
    half_of_lane = _iota((1, PAGE), 1) // NSA_BLOCK
    outs = []
    for h in range(NSA_KV_HEADS):
        sl = slice(h * HEAD_DIM, (h + 1) * HEAD_DIM)
        vsl = slice(LANE + h * HEAD_DIM, LANE + (h + 1) * HEAD_DIM)
        qh = _group_queries(q, h)
        ks_new = row[:, 2 * LANE + h * HEAD_DIM:2 * LANE + (h + 1) * HEAD_DIM]
        vs_new = row[:, 3 * LANE + h * HEAD_DIM:3 * LANE + (h + 1) * HEAD_DIM]
        s_new = jnp.sum(qh * ks_new, axis=1, keepdims=True)
        ss, ms = [], []
        for k in range(NSA_TOPK - 1):
            ss.append(_dot_hi(qh, blk_refs[h * NSA_TOPK + k][0, sl, :]))
            ms.append(half_of_lane == idx_ref[b, h * NSA_TOPK + k] % 2)
        m = s_new
        for s, mk in zip(ss, ms):
            m = jnp.maximum(m, jnp.max(jnp.where(mk, s, NEG), axis=1, keepdims=True))
        p_new = jnp.exp(s_new - m)
        den = p_new
        acc = p_new * vs_new
        for k, (s, mk) in enumerate(zip(ss, ms)):
            p = jnp.where(mk, jnp.exp(s - m), 0.0)
            den = den + jnp.sum(p, axis=1, keepdims=True)
            acc = acc + _dot_nt(p, blk_refs[h * NSA_TOPK + k][0, vsl, :], HI)
        osel = acc / jnp.maximum(den, 1e-30)
        sw = _dot_hi(qh, wc_ref[0, sl, :])
        delta = wbl - _iota((1, wbl), 1)
        mw = (delta >= 0) & (delta < NSA_WINDOW) & ((past - delta) >= 0)
        kw_new = nwin[:, sl]
        vw_new = nwin[:, vsl]
        sw_new = jnp.sum(qh * kw_new, axis=1, keepdims=True)
        m = jnp.maximum(jnp.max(jnp.where(mw, sw, NEG), axis=1, keepdims=True), sw_new)
        pw = jnp.where(mw, jnp.exp(sw - m), 0.0)
        pw_new = jnp.exp(sw_new - m)
        ow = (_dot_nt(pw, wc_ref[0, vsl, :], HI) + pw_new * vw_new) / jnp.maximum(
            jnp.sum(pw, axis=1, keepdims=True) + pw_new, 1e-30)
        for g in range(NSA_GROUP):
            c0 = 12 + (h * NSA_GROUP + g) * 3
            outs.append(gates[:, c0:c0 + 1] * oc[h * NSA_GROUP + g:h * NSA_GROUP + g + 1, :]
                        + gates[:, c0 + 1:c0 + 2] * osel[g:g + 1, :] + gates[:, c0 + 2:c0 + 3] * ow[g:g + 1, :])
    o_ref[0] = jnp.concatenate(outs, axis=1)
    new_col = _dot_sel(jnp.ones((8, wbl), F32), jnp.concatenate([nwin, jnp.zeros((7, 2 * NSA_KV), F32)], axis=0),
                       ((0,), (0,)), sel_right=True)
    win_ref[0] = jnp.where(_iota((1, wbl), 1) == wbl - 1, new_col, pltpu.roll(wc_ref[0], wbl - 1, 1))


def _nsa_sample_attn(cache_t, page_table, idx, q3, row3, nwin3, win_t, gates3, oc, past):
    bsz = q3.shape[0]
    wbl = win_t.shape[2]
    ncomplete = past // NSA_BLOCK

    def blk_map(b, pt, ix, j):
        n = jnp.minimum(ix[b, j], ncomplete - 1)
        return (pt[b, n // 2], 1, 0)

    blk_specs = [pl.BlockSpec((1, 2 * LANE, PAGE), functools.partial(blk_map, j=j))
                 for j in range(NSA_KV_HEADS * NSA_TOPK)]
    one = lambda w: pl.BlockSpec((1, 1, w), lambda b, pt, ix: (b, 0, 0))
    win_spec = pl.BlockSpec((1, 2 * NSA_KV, wbl), lambda b, pt, ix: (b, 0, 0))
    return pl.pallas_call(
        functools.partial(_nsa_sample_attn_kernel, past=past),
        grid_spec=pltpu.PrefetchScalarGridSpec(
            num_scalar_prefetch=2,
            grid=(bsz,),
            in_specs=blk_specs + [one(NSA_Q), one(4 * NSA_KV), one(2 * NSA_KV), win_spec, one(SMALL_W),
                                  pl.BlockSpec((1, NSA_HEADS, HEAD_DIM), lambda b, pt, ix: (b, 0, 0))],
            out_specs=[one(NSA_Q), win_spec]),
        out_shape=[jax.ShapeDtypeStruct((bsz, 1, NSA_Q), F32),
                   jax.ShapeDtypeStruct((bsz, 2 * NSA_KV, wbl), F32)],
        compiler_params=_cparams("parallel"),
        name="nsa_sample_attn",
    )(page_table, idx, *([cache_t] * (NSA_KV_HEADS * NSA_TOPK)), q3, row3, nwin3, win_t, gates3, oc)


def _outproj_kernel(og_ref, on_ref, os_ref, x_ref, wo_ref, g_ref, wq_ref, x1_ref, xn_ref, qh_ref):
    x1 = (x_ref[...] + _dot(og_ref[...].astype(BF16), wo_ref[0:GDN_V, :])
          + _dot(on_ref[...].astype(BF16), wo_ref[GDN_V:GDN_V + NSA_Q, :])
          + _dot(os_ref[...].astype(BF16), wo_ref[GDN_V + NSA_Q:, :]))
    x1_ref[...] = x1
    xn = (x1 * lax.rsqrt(jnp.mean(x1 * x1, axis=-1, keepdims=True) + EPS) * g_ref[...]).astype(BF16)
    xn_ref[...] = xn
    qh_ref[...] = _dot(xn, wq_ref[...])


def _outproj(og, on, os_, x, w_out, norm_g, wq):
    n = x.shape[0]
    tm = min(n, 512)
    dq = PEER_HEADS * PEER_DKEY
    tok = lambda w: pl.BlockSpec((tm, w), lambda i: (i, 0))
    const = lambda a, b: pl.BlockSpec((a, b), lambda i: (0, 0))
    return pl.pallas_call(
        _outproj_kernel,
        grid=(n // tm,),
        in_specs=[tok(GDN_V), tok(NSA_Q), tok(SSM_DI), tok(D_MODEL), const(D_MODEL, D_MODEL), const(1, D_MODEL),
                  const(D_MODEL, dq)],
        out_specs=[tok(D_MODEL), tok(D_MODEL), tok(dq)],
        out_shape=[jax.ShapeDtypeStruct((n, D_MODEL), F32), jax.ShapeDtypeStruct((n, D_MODEL), BF16),
                   jax.ShapeDtypeStruct((n, dq), F32)],
        compiler_params=_cparams("parallel"),
        name="outproj",
    )(og, on, os_, x, w_out.astype(BF16), norm_g.reshape(1, D_MODEL), wq.astype(BF16))


PEER_PAIRS = tuple((a, b) for a in range(PEER_TOPK) for b in range(PEER_TOPK) if (a + 1) * (b + 1) <= PEER_TOPK)


def _topk_rows(s, k):
    n = s.shape[0]
    nio = _iota(s.shape, 0).astype(F32)
    work = s
    rank = jnp.full(s.shape, float(k), F32)
    vals = []
    for r in range(k):
        m = _colmax(work)
        idx = jnp.min(jnp.where(work == m, nio, float(n)), axis=0, keepdims=True)
        pick = nio == idx
        rank = jnp.where(pick, float(r), rank)
        work = jnp.where(pick, -jnp.inf, work)
        vals.append(m)
    return rank, vals


def _peer_keys_kernel(qh_ref, sk_ref, r2_ref, e2_ref, lx_ref, e1_ref):
    half = PEER_DKEY // 2
    for h in range(PEER_HEADS):
        q1 = qh_ref[:, h * PEER_DKEY:h * PEER_DKEY + half]
        q2 = qh_ref[:, h * PEER_DKEY + half:(h + 1) * PEER_DKEY]
        s1 = _dot_nt(sk_ref[0], q1, HI)
        s2 = _dot_nt(sk_ref[1], q2, HI)
        rank1, v1 = _topk_rows(s1, PEER_TOPK)
        rank2, v2 = _topk_rows(s2, PEER_TOPK)
        npad = -len(PEER_PAIRS) % 8
        cand = jnp.concatenate([v1[a] + v2[b] for a, b in PEER_PAIRS]
                               + [jnp.full((npad, s1.shape[1]), -jnp.inf, F32)], axis=0)
        crank, _ = _topk_rows(cand, PEER_TOPK)
        chosen = crank < PEER_TOPK
        z = _colsum(jnp.where(chosen, jnp.exp(cand - cand[0:1]), 0.0))
        chosen_f = jnp.where(chosen, 1.0, 0.0)
        lx = jnp.zeros(s1.shape, F32)
        for a in range(PEER_TOPK):
            rows = [i for i, (pa, _) in enumerate(PEER_PAIRS) if pa == a]
            cnt = chosen_f[rows[0]:rows[0] + 1]
            for i in rows[1:]:
                cnt = cnt + chosen_f[i:i + 1]
            lx = jnp.where(rank1 == float(a), cnt, lx)
        r2_ref[h] = rank2.astype(BF16)
        e2_ref[h] = (jnp.exp(s2 - v2[0]) / z).astype(BF16)
        lx_ref[h] = lx
        e1_ref[h] = jnp.where(rank1 < PEER_TOPK, jnp.exp(s1 - v1[0]), 0.0)


def _peer_keys(qh, subkeys):
    n = qh.shape[0]
    tk = min(n, 256)
    dq = PEER_HEADS * PEER_DKEY
    half = PEER_DKEY // 2
    out = pl.BlockSpec((PEER_HEADS, PEER_KEYS, tk), lambda i: (0, 0, i))
    return pl.pallas_call(
        _peer_keys_kernel,
        grid=(n // tk,),
        in_specs=[pl.BlockSpec((tk, dq), lambda i: (i, 0)),
                  pl.BlockSpec((2, PEER_KEYS, half), lambda i: (0, 0, 0))],
        out_specs=[out] * 4,
        out_shape=[jax.ShapeDtypeStruct((PEER_HEADS, PEER_KEYS, n), dt) for dt in (BF16, BF16, F32, F32)],
        compiler_params=_cparams("parallel"),
        name="peer_keys",
    )(qh, subkeys)


PEER_AC = 8


def _gelu_tanh(x):
    return 0.5 * x * (1.0 + jnp.tanh(math.sqrt(2.0 / math.pi) * (x + 0.044715 * (x * x * x))))


def _peer_kernel(xn_ref, u_ref, vt_ref, r2_ref, e2_ref, lx_ref, e1_ref, x1_ref, o_ref, acc, *, nsteps):
    j = pl.program_id(1)

    @pl.when(j == 0)
    def _():
        acc[...] = jnp.zeros_like(acc)

    ht = _dot_nt(u_ref[...], xn_ref[...])
    ws = []
    for a in range(PEER_AC):
        row = j * PEER_AC + a
        gate = None
        for h in range(PEER_HEADS):
            lim = lx_ref[h, pl.ds(row, 1), :].astype(BF16)
            term = jnp.where(r2_ref[h] < lim, e2_ref[h], jnp.zeros((), BF16)) * e1_ref[h, pl.ds(row, 1), :].astype(BF16)
            gate = term if gate is None else gate + term
        ws.append(gate * _gelu_tanh(ht[a * PEER_KEYS:(a + 1) * PEER_KEYS, :]).astype(BF16))
    acc[...] += _dot(vt_ref[...], jnp.concatenate(ws, axis=0))

    @pl.when(j == nsteps - 1)
    def _():
        o_ref[...] = x1_ref[...] + acc[...].T


def _peer(xn, u_bf, vt_bf, r2, e2, lx, e1, x1):
    n = xn.shape[0]
    tm = min(n, 512)
    ec = PEER_AC * PEER_KEYS
    nsteps = PEER_KEYS // PEER_AC
    tab = pl.BlockSpec((PEER_HEADS, PEER_KEYS, tm), lambda i, j: (0, 0, i))
    return pl.pallas_call(
        functools.partial(_peer_kernel, nsteps=nsteps),
        grid=(n // tm, nsteps),
        in_specs=[pl.BlockSpec((tm, D_MODEL), lambda i, j: (i, 0)),
                  pl.BlockSpec((ec, D_MODEL), lambda i, j: (j, 0)),
                  pl.BlockSpec((D_MODEL, ec), lambda i, j: (0, j)),
                  tab, tab, tab, tab,
                  pl.BlockSpec((tm, D_MODEL), lambda i, j: (i, 0))],
        out_specs=pl.BlockSpec((tm, D_MODEL), lambda i, j: (i, 0)),
        out_shape=jax.ShapeDtypeStruct((n, D_MODEL), F32),
        scratch_shapes=[pltpu.VMEM((D_MODEL, tm), F32)],
        compiler_params=_cparams("parallel", "arbitrary"),
        name="peer",
    )(xn, u_bf, vt_bf, r2, e2, lx, e1, x1)


def _pad_axis(a, axis, size):
    if a.shape[axis] == size:
        return a
    pad = [(0, 0)] * a.ndim
    pad[axis] = (0, size - a.shape[axis])
    return jnp.pad(a, pad)


def _mixers(x2d, bsz, t, pos, conv_hist, gdn0, ssm0, lp, tokens_as_time=False):
    conv_in, z, q, kv, small = _inproj(x2d, lp["norm_mix"], lp["w_cat"])
    r3 = lambda a: a.reshape(bsz, t, -1)
    c, new_conv = _conv(r3(conv_in), conv_hist, lp["conv_w"], lp["conv_b"])
    tpad = -(-t // CHUNK) * CHUNK
    t_valid = None if tpad == t else t
    cp, zp, sp = (_pad_axis(a, 1, tpad) for a in (c, r3(z), r3(small)))
    og, s_gdn = _gdn(cp, zp, sp, gdn0, lp["gdn_a_log"], lp["gdn_dt_bias"], lp["gdn_norm"], t_valid=t_valid)
    os_, s_ssm = _ssd(cp, zp, sp, ssm0, lp["ssm_a_log"], lp["ssm_dt_bias"], lp["ssm_d"], lp["ssm_norm"],
                      t_valid=t_valid)
    lead = (1, bsz * t) if tokens_as_time else (bsz, t)
    qn, rows, win, gates, ks, vs, kw, vw = _nsaproj(q.reshape(*lead, -1), kv.reshape(*lead, -1),
                                                    small.reshape(*lead, -1), pos, lp["nsa_qk_norm"])
    return dict(og=og[:, :t], os=os_[:, :t], new_conv=new_conv, s_gdn=s_gdn, s_ssm=s_ssm, q=r3(qn), rows=r3(rows),
                win=r3(win), gates=r3(gates), ks=ks, vs=vs, kw=kw, vw=vw)


def _channel_mix(og, on, os_, x, lp):
    x1, xn, qh = _outproj(og, on, os_, x, lp["w_out"], lp["norm_ffn"], lp["peer_wq"])
    n = x.shape[0]
    npad = -(-n // LANE) * LANE
    x1, xn, qh = (_pad_axis(a, 0, npad) for a in (x1, xn, qh))
    r2, e2, lx, e1 = _peer_keys(qh, lp["peer_subkeys"])
    return _peer(xn, lp["peer_u"], lp["peer_vt"], r2, e2, lx, e1, x1)[:n]


def kernel(x_prompt, x_sample, cache_nsa_kv, cache_nsa_win, state_conv, state_gdn, state_ssm, page_table, norm_mix, w_in, conv_w, conv_b, gdn_a_log, gdn_dt_bias, gdn_norm, ssm_a_log, ssm_dt_bias, ssm_d, ssm_norm, nsa_qk_norm, nsa_pe, nsa_phi, w_out, norm_ffn, peer_wq, peer_subkeys, peer_u, peer_v):
    bp, tp, _ = x_prompt.shape
    bs, ts, _ = x_sample.shape
    past = page_table.shape[1] * cache_nsa_kv.shape[2]
    n_pool = cache_nsa_kv.shape[1]
    yp = x_prompt.reshape(bp * tp, D_MODEL)
    ys = x_sample.reshape(bs * ts, D_MODEL)
    outs_p, outs_s = [], []
    for l in range(norm_mix.shape[0]):
        lp = dict(norm_mix=norm_mix[l], w_cat=_cat_w_in(w_in[l]), conv_w=conv_w[l], conv_b=conv_b[l],
                  gdn_a_log=gdn_a_log[l], gdn_dt_bias=gdn_dt_bias[l], gdn_norm=gdn_norm[l],
                  ssm_a_log=ssm_a_log[l], ssm_dt_bias=ssm_dt_bias[l], ssm_d=ssm_d[l], ssm_norm=ssm_norm[l],
                  nsa_qk_norm=nsa_qk_norm[l], nsa_pe=nsa_pe[l], nsa_phi=nsa_phi[l], w_out=w_out[l],
                  norm_ffn=norm_ffn[l], peer_wq=peer_wq[l], peer_subkeys=peer_subkeys[l],
                  peer_u=peer_u[l].astype(BF16), peer_vt=peer_v[l].T.astype(BF16))
        m = _mixers(yp, bp, tp, jnp.arange(tp, dtype=jnp.int32), jnp.zeros((bp, CONV_W - 1, CONV_CH), F32),
                    jnp.zeros((bp, GDN_HEADS, GDN_DK, GDN_DV), F32),
                    jnp.zeros((bp, SSM_HEADS, SSM_HEADDIM, SSM_DSTATE), F32), lp)
        pages = m["rows"].reshape(bp * tp // PAGE, PAGE, 4 * NSA_KV)
        ident = jnp.arange(bp * tp // PAGE, dtype=jnp.int32).reshape(bp, tp // PAGE)
        ck, cv = _compress(pages, ident, lp["nsa_pe"], lp["nsa_phi"])
        gates_t = jnp.transpose(m["gates"][..., 12:12 + 3 * NSA_HEADS], (0, 2, 1))
        o_nsa = _nsa_prompt(m["q"], ck, cv, m["ks"], m["vs"], m["kw"], m["vw"], gates_t)
        yp = _channel_mix(m["og"].reshape(bp * tp, -1), o_nsa.reshape(bp * tp, -1), m["os"].reshape(bp * tp, -1),
                          yp, lp)
        keep = min(NSA_WINDOW, tp)
        outs_p.append((m["rows"].reshape(bp, tp, 4, NSA_KV_HEADS, HEAD_DIM),
                       m["win"][:, tp - keep:].reshape(bp, keep, 2, NSA_KV_HEADS, HEAD_DIM),
                       m["new_conv"], m["s_gdn"], m["s_ssm"]))
        m = _mixers(ys, bs, ts, jnp.full((bs * ts,), past, jnp.int32), state_conv[l], state_gdn[l], state_ssm[l], lp,
                    tokens_as_time=True)
        cache_t = jnp.transpose(cache_nsa_kv[l], (0, 2, 3, 4, 1)).reshape(n_pool, 4 * NSA_KV, PAGE)
        ck, cv = _compress_t(cache_t, page_table, lp["nsa_pe"], lp["nsa_phi"])
        q3 = m["q"].reshape(bs, 1, NSA_Q)
        oc, idx = _nsa_sample_cmp(q3, ck, cv, past)
        wbl = cache_nsa_win.shape[2]
        win_t = jnp.transpose(cache_nsa_win[l], (0, 2, 3, 4, 1)).reshape(bs, 2 * NSA_KV, wbl)
        o_nsa, new_win_t = _nsa_sample_attn(
            cache_t, page_table, idx.reshape(bs, NSA_KV_HEADS * NSA_TOPK), q3, m["rows"].reshape(bs, 1, 4 * NSA_KV),
            m["win"].reshape(bs, 1, 2 * NSA_KV), win_t, m["gates"].reshape(bs, 1, SMALL_W), oc, past)
        new_win = jnp.transpose(new_win_t.reshape(bs, 2, NSA_KV_HEADS, HEAD_DIM, wbl), (0, 4, 1, 2, 3))
        ys = _channel_mix(m["og"].reshape(bs, -1), o_nsa.reshape(bs, -1), m["os"].reshape(bs, -1), ys, lp)
        outs_s.append((m["rows"].reshape(bs, ts, 4, NSA_KV_HEADS, HEAD_DIM),
                       new_win, m["new_conv"], m["s_gdn"], m["s_ssm"]))
    stack = lambda outs, i: jnp.stack([o[i] for o in outs])
    return (yp.reshape(bp, tp, D_MODEL), ys.reshape(bs, ts, D_MODEL),
            stack(outs_p, 0), stack(outs_p, 1), stack(outs_p, 2), stack(outs_p, 3), stack(outs_p, 4),
            stack(outs_s, 0), stack(outs_s, 1), stack(outs_s, 2), stack(outs_s, 3), stack(outs_s, 4))
```

```python
import functools
import math

import jax
import jax.numpy as jnp
from jax import lax
from jax.experimental import pallas as pl
from jax.experimental.pallas import tpu as pltpu

F32 = jnp.float32
BF16 = jnp.bfloat16
HI = lax.Precision.HIGHEST

D_MODEL = 1024
HEAD_DIM = 64
GDN_HEADS = 4
GDN_DK = 64
GDN_DV = 64
CHUNK = 64
NSA_HEADS = 8
NSA_KV_HEADS = 2
NSA_GROUP = 4
NSA_BLOCK = 64
NSA_TOPK = 16
NSA_WINDOW = 512
SSM_HEADS = 4
SSM_HEADDIM = 64
SSM_GROUPS = 2
SSM_DSTATE = 128
CONV_W = 4
PEER_KEYS = 128
PEER_HEADS = 8
PEER_TOPK = 16
PEER_DKEY = 256
ROPE_THETA = 10000.0
EPS = 1e-6

GDN_QK = GDN_HEADS * GDN_DK
GDN_V = GDN_HEADS * GDN_DV
SSM_DI = SSM_HEADS * SSM_HEADDIM
SSM_BC = SSM_GROUPS * SSM_DSTATE
GDN_CONV_CH = 2 * GDN_QK + GDN_V
CONV_CH = GDN_CONV_CH + SSM_DI + 2 * SSM_BC
NSA_Q = NSA_HEADS * HEAD_DIM
NSA_KV = NSA_KV_HEADS * HEAD_DIM
SMALL_W = 128
LANE = 128
VMEM_LIMIT = 48 * 1024 * 1024


def _cparams(*sem):
    return pltpu.CompilerParams(dimension_semantics=sem, vmem_limit_bytes=VMEM_LIMIT)


def _dot(a, b):
    return jnp.dot(a, b, preferred_element_type=F32)


def _split_bf16(a):
    hi = a.astype(BF16)
    return hi, (a - hi.astype(F32)).astype(BF16)


def _dot3_general(a, b, dims):
    a_hi, a_lo = _split_bf16(a)
    b_hi, b_lo = _split_bf16(b)
    dg = lambda x, y: lax.dot_general(x, y, (dims, ((), ())), preferred_element_type=F32)
    return dg(a_hi, b_hi) + dg(a_lo, b_hi) + dg(a_hi, b_lo)


def _dot_hi(a, b):
    return _dot3_general(a, b, ((1,), (0,)))


def _split3(a):
    hi = a.astype(BF16)
    r = a - hi.astype(F32)
    mid = r.astype(BF16)
    return hi, mid, (r - mid.astype(F32)).astype(BF16)


def _dot_sel(sel, x, dims=((1,), (0,)), sel_right=False):
    out = None
    for piece in _split3(x):
        ops = (piece, sel.astype(BF16)) if sel_right else (sel.astype(BF16), piece)
        t = lax.dot_general(*ops, (dims, ((), ())), preferred_element_type=F32)
        out = t if out is None else out + t
    return out


def _dot_nt(a, b, precision=None):
    if precision is not None:
        return _dot3_general(a, b, ((1,), (1,)))
    return lax.dot_general(a, b, (((1,), (1,)), ((), ())), preferred_element_type=F32)


def _dot_tn(a, b, precision=None):
    if precision is not None:
        return _dot3_general(a, b, ((0,), (0,)))
    return lax.dot_general(a, b, (((0,), (0,)), ((), ())), preferred_element_type=F32)


def _sigmoid(x):
    return 1.0 / (1.0 + jnp.exp(-x))


def _softplus(x):
    return jnp.maximum(x, 0.0) + jnp.log1p(jnp.exp(-jnp.abs(x)))


def _iota(shape, dim):
    return lax.broadcasted_iota(jnp.int32, shape, dim)


IN_GROUPS = (("conv", CONV_CH), ("z", GDN_V + SSM_DI), ("q", NSA_Q), ("kv", 6 * NSA_KV), ("small", SMALL_W))
IN_CAT = sum(w for _, w in IN_GROUPS)


def _inproj_kernel(x_ref, g_ref, w_ref, conv_ref, z_ref, q_ref, kv_ref, small_ref):
    x = x_ref[...]
    ms = jnp.mean(x * x, axis=-1, keepdims=True)
    xn = (x * lax.rsqrt(ms + EPS) * g_ref[...]).astype(BF16)
    off = 0
    for ref, (_, w) in zip((conv_ref, z_ref, q_ref, kv_ref, small_ref), IN_GROUPS):
        ref[...] = _dot(xn, w_ref[:, off:off + w])
        off += w


def _inproj(x2d, norm_g, w_cat):
    n = x2d.shape[0]
    tm = min(n, 512)
    return pl.pallas_call(
        _inproj_kernel,
        grid=(n // tm,),
        in_specs=[pl.BlockSpec((tm, D_MODEL), lambda i: (i, 0)),
                  pl.BlockSpec((1, D_MODEL), lambda i: (0, 0)),
                  pl.BlockSpec((D_MODEL, IN_CAT), lambda i: (0, 0))],
        out_specs=[pl.BlockSpec((tm, w), lambda i: (i, 0)) for _, w in IN_GROUPS],
        out_shape=[jax.ShapeDtypeStruct((n, w), F32) for _, w in IN_GROUPS],
        compiler_params=_cparams("parallel"),
        name="inproj",
    )(x2d, norm_g.reshape(1, D_MODEL), w_cat)


def _cat_w_in(w_in):
    o = 0
    parts = {}
    for name, w in (("conv", CONV_CH), ("gdn_z", GDN_V), ("gdn_b", GDN_HEADS), ("gdn_a", GDN_HEADS),
                    ("ssm_z", SSM_DI), ("ssm_dt", SSM_HEADS), ("nsa_q", NSA_Q), ("nsa_kv", 6 * NSA_KV),
                    ("nsa_g", 3 * NSA_HEADS)):
        parts[name] = w_in[:, o:o + w]
        o += w
    small = jnp.concatenate([parts["gdn_b"], parts["gdn_a"], parts["ssm_dt"], parts["nsa_g"]], axis=1)
    small = jnp.pad(small, ((0, 0), (0, SMALL_W - small.shape[1])))
    return jnp.concatenate([parts["conv"], parts["gdn_z"], parts["ssm_z"], parts["nsa_q"], parts["nsa_kv"], small],
                           axis=1).astype(BF16)


HALO = 8


def _conv_kernel(x_ref, hist_ref, w_ref, b_ref, c_ref, nc_ref, xe, *, tm, nt):
    t = pl.program_id(1)

    @pl.when(t == 0)
    def _():
        xe[HALO - 3:HALO, :] = hist_ref[0]

    xe[HALO:HALO + tm, :] = x_ref[0]
    y = b_ref[...] + xe[HALO - 3:HALO - 3 + tm, :] * w_ref[0:1, :]
    for j in range(1, CONV_W):
        y = y + xe[HALO - 3 + j:HALO - 3 + j + tm, :] * w_ref[j:j + 1, :]
    c_ref[0] = y * _sigmoid(y)
    last = xe[HALO + tm - 3:HALO + tm, :]

    @pl.when(t == nt - 1)
    def _():
        nc_ref[0] = last

    xe[HALO - 3:HALO, :] = last


def _conv(x, hist, w, b):
    bsz, t, ch = x.shape
    tm = min(t, 512)
    nt = t // tm
    return pl.pallas_call(
        functools.partial(_conv_kernel, tm=tm, nt=nt),
        grid=(bsz, nt),
        in_specs=[pl.BlockSpec((1, tm, ch), lambda i, j: (i, j, 0)),
                  pl.BlockSpec((1, CONV_W - 1, ch), lambda i, j: (i, 0, 0)),
                  pl.BlockSpec((CONV_W, ch), lambda i, j: (0, 0)),
                  pl.BlockSpec((1, ch), lambda i, j: (0, 0))],
        out_specs=[pl.BlockSpec((1, tm, ch), lambda i, j: (i, j, 0)),
                   pl.BlockSpec((1, CONV_W - 1, ch), lambda i, j: (i, 0, 0))],
        out_shape=[jax.ShapeDtypeStruct((bsz, t, ch), F32),
                   jax.ShapeDtypeStruct((bsz, CONV_W - 1, ch), F32)],
        scratch_shapes=[pltpu.VMEM((HALO + tm, ch), F32)],
        compiler_params=_cparams("parallel", "arbitrary"),
        name="conv",
    )(x, hist, w, b.reshape(1, ch))


def _tri(n, strict=False):
    r, c = _iota((n, n), 0), _iota((n, n), 1)
    return (r > c) if strict else (r >= c)


def _row_of(col):
    n = col.shape[0]
    eye = (_iota((n, n), 0) == _iota((n, n), 1)).astype(F32)
    return _dot_sel(jnp.ones((n, n), F32), eye * col)


def _gdn_kernel(q_ref, k_ref, v_ref, z_ref, small_ref, s0_ref, alog_ref, dtb_ref, ng_ref, o_ref, s_ref, st,
                *, nc, t_valid, bb):
    c = pl.program_id(1)

    @pl.when(c == 0)
    def _():
        st[...] = s0_ref[...]

    incl = _tri(CHUNK)
    strict = _tri(CHUNK, strict=True)
    if t_valid is not None:
        valid = (c * CHUNK + _iota((CHUNK, 1), 0)) < t_valid
    chains = [(b, h) for b in range(bb) for h in range(GDN_HEADS)]
    beta_all, gc_all = {}, {}
    for b in range(bb):
        sm = small_ref[b]
        bt = _sigmoid(sm[:, 0:GDN_HEADS])
        g_all = -jnp.exp(alog_ref[...]) * _softplus(sm[:, GDN_HEADS:2 * GDN_HEADS] + dtb_ref[...])
        if t_valid is not None:
            bt = jnp.where(valid, bt, 0.0)
            g_all = jnp.where(valid, g_all, 0.0)
        beta_all[b] = bt
        gc_all[b] = _dot_sel(incl.astype(F32), g_all)
    q, k, v, beta, gc = {}, {}, {}, {}, {}
    for c_ in chains:
        b, h = c_
        sl = slice(h * GDN_DK, (h + 1) * GDN_DK)
        qq = q_ref[b][:, sl]
        kk = k_ref[b][:, sl]
        vv = v_ref[b][:, sl]
        qq = qq * lax.rsqrt(jnp.sum(qq * qq, axis=-1, keepdims=True) + EPS) * (GDN_DK ** -0.5)
        kk = kk * lax.rsqrt(jnp.sum(kk * kk, axis=-1, keepdims=True) + EPS)
        if t_valid is not None:
            qq = jnp.where(valid, qq, 0.0)
            kk = jnp.where(valid, kk, 0.0)
            vv = jnp.where(valid, vv, 0.0)
        q[c_], k[c_], v[c_] = qq, kk, vv
        beta[c_] = beta_all[b][:, h:h + 1]
        gc[c_] = gc_all[b][:, h:h + 1]
    diff = {c_: gc[c_] - _row_of(gc[c_]) for c_ in chains}
    kkt = {c_: _dot_nt(k[c_], k[c_], HI) for c_ in chains}
    qkt = {c_: _dot_nt(q[c_], k[c_], HI) for c_ in chains}
    egc = {c_: jnp.exp(gc[c_]) for c_ in chains}
    p = {c_: -(beta[c_] * kkt[c_] * jnp.where(strict, jnp.exp(jnp.where(strict, diff[c_], 0.0)), 0.0))
         for c_ in chains}
    x = {c_: jnp.concatenate([v[c_] * beta[c_], k[c_] * (beta[c_] * egc[c_])], axis=-1) for c_ in chains}
    for it in range(6):
        px = {c_: _dot_hi(p[c_], x[c_]) for c_ in chains}
        if it < 5:
            p = {c_: _dot_hi(p[c_], p[c_]) for c_ in chains}
        x = {c_: x[c_] + px[c_] for c_ in chains}
    aqk = {c_: qkt[c_] * jnp.where(incl, jnp.exp(jnp.where(incl, diff[c_], 0.0)), 0.0) for c_ in chains}
    s = {c_: st[c_[0], c_[1]] for c_ in chains}
    wks = {c_: _dot_hi(x[c_][:, GDN_DV:], s[c_]) for c_ in chains}
    qs = {c_: _dot_hi(q[c_] * egc[c_], s[c_]) for c_ in chains}
    w = {c_: x[c_][:, :GDN_DV] - wks[c_] for c_ in chains}
    gl = {c_: gc[c_][CHUNK - 1:CHUNK, :] for c_ in chains}
    kdw = {c_: _dot_tn(k[c_] * jnp.exp(gl[c_] - gc[c_]), w[c_], HI) for c_ in chains}
    aw = {c_: _dot_hi(aqk[c_], w[c_]) for c_ in chains}
    for c_ in chains:
        st[c_[0], c_[1]] = jnp.exp(gl[c_]) * s[c_] + kdw[c_]
    for b in range(bb):
        outs = []
        for h in range(GDN_HEADS):
            o = qs[(b, h)] + aw[(b, h)]
            o = o * lax.rsqrt(jnp.mean(o * o, axis=-1, keepdims=True) + EPS) * ng_ref[...]
            zh = z_ref[b][:, h * GDN_DK:(h + 1) * GDN_DK]
            outs.append(o * (zh * _sigmoid(zh)))
        o_ref[b] = jnp.concatenate(outs, axis=-1)

    @pl.when(c == nc - 1)
    def _():
        s_ref[...] = st[...]


GDN_BB = 2


def _gdn(c, z, small, s0, a_log, dt_bias, norm_g, t_valid=None):
    bsz, t, _ = c.shape
    nc = t // CHUNK
    bb = GDN_BB
    return pl.pallas_call(
        functools.partial(_gdn_kernel, nc=nc, t_valid=t_valid, bb=bb),
        grid=(bsz // bb, nc),
        in_specs=[pl.BlockSpec((bb, CHUNK, GDN_QK), lambda i, j: (i, j, 0)),
                  pl.BlockSpec((bb, CHUNK, GDN_QK), lambda i, j: (i, j, 1)),
                  pl.BlockSpec((bb, CHUNK, GDN_V), lambda i, j: (i, j, 2)),
                  pl.BlockSpec((bb, CHUNK, GDN_V), lambda i, j: (i, j, 0)),
                  pl.BlockSpec((bb, CHUNK, SMALL_W), lambda i, j: (i, j, 0)),
                  pl.BlockSpec((bb, GDN_HEADS, GDN_DK, GDN_DV), lambda i, j: (i, 0, 0, 0)),
                  pl.BlockSpec((1, GDN_HEADS), lambda i, j: (0, 0)),
                  pl.BlockSpec((1, GDN_HEADS), lambda i, j: (0, 0)),
                  pl.BlockSpec((1, GDN_DV), lambda i, j: (0, 0))],
        out_specs=[pl.BlockSpec((bb, CHUNK, GDN_V), lambda i, j: (i, j, 0)),
                   pl.BlockSpec((bb, GDN_HEADS, GDN_DK, GDN_DV), lambda i, j: (i, 0, 0, 0))],
        out_shape=[jax.ShapeDtypeStruct((bsz, t, GDN_V), F32),
                   jax.ShapeDtypeStruct((bsz, GDN_HEADS, GDN_DK, GDN_DV), F32)],
        scratch_shapes=[pltpu.VMEM((bb, GDN_HEADS, GDN_DK, GDN_DV), F32)],
        compiler_params=_cparams("parallel", "arbitrary"),
        name="gdn",
    )(c, c, c, z, small, s0, a_log.reshape(1, -1), dt_bias.reshape(1, -1), norm_g.reshape(1, -1))


def _ssd_kernel(x_ref, b_ref, c_ref, z_ref, small_ref, h0_ref, alog_ref, dtb_ref, dsk_ref, ng_ref, o_ref, h_ref, st,
                *, nc, t_valid):
    ci = pl.program_id(1)

    @pl.when(ci == 0)
    def _():
        st[...] = h0_ref[0]

    sm = small_ref[0]
    dt_all = _softplus(sm[:, 2 * GDN_HEADS:2 * GDN_HEADS + SSM_HEADS] + dtb_ref[...])
    xs = x_ref[0]
    bm_all = b_ref[0]
    cm_all = c_ref[0]
    if t_valid is not None:
        valid = (ci * CHUNK + _iota((CHUNK, 1), 0)) < t_valid
        dt_all = jnp.where(valid, dt_all, 0.0)
        xs = jnp.where(valid, xs, 0.0)
        bm_all = jnp.where(valid, bm_all, 0.0)
        cm_all = jnp.where(valid, cm_all, 0.0)
    incl = _tri(CHUNK)
    cs_all = _dot_sel(incl.astype(F32), dt_all * (-jnp.exp(alog_ref[...])))
    rep = SSM_HEADS // SSM_GROUPS
    ys = []
    for h in range(SSM_HEADS):
        g = h // rep
        x = xs[:, h * SSM_HEADDIM:(h + 1) * SSM_HEADDIM]
        bm = bm_all[:, g * SSM_DSTATE:(g + 1) * SSM_DSTATE]
        cm = cm_all[:, g * SSM_DSTATE:(g + 1) * SSM_DSTATE]
        dt = dt_all[:, h:h + 1]
        cs = cs_all[:, h:h + 1]
        diff = cs - _row_of(cs)
        seg = jnp.where(incl, jnp.exp(jnp.where(incl, diff, 0.0)), 0.0)
        xdt = x * dt
        cb = _dot_nt(cm, bm, HI) * seg
        cl = cs[CHUNK - 1:CHUNK, :]
        hs = st[h]
        y = _dot_hi(cb, xdt) + _dot_nt(cm * jnp.exp(cs), hs, HI)
        st[h] = jnp.exp(cl) * hs + _dot_tn(xdt * jnp.exp(cl - cs), bm, HI)
        ys.append(y + dsk_ref[0:1, h:h + 1] * x)
    y = jnp.concatenate(ys, axis=-1)
    zz = z_ref[0]
    y = y * (zz * _sigmoid(zz))
    gs = SSM_DI // SSM_GROUPS
    outs = []
    for g in range(SSM_GROUPS):
        yg = y[:, g * gs:(g + 1) * gs]
        outs.append(yg * lax.rsqrt(jnp.mean(yg * yg, axis=-1, keepdims=True) + EPS) * ng_ref[0:1, g * gs:(g + 1) * gs])
    o_ref[0] = jnp.concatenate(outs, axis=-1)

    @pl.when(ci == nc - 1)
    def _():
        h_ref[0] = st[...]


def _ssd(c, z, small, h0, a_log, dt_bias, d_skip, norm_g, t_valid=None):
    bsz, t, _ = c.shape
    nc = t // CHUNK
    w = SSM_DI
    return pl.pallas_call(
        functools.partial(_ssd_kernel, nc=nc, t_valid=t_valid),
        grid=(bsz, nc),
        in_specs=[pl.BlockSpec((1, CHUNK, w), lambda i, j: (i, j, 3)),
                  pl.BlockSpec((1, CHUNK, w), lambda i, j: (i, j, 4)),
                  pl.BlockSpec((1, CHUNK, w), lambda i, j: (i, j, 5)),
                  pl.BlockSpec((1, CHUNK, w), lambda i, j: (i, j, 1)),
                  pl.BlockSpec((1, CHUNK, SMALL_W), lambda i, j: (i, j, 0)),
                  pl.BlockSpec((1, SSM_HEADS, SSM_HEADDIM, SSM_DSTATE), lambda i, j: (i, 0, 0, 0)),
                  pl.BlockSpec((1, SSM_HEADS), lambda i, j: (0, 0)),
                  pl.BlockSpec((1, SSM_HEADS), lambda i, j: (0, 0)),
                  pl.BlockSpec((1, SSM_HEADS), lambda i, j: (0, 0)),
                  pl.BlockSpec((1, SSM_DI), lambda i, j: (0, 0))],
        out_specs=[pl.BlockSpec((1, CHUNK, SSM_DI), lambda i, j: (i, j, 0)),
                   pl.BlockSpec((1, SSM_HEADS, SSM_HEADDIM, SSM_DSTATE), lambda i, j: (i, 0, 0, 0))],
        out_shape=[jax.ShapeDtypeStruct((bsz, t, SSM_DI), F32),
                   jax.ShapeDtypeStruct((bsz, SSM_HEADS, SSM_HEADDIM, SSM_DSTATE), F32)],
        scratch_shapes=[pltpu.VMEM((SSM_HEADS, SSM_HEADDIM, SSM_DSTATE), F32)],
        compiler_params=_cparams("parallel", "arbitrary"),
        name="ssd",
    )(c, c, c, z, small, h0, a_log.reshape(1, -1), dt_bias.reshape(1, -1), d_skip.reshape(1, -1),
      norm_g.reshape(1, -1))


PAGE = 128
NEG = -1e30


def _rope_tables(pos):
    half = HEAD_DIM // 2
    inv = ROPE_THETA ** (-jnp.arange(half, dtype=F32) * 2.0 / HEAD_DIM)
    ang = pos.astype(F32)[:, None] * inv[None, :]
    cos, sin = jnp.cos(ang), jnp.sin(ang)
    return jnp.tile(cos, (1, 4)), jnp.tile(jnp.concatenate([-sin, sin], axis=1), (1, 2))


def _nsaproj_kernel(q_ref, kv_ref, small_ref, cos_ref, sin_ref, qn_ref, bd_ref,
                    qo_ref, rows_ref, win_ref, gates_ref, ks_ref, vs_ref, kw_ref, vw_ref):
    cosf = cos_ref[...]
    sins = sin_ref[...]
    bd = bd_ref[...]
    first_half = (_iota((1, LANE), 1) % HEAD_DIM) < (HEAD_DIM // 2)

    def normrope(x, g):
        y = x * lax.rsqrt(_dot_hi(x * x, bd) + EPS) * g
        rot = jnp.where(first_half, pltpu.roll(y, LANE - HEAD_DIM // 2, 1), pltpu.roll(y, HEAD_DIM // 2, 1))
        return y * cosf + rot * sins

    qr = q_ref[0]
    qo_ref[0] = jnp.concatenate(
        [normrope(qr[:, i * LANE:(i + 1) * LANE], qn_ref[0:1, :]) * (HEAD_DIM ** -0.5) for i in range(NSA_Q // LANE)],
        axis=-1)
    kv = kv_ref[0]
    kc = normrope(kv[:, 0:LANE], qn_ref[1:2, :])
    vc = kv[:, LANE:2 * LANE]
    ks = normrope(kv[:, 2 * LANE:3 * LANE], qn_ref[2:3, :])
    vs = kv[:, 3 * LANE:4 * LANE]
    kw = normrope(kv[:, 4 * LANE:5 * LANE], qn_ref[3:4, :])
    vw = kv[:, 5 * LANE:6 * LANE]
    rows_ref[0] = jnp.concatenate([kc, vc, ks, vs], axis=-1)
    win_ref[0] = jnp.concatenate([kw, vw], axis=-1)
    gates_ref[0] = _sigmoid(small_ref[0])
    for h in range(NSA_KV_HEADS):
        sl = slice(h * HEAD_DIM, (h + 1) * HEAD_DIM)
        ks_ref[0, h] = ks[:, sl].astype(BF16)
        vs_ref[0, h] = vs[:, sl].astype(BF16)
        kw_ref[0, h] = kw[:, sl].astype(BF16)
        vw_ref[0, h] = vw[:, sl].astype(BF16)


def _nsaproj(q_raw, kv_raw, small, pos, qk_norm):
    bsz, t, _ = q_raw.shape
    tm = min(t, 512)
    cos, sin = _rope_tables(pos)
    qn = jnp.tile(qk_norm, (1, 2))
    bd = jnp.kron(jnp.eye(2, dtype=F32), jnp.full((HEAD_DIM, HEAD_DIM), 1.0 / HEAD_DIM, F32))
    tok = lambda w: pl.BlockSpec((1, tm, w), lambda i, j: (i, j, 0))
    headed = pl.BlockSpec((1, NSA_KV_HEADS, tm, HEAD_DIM), lambda i, j: (i, 0, j, 0))
    return pl.pallas_call(
        _nsaproj_kernel,
        grid=(bsz, t // tm),
        in_specs=[tok(NSA_Q), tok(6 * NSA_KV), tok(SMALL_W),
                  pl.BlockSpec((tm, LANE), lambda i, j: (j, 0)),
                  pl.BlockSpec((tm, LANE), lambda i, j: (j, 0)),
                  pl.BlockSpec((4, LANE), lambda i, j: (0, 0)),
                  pl.BlockSpec((LANE, LANE), lambda i, j: (0, 0))],
        out_specs=[tok(NSA_Q), tok(4 * NSA_KV), tok(2 * NSA_KV), tok(SMALL_W), headed, headed, headed, headed],
        out_shape=[jax.ShapeDtypeStruct((bsz, t, NSA_Q), F32),
                   jax.ShapeDtypeStruct((bsz, t, 4 * NSA_KV), F32),
                   jax.ShapeDtypeStruct((bsz, t, 2 * NSA_KV), F32),
                   jax.ShapeDtypeStruct((bsz, t, SMALL_W), F32)]
                  + [jax.ShapeDtypeStruct((bsz, NSA_KV_HEADS, t, HEAD_DIM), BF16)] * 4,
        compiler_params=_cparams("parallel", "parallel"),
        name="nsaproj",
    )(q_raw, kv_raw, small, cos, sin, qn, bd)


CMP_PAGES = 16


def _compress_kernel(pt_ref, *refs):
    del pt_ref
    page_refs = refs[:CMP_PAGES]
    pe_ref, phi_ref, ck_ref, cv_ref, bufk, bufv = refs[CMP_PAGES:]
    for i, r in enumerate(page_refs):
        bufk[i * PAGE:(i + 1) * PAGE, :] = r[0, :, 0:LANE]
        bufv[i * PAGE:(i + 1) * PAGE, :] = r[0, :, LANE:2 * LANE]
    nblk = CMP_PAGES * PAGE // NSA_BLOCK

    def body(l, carry):
        ak, av = carry
        kl = bufk[pl.ds(l, nblk, stride=NSA_BLOCK), :] + pe_ref[0, pl.ds(l, 1), :]
        vl = bufv[pl.ds(l, nblk, stride=NSA_BLOCK), :] + pe_ref[1, pl.ds(l, 1), :]
        return ak + _dot_hi(kl, phi_ref[0, l]), av + _dot_hi(vl, phi_ref[1, l])

    zero = jnp.zeros((nblk, LANE), F32)
    ak, av = lax.fori_loop(0, NSA_BLOCK, body, (zero, zero))
    ck_ref[0] = ak
    cv_ref[0] = av


def _compress(pages, page_table, pe, phi):
    bsz, npages = page_table.shape
    steps = npages // CMP_PAGES
    nblk = CMP_PAGES * PAGE // NSA_BLOCK
    pe2 = jnp.tile(pe, (1, 1, 2))
    eye2 = jnp.eye(2, dtype=F32)
    phi_bd = jnp.einsum('ab,klde->kladbe', eye2, phi).reshape(2, NSA_BLOCK, LANE, LANE)
    page_specs = [pl.BlockSpec((1, PAGE, 2 * LANE), lambda b, j, pt, i=i: (pt[b, j * CMP_PAGES + i], 0, 0))
                  for i in range(CMP_PAGES)]
    return pl.pallas_call(
        _compress_kernel,
        grid_spec=pltpu.PrefetchScalarGridSpec(
            num_scalar_prefetch=1,
            grid=(bsz, steps),
            in_specs=page_specs + [pl.BlockSpec((2, NSA_BLOCK, LANE), lambda b, j, pt: (0, 0, 0)),
                                   pl.BlockSpec((2, NSA_BLOCK, LANE, LANE), lambda b, j, pt: (0, 0, 0, 0))],
            out_specs=[pl.BlockSpec((1, nblk, LANE), lambda b, j, pt: (b, j, 0)),
                       pl.BlockSpec((1, nblk, LANE), lambda b, j, pt: (b, j, 0))],
            scratch_shapes=[pltpu.VMEM((CMP_PAGES * PAGE, LANE), F32)] * 2),
        out_shape=[jax.ShapeDtypeStruct((bsz, steps * nblk, LANE), F32)] * 2,
        compiler_params=_cparams("parallel", "parallel"),
        name="compress",
    )(page_table, *([pages] * CMP_PAGES), pe2, phi_bd)


QBLK = 128
KTILE = 512


def _colmax(x):
    return jnp.max(x, axis=0, keepdims=True)


def _colsum(x):
    return jnp.sum(x, axis=0, keepdims=True)


def _tile4(x):
    return jnp.concatenate([x] * NSA_GROUP, axis=1)


def _select_blocks(imp, q0, nb):
    nio = _iota(imp.shape, 0)
    nio_f = nio.astype(F32)
    cur = (q0 + _iota((1, imp.shape[1]), 1)) // NSA_BLOCK
    forced = (nio == 0) | (nio == cur) | (nio == cur - 1)
    causal = nio <= cur
    work = jnp.where(forced, jnp.inf, jnp.where(causal, imp, -jnp.inf))
    sel = jnp.zeros(imp.shape, F32)
    for _ in range(min(NSA_TOPK, nb)):
        m = _colmax(work)
        idx = jnp.min(jnp.where(work == m, nio_f, float(nb)), axis=0, keepdims=True)
        pick = nio_f == idx
        sel = jnp.where(pick, 1.0, sel)
        work = jnp.where(pick, -jnp.inf, work)
    return jnp.where(causal, sel, 0.0)


def _nsa_prompt_kernel(q_ref, ck_ref, cv_ref, ks_ref, vs_ref, kw_ref, vw_ref, gt_ref, o_ref, sel_sc, *, t_total):
    nb = t_total // NSA_BLOCK
    kt = min(KTILE, t_total)
    wk = min(NSA_WINDOW + QBLK, t_total)
    bpt = kt // NSA_BLOCK
    qi = pl.program_id(1)
    q0 = qi * QBLK
    qall = q_ref[0]
    gt = gt_ref[0]
    qpos = _tile4(q0 + _iota((1, QBLK), 1))
    eye = (_iota((HEAD_DIM, HEAD_DIM), 0) == _iota((HEAD_DIM, HEAD_DIM), 1)).astype(BF16)
    heads = range(NSA_KV_HEADS)
    qsb, oc = [], []
    for h in heads:
        qs = jnp.concatenate([qall[:, (h * NSA_GROUP + g) * HEAD_DIM:(h * NSA_GROUP + g + 1) * HEAD_DIM]
                              for g in range(NSA_GROUP)], axis=0)
        qsb.append(qs.astype(BF16))
        sl = slice(h * HEAD_DIM, (h + 1) * HEAD_DIM)
        s = _dot_nt(ck_ref[0][:, sl], qs, HI)
        mask = ((_iota((nb, 1), 0) + 1) * NSA_BLOCK - 1) <= qpos
        m = _colmax(jnp.where(mask, s, NEG))
        p = jnp.where(mask, jnp.exp(s - m), 0.0)
        p = p / jnp.maximum(_colsum(p), 1e-30)
        oc.append(_dot_tn(cv_ref[0][:, sl], p, HI))
        imp = p[:, 0:QBLK]
        for g in range(1, NSA_GROUP):
            imp = imp + p[:, g * QBLK:(g + 1) * QBLK]
        sel_sc[h] = (_select_blocks(imp, q0, nb) - 1.0) * (-NEG)

    def body(i, carry, diagonal):
        k0 = pl.multiple_of(i * kt, kt)
        sc = []
        for h in heads:
            brows = sel_sc[h, pl.ds(pl.multiple_of(i * bpt, bpt), bpt), :]
            bias = jnp.concatenate([jnp.broadcast_to(brows[j:j + 1, :], (NSA_BLOCK, QBLK)) for j in range(bpt)],
                                   axis=0)
            s_h = _dot_nt(ks_ref[0, h, pl.ds(k0, kt), :], qsb[h]) + _tile4(bias)
            if diagonal:
                s_h = jnp.where((k0 + _iota((kt, 1), 0)) <= qpos, s_h, NEG)
            sc.append(s_h)
        m_n = [jnp.maximum(carry[h][0], _colmax(sc[h])) for h in heads]
        pp = [jnp.exp(sc[h] - m_n[h]) for h in heads]
        pv = [_dot_tn(vs_ref[0, h, pl.ds(k0, kt), :], pp[h].astype(BF16)) for h in heads]
        out = []
        for h in heads:
            m_i, l_i, acc = carry[h]
            alpha = jnp.exp(m_i - m_n[h])
            out.append((m_n[h], l_i * alpha + _colsum(pp[h]), acc * alpha + pv[h]))
        return tuple(out)

    ntile = (q0 + QBLK + kt - 1) // kt
    init = (jnp.full((1, NSA_GROUP * QBLK), NEG, F32), jnp.zeros((1, NSA_GROUP * QBLK), F32),
            jnp.zeros((HEAD_DIM, NSA_GROUP * QBLK), F32))
    carry = lax.fori_loop(0, ntile - 1, functools.partial(body, diagonal=False), (init,) * NSA_KV_HEADS)
    carry = body(ntile - 1, carry, True)
    outs = []
    for h in heads:
        _, l_s, acc_s = carry[h]
        osel = acc_s / jnp.maximum(l_s, 1e-30)
        w0 = pl.multiple_of(jnp.maximum(q0 + QBLK - wk, 0), QBLK)
        sw = _dot_nt(kw_ref[0, h, pl.ds(w0, wk), :], qsb[h])
        delta = qpos - (w0 + _iota((wk, 1), 0))
        mw = (delta >= 0) & (delta < NSA_WINDOW)
        m = _colmax(jnp.where(mw, sw, NEG))
        pw = jnp.where(mw, jnp.exp(sw - m), 0.0)
        ow = _dot_tn(vw_ref[0, h, pl.ds(w0, wk), :], pw.astype(BF16)) / jnp.maximum(_colsum(pw), 1e-30)
        gate = [jnp.concatenate([gt[(h * NSA_GROUP + g) * 3 + j:(h * NSA_GROUP + g) * 3 + j + 1, :]
                                 for g in range(NSA_GROUP)], axis=1) for j in range(3)]
        ot = gate[0] * oc[h] + gate[1] * osel + gate[2] * ow
        for g in range(NSA_GROUP):
            outs.append(_dot_tn(ot[:, g * QBLK:(g + 1) * QBLK].astype(BF16), eye))
    o_ref[0] = jnp.concatenate(outs, axis=-1)


def _nsa_prompt(q, ck, cv, ks, vs, kw, vw, gates_t):
    bsz, t, _ = q.shape
    nb = t // NSA_BLOCK
    full_kv = pl.BlockSpec((1, NSA_KV_HEADS, t, HEAD_DIM), lambda i, j: (i, 0, 0, 0))
    return pl.pallas_call(
        functools.partial(_nsa_prompt_kernel, t_total=t),
        grid=(bsz, t // QBLK),
        in_specs=[pl.BlockSpec((1, QBLK, NSA_Q), lambda i, j: (i, j, 0)),
                  pl.BlockSpec((1, nb, LANE), lambda i, j: (i, 0, 0)),
                  pl.BlockSpec((1, nb, LANE), lambda i, j: (i, 0, 0)),
                  full_kv, full_kv, full_kv, full_kv,
                  pl.BlockSpec((1, 3 * NSA_HEADS, QBLK), lambda i, j: (i, 0, j))],
        out_specs=pl.BlockSpec((1, QBLK, NSA_Q), lambda i, j: (i, j, 0)),
        out_shape=jax.ShapeDtypeStruct((bsz, t, NSA_Q), F32),
        scratch_shapes=[pltpu.VMEM((NSA_KV_HEADS, nb, QBLK), F32)],
        compiler_params=_cparams("parallel", "arbitrary"),
        name="nsa_prompt",
    )(q, ck, cv, ks, vs, kw, vw, gates_t)


def _group_queries(q, h):
    return jnp.concatenate([q[:, (h * NSA_GROUP + g) * HEAD_DIM:(h * NSA_GROUP + g + 1) * HEAD_DIM]
                            for g in range(NSA_GROUP)] + [jnp.zeros((8 - NSA_GROUP, HEAD_DIM), F32)], axis=0)


def _dot_x3(a, w_hi, w_lo):
    a_hi, a_lo = _split_bf16(a)
    return _dot(a_hi, w_hi) + _dot(a_lo, w_hi) + _dot(a_hi, w_lo)


def _compress_t_kernel(pt_ref, *refs, npg):
    del pt_ref
    page_refs = refs[:npg]
    pe_ref, whi_ref, wlo_ref, ck_ref, cv_ref, buf = refs[npg:]
    rows = 2 * LANE
    for i, r in enumerate(page_refs):
        buf[i * rows:(i + 1) * rows, :] = r[0]

    def body(d, carry):
        accs = list(carry)
        for kv in range(2):
            a = jnp.concatenate([buf[pl.ds(kv * LANE + h * HEAD_DIM + d, npg, stride=rows), :]
                                 for h in range(NSA_KV_HEADS)], axis=0) + pe_ref[kv, pl.ds(d, 1), :]
            if kv == 0:
                accs[kv] = accs[kv] + _dot_x3(a, whi_ref[kv, d], wlo_ref[kv, d])
            else:
                accs[kv] = accs[kv] + _dot(a.astype(BF16), whi_ref[kv, d])
        return tuple(accs)

    zero = jnp.zeros((NSA_KV_HEADS * npg, LANE), F32)
    ak, av = lax.fori_loop(0, HEAD_DIM, body, (zero, zero))
    ck_ref[0] = ak
    cv_ref[0] = av


def _compress_t(cache_t, page_table, pe, phi):
    bsz, npg = page_table.shape
    eye2 = jnp.eye(2, dtype=F32)
    w = jnp.einsum('ab,klde->kdalbe', eye2, phi).reshape(2, HEAD_DIM, LANE, LANE)
    w_hi = w.astype(BF16)
    w_lo = (w - w_hi.astype(F32)).astype(BF16)
    pe_t = jnp.tile(jnp.transpose(pe, (0, 2, 1)), (1, 1, 2))
    page_specs = [pl.BlockSpec((1, 2 * LANE, PAGE), lambda b, pt, i=i: (pt[b, i], 0, 0)) for i in range(npg)]
    const = lambda shape: pl.BlockSpec(shape, lambda b, pt: (0,) * len(shape))
    out = pl.BlockSpec((1, NSA_KV_HEADS * npg, LANE), lambda b, pt: (b, 0, 0))
    return pl.pallas_call(
        functools.partial(_compress_t_kernel, npg=npg),
        grid_spec=pltpu.PrefetchScalarGridSpec(
            num_scalar_prefetch=1,
            grid=(bsz,),
            in_specs=page_specs + [const((2, HEAD_DIM, LANE)), const((2, HEAD_DIM, LANE, LANE)),
                                   const((2, HEAD_DIM, LANE, LANE))],
            out_specs=[out, out],
            scratch_shapes=[pltpu.VMEM((npg * 2 * LANE, PAGE), F32)]),
        out_shape=[jax.ShapeDtypeStruct((bsz, NSA_KV_HEADS * npg, LANE), F32)] * 2,
        compiler_params=_cparams("parallel"),
        name="compress_t",
    )(page_table, *([cache_t] * npg), pe_t, w_hi, w_lo)


def _nsa_sample_cmp_kernel(q_ref, ck_ref, cv_ref, oc_ref, idx_ref, *, past, npg):
    nbc = 2 * npg
    cur = past // NSA_BLOCK
    q = q_ref[0]
    lane = _iota((1, nbc), 1)
    blk = 2 * (lane % npg) + lane // npg
    mask = ((blk + 1) * NSA_BLOCK - 1) <= past
    ocs, idxs = [], []
    for h in range(NSA_KV_HEADS):
        qh = _group_queries(q, h)
        ckh = ck_ref[0][h * npg:(h + 1) * npg, :]
        cvh = cv_ref[0][h * npg:(h + 1) * npg, :]
        s = jnp.concatenate([_dot_nt(qh, ckh[:, i * HEAD_DIM:(i + 1) * HEAD_DIM], HI) for i in range(2)], axis=1)
        m = jnp.max(jnp.where(mask, s, NEG), axis=1, keepdims=True)
        p = jnp.where(mask, jnp.exp(s - m), 0.0)
        p = p / jnp.maximum(jnp.sum(p, axis=1, keepdims=True), 1e-30)
        oc = _dot_hi(p[:, 0:npg], cvh[:, 0:HEAD_DIM]) + _dot_hi(p[:, npg:nbc], cvh[:, HEAD_DIM:LANE])
        ocs.append(oc[0:NSA_GROUP])
        imp = p[0:1]
        for g in range(1, NSA_GROUP):
            imp = imp + p[g:g + 1]
        forced = (blk == 0) | (blk == cur) | (blk == cur - 1)
        score = jnp.where(forced, jnp.inf, jnp.where(blk <= cur, imp, -jnp.inf))
        avail = jnp.ones((1, nbc), jnp.bool_)
        slot = _iota((1, NSA_TOPK), 1)
        picked = jnp.full((1, NSA_TOPK), cur, jnp.int32)
        for r in range(NSA_TOPK - 1):
            mm = jnp.max(jnp.where(avail, score, -jnp.inf), axis=1, keepdims=True)
            eq = avail & (score == mm)
            idx = jnp.min(jnp.where(eq, blk, nbc), axis=1, keepdims=True)
            avail = avail & (blk != idx)
            picked = jnp.where(slot == r, idx, picked)
        idxs.append(picked)
    oc_ref[0] = jnp.concatenate(ocs, axis=0)
    idx_ref[0] = jnp.concatenate(idxs, axis=1)


def _nsa_sample_cmp(q3, ck, cv, past):
    bsz = q3.shape[0]
    npg = ck.shape[1] // NSA_KV_HEADS
    return pl.pallas_call(
        functools.partial(_nsa_sample_cmp_kernel, past=past, npg=npg),
        grid=(bsz,),
        in_specs=[pl.BlockSpec((1, 1, NSA_Q), lambda b: (b, 0, 0)),
                  pl.BlockSpec((1, NSA_KV_HEADS * npg, LANE), lambda b: (b, 0, 0)),
                  pl.BlockSpec((1, NSA_KV_HEADS * npg, LANE), lambda b: (b, 0, 0))],
        out_specs=[pl.BlockSpec((1, NSA_HEADS, HEAD_DIM), lambda b: (b, 0, 0)),
                   pl.BlockSpec((1, 1, NSA_KV_HEADS * NSA_TOPK), lambda b: (b, 0, 0))],
        out_shape=[jax.ShapeDtypeStruct((bsz, NSA_HEADS, HEAD_DIM), F32),
                   jax.ShapeDtypeStruct((bsz, 1, NSA_KV_HEADS * NSA_TOPK), jnp.int32)],
        compiler_params=_cparams("parallel"),
        name="nsa_sample_cmp",
    )(q3, ck, cv)


def _nsa_sample_attn_kernel(page_ref, half_ref, *refs, past):
    del page_ref
    nsel = NSA_KV_HEADS * NSA_TOPK
    blk_refs = refs[:nsel]
    q_ref, row_ref, nwin_ref, wc_ref, g_ref, oc_ref, o_ref, win_ref = refs[nsel:]
    b = pl.program_id(0)
    wbl = wc_ref.shape[2]
    q = q_ref[0]
    row = row_ref[0]
    nwin = nwin_ref[0]
    gates = g_ref[0]
    oc = oc_ref[0]
    half_of_lane = _iota((1, PAGE), 1) // NSA_BLOCK
    outs = []
    for h in range(NSA_KV_HEADS):
        sl = slice(h * HEAD_DIM, (h + 1) * HEAD_DIM)
        vsl = slice(LANE + h * HEAD_DIM, LANE + (h + 1) * HEAD_DIM)
        qh = _group_queries(q, h)
        ks_new = row[:, 2 * LANE + h * HEAD_DIM:2 * LANE + (h + 1) * HEAD_DIM]
        vs_new = row[:, 3 * LANE + h * HEAD_DIM:3 * LANE + (h + 1) * HEAD_DIM]
        s_new = jnp.sum(qh * ks_new, axis=1, keepdims=True)
        ss, ms = [], []
        for k in range(NSA_TOPK - 1):
            ss.append(_dot_hi(qh, blk_refs[h * NSA_TOPK + k][0, sl, :]))
            ms.append(half_of_lane == half_ref[b, h * NSA_TOPK + k])
        m = s_new
        for s, mk in zip(ss, ms):
            m = jnp.maximum(m, jnp.max(jnp.where(mk, s, NEG), axis=1, keepdims=True))
        p_new = jnp.exp(s_new - m)
        den = p_new
        acc = p_new * vs_new
        for k, (s, mk) in enumerate(zip(ss, ms)):
            p = jnp.where(mk, jnp.exp(s - m), 0.0)
            den = den + jnp.sum(p, axis=1, keepdims=True)
            acc = acc + _dot_nt(p, blk_refs[h * NSA_TOPK + k][0, vsl, :], HI)
        osel = acc / jnp.maximum(den, 1e-30)
        sw = _dot_hi(qh, wc_ref[0, sl, :])
        delta = wbl - _iota((1, wbl), 1)
        mw = (delta >= 0) & (delta < NSA_WINDOW) & ((past - delta) >= 0)
        kw_new = nwin[:, sl]
        vw_new = nwin[:, vsl]
        sw_new = jnp.sum(qh * kw_new, axis=1, keepdims=True)
        m = jnp.maximum(jnp.max(jnp.where(mw, sw, NEG), axis=1, keepdims=True), sw_new)
        pw = jnp.where(mw, jnp.exp(sw - m), 0.0)
        pw_new = jnp.exp(sw_new - m)
        ow = (_dot_nt(pw, wc_ref[0, vsl, :], HI) + pw_new * vw_new) / jnp.maximum(
            jnp.sum(pw, axis=1, keepdims=True) + pw_new, 1e-30)
        for g in range(NSA_GROUP):
            c0 = 12 + (h * NSA_GROUP + g) * 3
            outs.append(gates[:, c0:c0 + 1] * oc[h * NSA_GROUP + g:h * NSA_GROUP + g + 1, :]
                        + gates[:, c0 + 1:c0 + 2] * osel[g:g + 1, :] + gates[:, c0 + 2:c0 + 3] * ow[g:g + 1, :])
    o_ref[0] = jnp.concatenate(outs, axis=1)
    new_col = _dot_sel(jnp.ones((8, wbl), F32), jnp.concatenate([nwin, jnp.zeros((7, 2 * NSA_KV), F32)], axis=0),
                       ((0,), (0,)), sel_right=True)
    win_ref[0] = jnp.where(_iota((1, wbl), 1) == wbl - 1, new_col, pltpu.roll(wc_ref[0], wbl - 1, 1))


def _nsa_sample_attn(cache_t, page_table, idx, q3, row3, nwin3, win_t, win_off, gates3, oc, past):
    bsz = q3.shape[0]
    wbl = win_t.shape[2]
    ncomplete = past // NSA_BLOCK

    n = jnp.clip(idx, 0, ncomplete - 1)
    page_of = jnp.take_along_axis(page_table, n // 2, axis=1)
    half_of = n % 2

    def blk_map(b, pg, hf, j):
        return (pg[b, j], 1, 0)

    blk_specs = [pl.BlockSpec((1, 2 * LANE, PAGE), functools.partial(blk_map, j=j))
                 for j in range(NSA_KV_HEADS * NSA_TOPK)]
    one = lambda w: pl.BlockSpec((1, 1, w), lambda b, pt, ix: (b, 0, 0))
    win_in = pl.BlockSpec((1, 2 * NSA_KV, wbl), lambda b, pt, ix: (b + win_off, 0, 0))
    win_out = pl.BlockSpec((1, 2 * NSA_KV, wbl), lambda b, pt, ix: (b, 0, 0))
    return pl.pallas_call(
        functools.partial(_nsa_sample_attn_kernel, past=past),
        grid_spec=pltpu.PrefetchScalarGridSpec(
            num_scalar_prefetch=2,
            grid=(bsz,),
            in_specs=blk_specs + [one(NSA_Q), one(4 * NSA_KV), one(2 * NSA_KV), win_in, one(SMALL_W),
                                  pl.BlockSpec((1, NSA_HEADS, HEAD_DIM), lambda b, pt, ix: (b, 0, 0))],
            out_specs=[one(NSA_Q), win_out]),
        out_shape=[jax.ShapeDtypeStruct((bsz, 1, NSA_Q), F32),
                   jax.ShapeDtypeStruct((bsz, 2 * NSA_KV, wbl), F32)],
        compiler_params=_cparams("parallel"),
        name="nsa_sample_attn",
    )(page_of, half_of, *([cache_t] * (NSA_KV_HEADS * NSA_TOPK)), q3, row3, nwin3, win_t, gates3, oc)


def _outproj_kernel(og_ref, on_ref, os_ref, x_ref, wo_ref, g_ref, wq_ref, x1_ref, xn_ref, qh_ref):
    x1 = (x_ref[...] + _dot(og_ref[...].astype(BF16), wo_ref[0:GDN_V, :])
          + _dot(on_ref[...].astype(BF16), wo_ref[GDN_V:GDN_V + NSA_Q, :])
          + _dot(os_ref[...].astype(BF16), wo_ref[GDN_V + NSA_Q:, :]))
    x1_ref[...] = x1
    xn = (x1 * lax.rsqrt(jnp.mean(x1 * x1, axis=-1, keepdims=True) + EPS) * g_ref[...]).astype(BF16)
    xn_ref[...] = xn
    qh_ref[...] = _dot(xn, wq_ref[...])


def _outproj(og, on, os_, x, w_out, norm_g, wq):
    n = x.shape[0]
    tm = min(n, 512)
    dq = PEER_HEADS * PEER_DKEY
    tok = lambda w: pl.BlockSpec((tm, w), lambda i: (i, 0))
    const = lambda a, b: pl.BlockSpec((a, b), lambda i: (0, 0))
    return pl.pallas_call(
        _outproj_kernel,
        grid=(n // tm,),
        in_specs=[tok(GDN_V), tok(NSA_Q), tok(SSM_DI), tok(D_MODEL), const(D_MODEL, D_MODEL), const(1, D_MODEL),
                  const(D_MODEL, dq)],
        out_specs=[tok(D_MODEL), tok(D_MODEL), tok(dq)],
        out_shape=[jax.ShapeDtypeStruct((n, D_MODEL), F32), jax.ShapeDtypeStruct((n, D_MODEL), BF16),
                   jax.ShapeDtypeStruct((n, dq), F32)],
        compiler_params=_cparams("parallel"),
        name="outproj",
    )(og, on, os_, x, w_out.astype(BF16), norm_g.reshape(1, D_MODEL), wq.astype(BF16))


PEER_PAIRS = tuple((a, b) for a in range(PEER_TOPK) for b in range(PEER_TOPK) if (a + 1) * (b + 1) <= PEER_TOPK)


def _topk_rows(s, k):
    n = s.shape[0]
    nio = _iota(s.shape, 0).astype(F32)
    work = s
    rank = jnp.full(s.shape, float(k), F32)
    vals = []
    for r in range(k):
        m = _colmax(work)
        idx = jnp.min(jnp.where(work == m, nio, float(n)), axis=0, keepdims=True)
        pick = nio == idx
        rank = jnp.where(pick, float(r), rank)
        work = jnp.where(pick, -jnp.inf, work)
        vals.append(m)
    return rank, vals


def _dup_bf16_bits(x):
    bits = lax.shift_right_logical(pltpu.bitcast(x.astype(BF16).astype(F32), jnp.uint32), jnp.uint32(16))
    return pltpu.bitcast(bits | lax.shift_left(bits, jnp.uint32(16)), jnp.int32)


def _peer_keys_kernel(qh_ref, sk_ref, r2_ref, e2_ref, lx_ref, e1_ref):
    half = PEER_DKEY // 2
    for h in range(PEER_HEADS):
        q1 = qh_ref[:, h * PEER_DKEY:h * PEER_DKEY + half]
        q2 = qh_ref[:, h * PEER_DKEY + half:(h + 1) * PEER_DKEY]
        s1 = _dot_nt(sk_ref[0], q1, HI)
        s2 = _dot_nt(sk_ref[1], q2, HI)
        rank1, v1 = _topk_rows(s1, PEER_TOPK)
        rank2, v2 = _topk_rows(s2, PEER_TOPK)
        npad = -len(PEER_PAIRS) % 8
        cand = jnp.concatenate([v1[a] + v2[b] for a, b in PEER_PAIRS]
                               + [jnp.full((npad, s1.shape[1]), -jnp.inf, F32)], axis=0)
        crank, _ = _topk_rows(cand, PEER_TOPK)
        chosen = crank < PEER_TOPK
        z = _colsum(jnp.where(chosen, jnp.exp(cand - cand[0:1]), 0.0))
        chosen_f = jnp.where(chosen, 1.0, 0.0)
        lx = jnp.zeros(s1.shape, F32)
        for a in range(PEER_TOPK):
            rows = [i for i, (pa, _) in enumerate(PEER_PAIRS) if pa == a]
            cnt = chosen_f[rows[0]:rows[0] + 1]
            for i in rows[1:]:
                cnt = cnt + chosen_f[i:i + 1]
            lx = jnp.where(rank1 == float(a), cnt, lx)
        r2_ref[h] = rank2.astype(BF16)
        e2_ref[h] = (jnp.exp(s2 - v2[0]) / z).astype(BF16)
        lx_ref[h] = _dup_bf16_bits(lx)
        e1_ref[h] = _dup_bf16_bits(jnp.where(rank1 < PEER_TOPK, jnp.exp(s1 - v1[0]), 0.0))


def _peer_keys(qh, subkeys):
    n = qh.shape[0]
    tk = min(n, 256)
    dq = PEER_HEADS * PEER_DKEY
    half = PEER_DKEY // 2
    out = pl.BlockSpec((PEER_HEADS, PEER_KEYS, tk), lambda i: (0, 0, i))
    return pl.pallas_call(
        _peer_keys_kernel,
        grid=(n // tk,),
        in_specs=[pl.BlockSpec((tk, dq), lambda i: (i, 0)),
                  pl.BlockSpec((2, PEER_KEYS, half), lambda i: (0, 0, 0))],
        out_specs=[out] * 4,
        out_shape=[jax.ShapeDtypeStruct((PEER_HEADS, PEER_KEYS, n), dt) for dt in (BF16, BF16, jnp.int32, jnp.int32)],
        compiler_params=_cparams("parallel"),
        name="peer_keys",
    )(qh, subkeys)


PEER_AC = 8


def _gelu_tanh(x):
    c = math.sqrt(2.0 / math.pi)
    return x * (0.5 + 0.5 * jnp.tanh(x * (c + (c * 0.044715) * (x * x))))


def _row_tile_bf16(ref, h, row):
    r = ref[h, pl.ds(row, 1), :]
    t = pltpu.bitcast(jnp.broadcast_to(r, (8, r.shape[1])), BF16)
    return jnp.concatenate([t] * (PEER_KEYS // t.shape[0]), axis=0)


PEER_SUB = 2


def _peer_kernel(xn_ref, u_ref, vt_ref, r2_ref, e2_ref, lx_ref, e1_ref, x1_ref, o_ref, acc, *, nsteps):
    j = pl.program_id(1)

    @pl.when(j == 0)
    def _():
        acc[...] = jnp.zeros_like(acc)

    xn = xn_ref[...]
    nsb = PEER_AC // PEER_SUB
    width = PEER_SUB * PEER_KEYS

    def hidden(sb):
        return _dot_nt(u_ref[sb * width:(sb + 1) * width, :], xn)

    def gates(sb):
        out = []
        for a in range(PEER_SUB):
            row = j * PEER_AC + sb * PEER_SUB + a
            gate = None
            for h in range(PEER_HEADS):
                term = jnp.where(r2_ref[h] < _row_tile_bf16(lx_ref, h, row), e2_ref[h],
                                 jnp.zeros((), BF16)) * _row_tile_bf16(e1_ref, h, row)
                gate = term if gate is None else gate + term
            out.append(gate)
        return out

    part = None
    ht_next = hidden(0)
    for sb in range(nsb):
        ht = ht_next
        gs = gates(sb)
        if sb + 1 < nsb:
            ht_next = hidden(sb + 1)
        ws = [gs[a] * _gelu_tanh(ht[a * PEER_KEYS:(a + 1) * PEER_KEYS, :]).astype(BF16) for a in range(PEER_SUB)]
        t = _dot(vt_ref[:, sb * width:(sb + 1) * width], jnp.concatenate(ws, axis=0))
        part = t if part is None else t + part
    acc[...] += part

    @pl.when(j == nsteps - 1)
    def _():
        o_ref[...] = x1_ref[...] + acc[...].T


def _peer(xn, u_bf, vt_bf, r2, e2, lx, e1, x1):
    n = xn.shape[0]
    tm = min(n, 512)
    ec = PEER_AC * PEER_KEYS
    nsteps = PEER_KEYS // PEER_AC
    tab = pl.BlockSpec((PEER_HEADS, PEER_KEYS, tm), lambda i, j: (0, 0, i))
    return pl.pallas_call(
        functools.partial(_peer_kernel, nsteps=nsteps),
        grid=(n // tm, nsteps),
        in_specs=[pl.BlockSpec((tm, D_MODEL), lambda i, j: (i, 0)),
                  pl.BlockSpec((ec, D_MODEL), lambda i, j: (j, 0)),
                  pl.BlockSpec((D_MODEL, ec), lambda i, j: (0, j)),
                  tab, tab, tab, tab,
                  pl.BlockSpec((tm, D_MODEL), lambda i, j: (i, 0))],
        out_specs=pl.BlockSpec((tm, D_MODEL), lambda i, j: (i, 0)),
        out_shape=jax.ShapeDtypeStruct((n, D_MODEL), F32),
        scratch_shapes=[pltpu.VMEM((D_MODEL, tm), F32)],
        compiler_params=_cparams("parallel", "arbitrary"),
        name="peer",
    )(xn, u_bf, vt_bf, r2, e2, lx, e1, x1)


def _pad_axis(a, axis, size):
    if a.shape[axis] == size:
        return a
    pad = [(0, 0)] * a.ndim
    pad[axis] = (0, size - a.shape[axis])
    return jnp.pad(a, pad)


def _mixers(x2d, bsz, t, pos, conv_hist, gdn0, ssm0, lp, tokens_as_time=False):
    conv_in, z, q, kv, small = _inproj(x2d, lp["norm_mix"], lp["w_cat"])
    r3 = lambda a: a.reshape(bsz, t, -1)
    c, new_conv = _conv(r3(conv_in), conv_hist, lp["conv_w"], lp["conv_b"])
    tpad = -(-t // CHUNK) * CHUNK
    t_valid = None if tpad == t else t
    cp, zp, sp = (_pad_axis(a, 1, tpad) for a in (c, r3(z), r3(small)))
    og, s_gdn = _gdn(cp, zp, sp, gdn0, lp["gdn_a_log"], lp["gdn_dt_bias"], lp["gdn_norm"], t_valid=t_valid)
    os_, s_ssm = _ssd(cp, zp, sp, ssm0, lp["ssm_a_log"], lp["ssm_dt_bias"], lp["ssm_d"], lp["ssm_norm"],
                      t_valid=t_valid)
    lead = (1, bsz * t) if tokens_as_time else (bsz, t)
    qn, rows, win, gates, ks, vs, kw, vw = _nsaproj(q.reshape(*lead, -1), kv.reshape(*lead, -1),
                                                    small.reshape(*lead, -1), pos, lp["nsa_qk_norm"])
    return dict(og=og[:, :t], os=os_[:, :t], new_conv=new_conv, s_gdn=s_gdn, s_ssm=s_ssm, q=r3(qn), rows=r3(rows),
                win=r3(win), gates=r3(gates), ks=ks, vs=vs, kw=kw, vw=vw)


def _channel_mix(og, on, os_, x, lp):
    x1, xn, qh = _outproj(og, on, os_, x, lp["w_out"], lp["norm_ffn"], lp["peer_wq"])
    n = x.shape[0]
    npad = -(-n // LANE) * LANE
    x1, xn, qh = (_pad_axis(a, 0, npad) for a in (x1, xn, qh))
    r2, e2, lx, e1 = _peer_keys(qh, lp["peer_subkeys"])
    return _peer(xn, lp["peer_u"], lp["peer_vt"], r2, e2, lx, e1, x1)[:n]


def kernel(x_prompt, x_sample, cache_nsa_kv, cache_nsa_win, state_conv, state_gdn, state_ssm, page_table, norm_mix, w_in, conv_w, conv_b, gdn_a_log, gdn_dt_bias, gdn_norm, ssm_a_log, ssm_dt_bias, ssm_d, ssm_norm, nsa_qk_norm, nsa_pe, nsa_phi, w_out, norm_ffn, peer_wq, peer_subkeys, peer_u, peer_v):
    bp, tp, _ = x_prompt.shape
    bs, ts, _ = x_sample.shape
    past = page_table.shape[1] * cache_nsa_kv.shape[2]
    n_pool = cache_nsa_kv.shape[1]
    yp = x_prompt.reshape(bp * tp, D_MODEL)
    ys = x_sample.reshape(bs * ts, D_MODEL)
    depth = norm_mix.shape[0]
    wbl = cache_nsa_win.shape[2]
    cache_t = jnp.transpose(cache_nsa_kv, (0, 1, 3, 4, 5, 2)).reshape(depth * n_pool, 4 * NSA_KV, PAGE)
    win_t = jnp.transpose(cache_nsa_win, (0, 1, 3, 4, 5, 2)).reshape(depth * bs, 2 * NSA_KV, wbl)
    outs_p, outs_s = [], []
    for l in range(norm_mix.shape[0]):
        lp = dict(norm_mix=norm_mix[l], w_cat=_cat_w_in(w_in[l]), conv_w=conv_w[l], conv_b=conv_b[l],
                  gdn_a_log=gdn_a_log[l], gdn_dt_bias=gdn_dt_bias[l], gdn_norm=gdn_norm[l],
                  ssm_a_log=ssm_a_log[l], ssm_dt_bias=ssm_dt_bias[l], ssm_d=ssm_d[l], ssm_norm=ssm_norm[l],
                  nsa_qk_norm=nsa_qk_norm[l], nsa_pe=nsa_pe[l], nsa_phi=nsa_phi[l], w_out=w_out[l],
                  norm_ffn=norm_ffn[l], peer_wq=peer_wq[l], peer_subkeys=peer_subkeys[l],
                  peer_u=peer_u[l].astype(BF16), peer_vt=peer_v[l].T.astype(BF16))
        m = _mixers(yp, bp, tp, jnp.arange(tp, dtype=jnp.int32), jnp.zeros((bp, CONV_W - 1, CONV_CH), F32),
                    jnp.zeros((bp, GDN_HEADS, GDN_DK, GDN_DV), F32),
                    jnp.zeros((bp, SSM_HEADS, SSM_HEADDIM, SSM_DSTATE), F32), lp)
        pages = m["rows"].reshape(bp * tp // PAGE, PAGE, 4 * NSA_KV)
        ident = jnp.arange(bp * tp // PAGE, dtype=jnp.int32).reshape(bp, tp // PAGE)
        ck, cv = _compress(pages, ident, lp["nsa_pe"], lp["nsa_phi"])
        gates_t = jnp.transpose(m["gates"][..., 12:12 + 3 * NSA_HEADS], (0, 2, 1))
        o_nsa = _nsa_prompt(m["q"], ck, cv, m["ks"], m["vs"], m["kw"], m["vw"], gates_t)
        yp = _channel_mix(m["og"].reshape(bp * tp, -1), o_nsa.reshape(bp * tp, -1), m["os"].reshape(bp * tp, -1),
                          yp, lp)
        keep = min(NSA_WINDOW, tp)
        outs_p.append((m["rows"].reshape(bp, tp, 4, NSA_KV_HEADS, HEAD_DIM),
                       m["win"][:, tp - keep:].reshape(bp, keep, 2, NSA_KV_HEADS, HEAD_DIM),
                       m["new_conv"], m["s_gdn"], m["s_ssm"]))
        m = _mixers(ys, bs, ts, jnp.full((bs * ts,), past, jnp.int32), state_conv[l], state_gdn[l], state_ssm[l], lp,
                    tokens_as_time=True)
        pages_l = page_table + l * n_pool
        ck, cv = _compress_t(cache_t, pages_l, lp["nsa_pe"], lp["nsa_phi"])
        q3 = m["q"].reshape(bs, 1, NSA_Q)
        oc, idx = _nsa_sample_cmp(q3, ck, cv, past)
        o_nsa, new_win_t = _nsa_sample_attn(
            cache_t, pages_l, idx.reshape(bs, NSA_KV_HEADS * NSA_TOPK), q3, m["rows"].reshape(bs, 1, 4 * NSA_KV),
            m["win"].reshape(bs, 1, 2 * NSA_KV), win_t, l * bs, m["gates"].reshape(bs, 1, SMALL_W), oc, past)
        new_win = jnp.transpose(new_win_t.reshape(bs, 2, NSA_KV_HEADS, HEAD_DIM, wbl), (0, 4, 1, 2, 3))
        ys = _channel_mix(m["og"].reshape(bs, -1), o_nsa.reshape(bs, -1), m["os"].reshape(bs, -1), ys, lp)
        outs_s.append((m["rows"].reshape(bs, ts, 4, NSA_KV_HEADS, HEAD_DIM),
                       new_win, m["new_conv"], m["s_gdn"], m["s_ssm"]))
    stack = lambda outs, i: jnp.stack([o[i] for o in outs])
    return (yp.reshape(bp, tp, D_MODEL), ys.reshape(bs, ts, D_MODEL),
            stack(outs_p, 0), stack(outs_p, 1), stack(outs_p, 2), stack(outs_p, 3), stack(outs_p, 4),
            stack(outs_s, 0), stack(outs_s, 1), stack(outs_s, 2), stack(outs_s, 3), stack(outs_s, 4))
```

```python
import functools
import math

import jax
import jax.numpy as jnp
from jax import lax
from jax.experimental import pallas as pl
from jax.experimental.pallas import tpu as pltpu

F32 = jnp.float32
BF16 = jnp.bfloat16
HI = lax.Precision.HIGHEST

D_MODEL = 1024
HEAD_DIM = 64
GDN_HEADS = 4
GDN_DK = 64
GDN_DV = 64
CHUNK = 64
NSA_HEADS = 8
NSA_KV_HEADS = 2
NSA_GROUP = 4
NSA_BLOCK = 64
NSA_TOPK = 16
NSA_WINDOW = 512
SSM_HEADS = 4
SSM_HEADDIM = 64
SSM_GROUPS = 2
SSM_DSTATE = 128
CONV_W = 4
PEER_KEYS = 128
PEER_HEADS = 8
PEER_TOPK = 16
PEER_DKEY = 256
ROPE_THETA = 10000.0
EPS = 1e-6

GDN_QK = GDN_HEADS * GDN_DK
GDN_V = GDN_HEADS * GDN_DV
SSM_DI = SSM_HEADS * SSM_HEADDIM
SSM_BC = SSM_GROUPS * SSM_DSTATE
GDN_CONV_CH = 2 * GDN_QK + GDN_V
CONV_CH = GDN_CONV_CH + SSM_DI + 2 * SSM_BC
NSA_Q = NSA_HEADS * HEAD_DIM
NSA_KV = NSA_KV_HEADS * HEAD_DIM
SMALL_W = 128
LANE = 128
VMEM_LIMIT = 48 * 1024 * 1024


def _cparams(*sem):
    return pltpu.CompilerParams(dimension_semantics=sem, vmem_limit_bytes=VMEM_LIMIT)


def _dot(a, b):
    return jnp.dot(a, b, preferred_element_type=F32)


def _split_bf16(a):
    hi = a.astype(BF16)
    return hi, (a - hi.astype(F32)).astype(BF16)


def _dot3_general(a, b, dims):
    a_hi, a_lo = _split_bf16(a)
    b_hi, b_lo = _split_bf16(b)
    dg = lambda x, y: lax.dot_general(x, y, (dims, ((), ())), preferred_element_type=F32)
    return dg(a_hi, b_hi) + dg(a_lo, b_hi) + dg(a_hi, b_lo)


def _dot_hi(a, b):
    return _dot3_general(a, b, ((1,), (0,)))


def _split3(a):
    hi = a.astype(BF16)
    r = a - hi.astype(F32)
    mid = r.astype(BF16)
    return hi, mid, (r - mid.astype(F32)).astype(BF16)


def _dot_sel(sel, x, dims=((1,), (0,)), sel_right=False):
    out = None
    for piece in _split3(x):
        ops = (piece, sel.astype(BF16)) if sel_right else (sel.astype(BF16), piece)
        t = lax.dot_general(*ops, (dims, ((), ())), preferred_element_type=F32)
        out = t if out is None else out + t
    return out


def _dot_nt(a, b, precision=None):
    if precision is not None:
        return _dot3_general(a, b, ((1,), (1,)))
    return lax.dot_general(a, b, (((1,), (1,)), ((), ())), preferred_element_type=F32)


def _dot_tn(a, b, precision=None):
    if precision is not None:
        return _dot3_general(a, b, ((0,), (0,)))
    return lax.dot_general(a, b, (((0,), (0,)), ((), ())), preferred_element_type=F32)


def _sigmoid(x):
    return 1.0 / (1.0 + jnp.exp(-x))


def _softplus(x):
    return jnp.maximum(x, 0.0) + jnp.log1p(jnp.exp(-jnp.abs(x)))


def _iota(shape, dim):
    return lax.broadcasted_iota(jnp.int32, shape, dim)


IN_GROUPS = (("conv", CONV_CH), ("z", GDN_V + SSM_DI), ("q", NSA_Q), ("kv", 6 * NSA_KV), ("small", SMALL_W))
IN_CAT = sum(w for _, w in IN_GROUPS)


def _inproj_kernel(x_ref, g_ref, w_ref, conv_ref, z_ref, q_ref, kv_ref, small_ref):
    x = x_ref[...]
    ms = jnp.mean(x * x, axis=-1, keepdims=True)
    xn = (x * lax.rsqrt(ms + EPS) * g_ref[...]).astype(BF16)
    off = 0
    for ref, (_, w) in zip((conv_ref, z_ref, q_ref, kv_ref, small_ref), IN_GROUPS):
        ref[...] = _dot(xn, w_ref[:, off:off + w])
        off += w


def _inproj(x2d, norm_g, w_cat):
    n = x2d.shape[0]
    tm = min(n, 512)
    return pl.pallas_call(
        _inproj_kernel,
        grid=(n // tm,),
        in_specs=[pl.BlockSpec((tm, D_MODEL), lambda i: (i, 0)),
                  pl.BlockSpec((1, D_MODEL), lambda i: (0, 0)),
                  pl.BlockSpec((D_MODEL, IN_CAT), lambda i: (0, 0))],
        out_specs=[pl.BlockSpec((tm, w), lambda i: (i, 0)) for _, w in IN_GROUPS],
        out_shape=[jax.ShapeDtypeStruct((n, w), F32) for _, w in IN_GROUPS],
        compiler_params=_cparams("parallel"),
        name="inproj",
    )(x2d, norm_g.reshape(1, D_MODEL), w_cat)


def _cat_w_in(w_in):
    o = 0
    parts = {}
    for name, w in (("conv", CONV_CH), ("gdn_z", GDN_V), ("gdn_b", GDN_HEADS), ("gdn_a", GDN_HEADS),
                    ("ssm_z", SSM_DI), ("ssm_dt", SSM_HEADS), ("nsa_q", NSA_Q), ("nsa_kv", 6 * NSA_KV),
                    ("nsa_g", 3 * NSA_HEADS)):
        parts[name] = w_in[:, o:o + w]
        o += w
    small = jnp.concatenate([parts["gdn_b"], parts["gdn_a"], parts["ssm_dt"], parts["nsa_g"]], axis=1)
    small = jnp.pad(small, ((0, 0), (0, SMALL_W - small.shape[1])))
    return jnp.concatenate([parts["conv"], parts["gdn_z"], parts["ssm_z"], parts["nsa_q"], parts["nsa_kv"], small],
                           axis=1).astype(BF16)


HALO = 8


def _conv_kernel(x_ref, hist_ref, w_ref, b_ref, c_ref, nc_ref, xe, *, tm, nt):
    t = pl.program_id(1)

    @pl.when(t == 0)
    def _():
        xe[HALO - 3:HALO, :] = hist_ref[0]

    xe[HALO:HALO + tm, :] = x_ref[0]
    y = b_ref[...] + xe[HALO - 3:HALO - 3 + tm, :] * w_ref[0:1, :]
    for j in range(1, CONV_W):
        y = y + xe[HALO - 3 + j:HALO - 3 + j + tm, :] * w_ref[j:j + 1, :]
    c_ref[0] = y * _sigmoid(y)
    last = xe[HALO + tm - 3:HALO + tm, :]

    @pl.when(t == nt - 1)
    def _():
        nc_ref[0] = last

    xe[HALO - 3:HALO, :] = last


def _conv(x, hist, w, b):
    bsz, t, ch = x.shape
    tm = min(t, 512)
    nt = t // tm
    return pl.pallas_call(
        functools.partial(_conv_kernel, tm=tm, nt=nt),
        grid=(bsz, nt),
        in_specs=[pl.BlockSpec((1, tm, ch), lambda i, j: (i, j, 0)),
                  pl.BlockSpec((1, CONV_W - 1, ch), lambda i, j: (i, 0, 0)),
                  pl.BlockSpec((CONV_W, ch), lambda i, j: (0, 0)),
                  pl.BlockSpec((1, ch), lambda i, j: (0, 0))],
        out_specs=[pl.BlockSpec((1, tm, ch), lambda i, j: (i, j, 0)),
                   pl.BlockSpec((1, CONV_W - 1, ch), lambda i, j: (i, 0, 0))],
        out_shape=[jax.ShapeDtypeStruct((bsz, t, ch), F32),
                   jax.ShapeDtypeStruct((bsz, CONV_W - 1, ch), F32)],
        scratch_shapes=[pltpu.VMEM((HALO + tm, ch), F32)],
        compiler_params=_cparams("parallel", "arbitrary"),
        name="conv",
    )(x, hist, w, b.reshape(1, ch))


def _tri(n, strict=False):
    r, c = _iota((n, n), 0), _iota((n, n), 1)
    return (r > c) if strict else (r >= c)


def _row_of(col):
    n = col.shape[0]
    eye = (_iota((n, n), 0) == _iota((n, n), 1)).astype(F32)
    return _dot_sel(jnp.ones((n, n), F32), eye * col)


def _gdn_kernel(q_ref, k_ref, v_ref, z_ref, small_ref, s0_ref, alog_ref, dtb_ref, ng_ref, o_ref, s_ref, st,
                *, nc, t_valid, bb):
    c = pl.program_id(1)

    @pl.when(c == 0)
    def _():
        st[...] = s0_ref[...]

    incl = _tri(CHUNK)
    strict = _tri(CHUNK, strict=True)
    if t_valid is not None:
        valid = (c * CHUNK + _iota((CHUNK, 1), 0)) < t_valid
    chains = [(b, h) for b in range(bb) for h in range(GDN_HEADS)]
    beta_all, gc_all = {}, {}
    for b in range(bb):
        sm = small_ref[b]
        bt = _sigmoid(sm[:, 0:GDN_HEADS])
        g_all = -jnp.exp(alog_ref[...]) * _softplus(sm[:, GDN_HEADS:2 * GDN_HEADS] + dtb_ref[...])
        if t_valid is not None:
            bt = jnp.where(valid, bt, 0.0)
            g_all = jnp.where(valid, g_all, 0.0)
        beta_all[b] = bt
        gc_all[b] = _dot_sel(incl.astype(F32), g_all)
    q, k, v, beta, gc = {}, {}, {}, {}, {}
    for c_ in chains:
        b, h = c_
        sl = slice(h * GDN_DK, (h + 1) * GDN_DK)
        qq = q_ref[b][:, sl]
        kk = k_ref[b][:, sl]
        vv = v_ref[b][:, sl]
        qq = qq * lax.rsqrt(jnp.sum(qq * qq, axis=-1, keepdims=True) + EPS) * (GDN_DK ** -0.5)
        kk = kk * lax.rsqrt(jnp.sum(kk * kk, axis=-1, keepdims=True) + EPS)
        if t_valid is not None:
            qq = jnp.where(valid, qq, 0.0)
            kk = jnp.where(valid, kk, 0.0)
            vv = jnp.where(valid, vv, 0.0)
        q[c_], k[c_], v[c_] = qq, kk, vv
        beta[c_] = beta_all[b][:, h:h + 1]
        gc[c_] = gc_all[b][:, h:h + 1]
    diff = {c_: gc[c_] - _row_of(gc[c_]) for c_ in chains}
    kkt = {c_: _dot_nt(k[c_], k[c_], HI) for c_ in chains}
    qkt = {c_: _dot_nt(q[c_], k[c_], HI) for c_ in chains}
    egc = {c_: jnp.exp(gc[c_]) for c_ in chains}
    p = {c_: -(beta[c_] * kkt[c_] * jnp.where(strict, jnp.exp(jnp.where(strict, diff[c_], 0.0)), 0.0))
         for c_ in chains}
    x = {c_: jnp.concatenate([v[c_] * beta[c_], k[c_] * (beta[c_] * egc[c_])], axis=-1) for c_ in chains}
    for it in range(6):
        px = {c_: _dot_hi(p[c_], x[c_]) for c_ in chains}
        if it < 5:
            p = {c_: _dot_hi(p[c_], p[c_]) for c_ in chains}
        x = {c_: x[c_] + px[c_] for c_ in chains}
    aqk = {c_: qkt[c_] * jnp.where(incl, jnp.exp(jnp.where(incl, diff[c_], 0.0)), 0.0) for c_ in chains}
    s = {c_: st[c_[0], c_[1]] for c_ in chains}
    wks = {c_: _dot_hi(x[c_][:, GDN_DV:], s[c_]) for c_ in chains}
    qs = {c_: _dot_hi(q[c_] * egc[c_], s[c_]) for c_ in chains}
    w = {c_: x[c_][:, :GDN_DV] - wks[c_] for c_ in chains}
    gl = {c_: gc[c_][CHUNK - 1:CHUNK, :] for c_ in chains}
    kdw = {c_: _dot_tn(k[c_] * jnp.exp(gl[c_] - gc[c_]), w[c_], HI) for c_ in chains}
    aw = {c_: _dot_hi(aqk[c_], w[c_]) for c_ in chains}
    for c_ in chains:
        st[c_[0], c_[1]] = jnp.exp(gl[c_]) * s[c_] + kdw[c_]
    for b in range(bb):
        outs = []
        for h in range(GDN_HEADS):
            o = qs[(b, h)] + aw[(b, h)]
            o = o * lax.rsqrt(jnp.mean(o * o, axis=-1, keepdims=True) + EPS) * ng_ref[...]
            zh = z_ref[b][:, h * GDN_DK:(h + 1) * GDN_DK]
            outs.append(o * (zh * _sigmoid(zh)))
        o_ref[b] = jnp.concatenate(outs, axis=-1)

    @pl.when(c == nc - 1)
    def _():
        s_ref[...] = st[...]


GDN_BB = 2


def _gdn(c, z, small, s0, a_log, dt_bias, norm_g, t_valid=None):
    bsz, t, _ = c.shape
    nc = t // CHUNK
    bb = GDN_BB
    return pl.pallas_call(
        functools.partial(_gdn_kernel, nc=nc, t_valid=t_valid, bb=bb),
        grid=(bsz // bb, nc),
        in_specs=[pl.BlockSpec((bb, CHUNK, GDN_QK), lambda i, j: (i, j, 0)),
                  pl.BlockSpec((bb, CHUNK, GDN_QK), lambda i, j: (i, j, 1)),
                  pl.BlockSpec((bb, CHUNK, GDN_V), lambda i, j: (i, j, 2)),
                  pl.BlockSpec((bb, CHUNK, GDN_V), lambda i, j: (i, j, 0)),
                  pl.BlockSpec((bb, CHUNK, SMALL_W), lambda i, j: (i, j, 0)),
                  pl.BlockSpec((bb, GDN_HEADS, GDN_DK, GDN_DV), lambda i, j: (i, 0, 0, 0)),
                  pl.BlockSpec((1, GDN_HEADS), lambda i, j: (0, 0)),
                  pl.BlockSpec((1, GDN_HEADS), lambda i, j: (0, 0)),
                  pl.BlockSpec((1, GDN_DV), lambda i, j: (0, 0))],
        out_specs=[pl.BlockSpec((bb, CHUNK, GDN_V), lambda i, j: (i, j, 0)),
                   pl.BlockSpec((bb, GDN_HEADS, GDN_DK, GDN_DV), lambda i, j: (i, 0, 0, 0))],
        out_shape=[jax.ShapeDtypeStruct((bsz, t, GDN_V), F32),
                   jax.ShapeDtypeStruct((bsz, GDN_HEADS, GDN_DK, GDN_DV), F32)],
        scratch_shapes=[pltpu.VMEM((bb, GDN_HEADS, GDN_DK, GDN_DV), F32)],
        compiler_params=_cparams("parallel", "arbitrary"),
        name="gdn",
    )(c, c, c, z, small, s0, a_log.reshape(1, -1), dt_bias.reshape(1, -1), norm_g.reshape(1, -1))


def _ssd_kernel(x_ref, b_ref, c_ref, z_ref, small_ref, h0_ref, alog_ref, dtb_ref, dsk_ref, ng_ref, o_ref, h_ref, st,
                *, nc, t_valid):
    ci = pl.program_id(1)

    @pl.when(ci == 0)
    def _():
        st[...] = h0_ref[0]

    sm = small_ref[0]
    dt_all = _softplus(sm[:, 2 * GDN_HEADS:2 * GDN_HEADS + SSM_HEADS] + dtb_ref[...])
    xs = x_ref[0]
    bm_all = b_ref[0]
    cm_all = c_ref[0]
    if t_valid is not None:
        valid = (ci * CHUNK + _iota((CHUNK, 1), 0)) < t_valid
        dt_all = jnp.where(valid, dt_all, 0.0)
        xs = jnp.where(valid, xs, 0.0)
        bm_all = jnp.where(valid, bm_all, 0.0)
        cm_all = jnp.where(valid, cm_all, 0.0)
    incl = _tri(CHUNK)
    cs_all = _dot_sel(incl.astype(F32), dt_all * (-jnp.exp(alog_ref[...])))
    rep = SSM_HEADS // SSM_GROUPS
    ys = []
    for h in range(SSM_HEADS):
        g = h // rep
        x = xs[:, h * SSM_HEADDIM:(h + 1) * SSM_HEADDIM]
        bm = bm_all[:, g * SSM_DSTATE:(g + 1) * SSM_DSTATE]
        cm = cm_all[:, g * SSM_DSTATE:(g + 1) * SSM_DSTATE]
        dt = dt_all[:, h:h + 1]
        cs = cs_all[:, h:h + 1]
        diff = cs - _row_of(cs)
        seg = jnp.where(incl, jnp.exp(jnp.where(incl, diff, 0.0)), 0.0)
        xdt = x * dt
        cb = _dot_nt(cm, bm, HI) * seg
        cl = cs[CHUNK - 1:CHUNK, :]
        hs = st[h]
        y = _dot_hi(cb, xdt) + _dot_nt(cm * jnp.exp(cs), hs, HI)
        st[h] = jnp.exp(cl) * hs + _dot_tn(xdt * jnp.exp(cl - cs), bm, HI)
        ys.append(y + dsk_ref[0:1, h:h + 1] * x)
    y = jnp.concatenate(ys, axis=-1)
    zz = z_ref[0]
    y = y * (zz * _sigmoid(zz))
    gs = SSM_DI // SSM_GROUPS
    outs = []
    for g in range(SSM_GROUPS):
        yg = y[:, g * gs:(g + 1) * gs]
        outs.append(yg * lax.rsqrt(jnp.mean(yg * yg, axis=-1, keepdims=True) + EPS) * ng_ref[0:1, g * gs:(g + 1) * gs])
    o_ref[0] = jnp.concatenate(outs, axis=-1)

    @pl.when(ci == nc - 1)
    def _():
        h_ref[0] = st[...]


def _ssd(c, z, small, h0, a_log, dt_bias, d_skip, norm_g, t_valid=None):
    bsz, t, _ = c.shape
    nc = t // CHUNK
    w = SSM_DI
    return pl.pallas_call(
        functools.partial(_ssd_kernel, nc=nc, t_valid=t_valid),
        grid=(bsz, nc),
        in_specs=[pl.BlockSpec((1, CHUNK, w), lambda i, j: (i, j, 3)),
                  pl.BlockSpec((1, CHUNK, w), lambda i, j: (i, j, 4)),
                  pl.BlockSpec((1, CHUNK, w), lambda i, j: (i, j, 5)),
                  pl.BlockSpec((1, CHUNK, w), lambda i, j: (i, j, 1)),
                  pl.BlockSpec((1, CHUNK, SMALL_W), lambda i, j: (i, j, 0)),
                  pl.BlockSpec((1, SSM_HEADS, SSM_HEADDIM, SSM_DSTATE), lambda i, j: (i, 0, 0, 0)),
                  pl.BlockSpec((1, SSM_HEADS), lambda i, j: (0, 0)),
                  pl.BlockSpec((1, SSM_HEADS), lambda i, j: (0, 0)),
                  pl.BlockSpec((1, SSM_HEADS), lambda i, j: (0, 0)),
                  pl.BlockSpec((1, SSM_DI), lambda i, j: (0, 0))],
        out_specs=[pl.BlockSpec((1, CHUNK, SSM_DI), lambda i, j: (i, j, 0)),
                   pl.BlockSpec((1, SSM_HEADS, SSM_HEADDIM, SSM_DSTATE), lambda i, j: (i, 0, 0, 0))],
        out_shape=[jax.ShapeDtypeStruct((bsz, t, SSM_DI), F32),
                   jax.ShapeDtypeStruct((bsz, SSM_HEADS, SSM_HEADDIM, SSM_DSTATE), F32)],
        scratch_shapes=[pltpu.VMEM((SSM_HEADS, SSM_HEADDIM, SSM_DSTATE), F32)],
        compiler_params=_cparams("parallel", "arbitrary"),
        name="ssd",
    )(c, c, c, z, small, h0, a_log.reshape(1, -1), dt_bias.reshape(1, -1), d_skip.reshape(1, -1),
      norm_g.reshape(1, -1))


PAGE = 128
NEG = -1e30


def _rope_tables(pos):
    half = HEAD_DIM // 2
    inv = ROPE_THETA ** (-jnp.arange(half, dtype=F32) * 2.0 / HEAD_DIM)
    ang = pos.astype(F32)[:, None] * inv[None, :]
    cos, sin = jnp.cos(ang), jnp.sin(ang)
    return jnp.tile(cos, (1, 4)), jnp.tile(jnp.concatenate([-sin, sin], axis=1), (1, 2))


def _nsaproj_kernel(q_ref, kv_ref, small_ref, cos_ref, sin_ref, qn_ref, bd_ref,
                    qo_ref, rows_ref, win_ref, gates_ref, ks_ref, vs_ref, kw_ref, vw_ref):
    cosf = cos_ref[...]
    sins = sin_ref[...]
    bd = bd_ref[...]
    first_half = (_iota((1, LANE), 1) % HEAD_DIM) < (HEAD_DIM // 2)

    def normrope(x, g):
        y = x * lax.rsqrt(_dot_hi(x * x, bd) + EPS) * g
        rot = jnp.where(first_half, pltpu.roll(y, LANE - HEAD_DIM // 2, 1), pltpu.roll(y, HEAD_DIM // 2, 1))
        return y * cosf + rot * sins

    qr = q_ref[0]
    qo_ref[0] = jnp.concatenate(
        [normrope(qr[:, i * LANE:(i + 1) * LANE], qn_ref[0:1, :]) * (HEAD_DIM ** -0.5) for i in range(NSA_Q // LANE)],
        axis=-1)
    kv = kv_ref[0]
    kc = normrope(kv[:, 0:LANE], qn_ref[1:2, :])
    vc = kv[:, LANE:2 * LANE]
    ks = normrope(kv[:, 2 * LANE:3 * LANE], qn_ref[2:3, :])
    vs = kv[:, 3 * LANE:4 * LANE]
    kw = normrope(kv[:, 4 * LANE:5 * LANE], qn_ref[3:4, :])
    vw = kv[:, 5 * LANE:6 * LANE]
    rows_ref[0] = jnp.concatenate([kc, vc, ks, vs], axis=-1)
    win_ref[0] = jnp.concatenate([kw, vw], axis=-1)
    gates_ref[0] = _sigmoid(small_ref[0])
    ones = jnp.ones((ks.shape[0], LANE - HEAD_DIM), BF16)
    for h in range(NSA_KV_HEADS):
        sl = slice(h * HEAD_DIM, (h + 1) * HEAD_DIM)
        ks_ref[0, h] = ks[:, sl].astype(BF16)
        vs_ref[0, h] = jnp.concatenate([vs[:, sl].astype(BF16), ones], axis=-1)
        kw_ref[0, h] = kw[:, sl].astype(BF16)
        vw_ref[0, h] = jnp.concatenate([vw[:, sl].astype(BF16), ones], axis=-1)


def _nsaproj(q_raw, kv_raw, small, pos, qk_norm):
    bsz, t, _ = q_raw.shape
    tm = min(t, 512)
    cos, sin = _rope_tables(pos)
    qn = jnp.tile(qk_norm, (1, 2))
    bd = jnp.kron(jnp.eye(2, dtype=F32), jnp.full((HEAD_DIM, HEAD_DIM), 1.0 / HEAD_DIM, F32))
    tok = lambda w: pl.BlockSpec((1, tm, w), lambda i, j: (i, j, 0))
    return pl.pallas_call(
        _nsaproj_kernel,
        grid=(bsz, t // tm),
        in_specs=[tok(NSA_Q), tok(6 * NSA_KV), tok(SMALL_W),
                  pl.BlockSpec((tm, LANE), lambda i, j: (j, 0)),
                  pl.BlockSpec((tm, LANE), lambda i, j: (j, 0)),
                  pl.BlockSpec((4, LANE), lambda i, j: (0, 0)),
                  pl.BlockSpec((LANE, LANE), lambda i, j: (0, 0))],
        out_specs=[tok(NSA_Q), tok(4 * NSA_KV), tok(2 * NSA_KV), tok(SMALL_W)]
                  + [pl.BlockSpec((1, NSA_KV_HEADS, tm, w), lambda i, j: (i, 0, j, 0))
                     for w in (HEAD_DIM, LANE, HEAD_DIM, LANE)],
        out_shape=[jax.ShapeDtypeStruct((bsz, t, NSA_Q), F32),
                   jax.ShapeDtypeStruct((bsz, t, 4 * NSA_KV), F32),
                   jax.ShapeDtypeStruct((bsz, t, 2 * NSA_KV), F32),
                   jax.ShapeDtypeStruct((bsz, t, SMALL_W), F32)]
                  + [jax.ShapeDtypeStruct((bsz, NSA_KV_HEADS, t, w), BF16) for w in (HEAD_DIM, LANE, HEAD_DIM, LANE)],
        compiler_params=_cparams("parallel", "parallel"),
        name="nsaproj",
    )(q_raw, kv_raw, small, cos, sin, qn, bd)


CMP_PAGES = 16


def _compress_kernel(pt_ref, *refs):
    del pt_ref
    page_refs = refs[:CMP_PAGES]
    pe_ref, phi_ref, ck_ref, cv_ref, bufk, bufv = refs[CMP_PAGES:]
    for i, r in enumerate(page_refs):
        bufk[i * PAGE:(i + 1) * PAGE, :] = r[0, :, 0:LANE]
        bufv[i * PAGE:(i + 1) * PAGE, :] = r[0, :, LANE:2 * LANE]
    nblk = CMP_PAGES * PAGE // NSA_BLOCK

    def body(l, carry):
        ak, av = carry
        kl = bufk[pl.ds(l, nblk, stride=NSA_BLOCK), :] + pe_ref[0, pl.ds(l, 1), :]
        vl = bufv[pl.ds(l, nblk, stride=NSA_BLOCK), :] + pe_ref[1, pl.ds(l, 1), :]
        return ak + _dot_hi(kl, phi_ref[0, l]), av + _dot_hi(vl, phi_ref[1, l])

    zero = jnp.zeros((nblk, LANE), F32)
    ak, av = lax.fori_loop(0, NSA_BLOCK, body, (zero, zero))
    ck_ref[0] = ak
    cv_ref[0] = av


def _compress(pages, page_table, pe, phi):
    bsz, npages = page_table.shape
    steps = npages // CMP_PAGES
    nblk = CMP_PAGES * PAGE // NSA_BLOCK
    pe2 = jnp.tile(pe, (1, 1, 2))
    eye2 = jnp.eye(2, dtype=F32)
    phi_bd = jnp.einsum('ab,klde->kladbe', eye2, phi).reshape(2, NSA_BLOCK, LANE, LANE)
    page_specs = [pl.BlockSpec((1, PAGE, 2 * LANE), lambda b, j, pt, i=i: (pt[b, j * CMP_PAGES + i], 0, 0))
                  for i in range(CMP_PAGES)]
    return pl.pallas_call(
        _compress_kernel,
        grid_spec=pltpu.PrefetchScalarGridSpec(
            num_scalar_prefetch=1,
            grid=(bsz, steps),
            in_specs=page_specs + [pl.BlockSpec((2, NSA_BLOCK, LANE), lambda b, j, pt: (0, 0, 0)),
                                   pl.BlockSpec((2, NSA_BLOCK, LANE, LANE), lambda b, j, pt: (0, 0, 0, 0))],
            out_specs=[pl.BlockSpec((1, nblk, LANE), lambda b, j, pt: (b, j, 0)),
                       pl.BlockSpec((1, nblk, LANE), lambda b, j, pt: (b, j, 0))],
            scratch_shapes=[pltpu.VMEM((CMP_PAGES * PAGE, LANE), F32)] * 2),
        out_shape=[jax.ShapeDtypeStruct((bsz, steps * nblk, LANE), F32)] * 2,
        compiler_params=_cparams("parallel", "parallel"),
        name="compress",
    )(page_table, *([pages] * CMP_PAGES), pe2, phi_bd)


QBLK = 128
KTILE = 512


def _colmax(x):
    return jnp.max(x, axis=0, keepdims=True)


def _colsum(x):
    return jnp.sum(x, axis=0, keepdims=True)


def _tile4(x):
    return jnp.concatenate([x] * NSA_GROUP, axis=1)


def _select_blocks(imp, q0, nb):
    nio = _iota(imp.shape, 0)
    nio_f = nio.astype(F32)
    cur = (q0 + _iota((1, imp.shape[1]), 1)) // NSA_BLOCK
    forced = (nio == 0) | (nio == cur) | (nio == cur - 1)
    causal = nio <= cur
    work = jnp.where(forced, jnp.inf, jnp.where(causal, imp, -jnp.inf))
    sel = jnp.zeros(imp.shape, F32)
    for _ in range(min(NSA_TOPK, nb)):
        m = _colmax(work)
        idx = jnp.min(jnp.where(work == m, nio_f, float(nb)), axis=0, keepdims=True)
        pick = nio_f == idx
        sel = jnp.where(pick, 1.0, sel)
        work = jnp.where(pick, -jnp.inf, work)
    return jnp.where(causal, sel, 0.0)


def _nsa_prompt_kernel(q_ref, ck_ref, cv_ref, ks_ref, vs_ref, kw_ref, vw_ref, gt_ref, wb_ref, o_ref, sel_sc, *,
                       t_total):
    nb = t_total // NSA_BLOCK
    kt = min(KTILE, t_total)
    wk = min(NSA_WINDOW + QBLK, t_total)
    bpt = kt // NSA_BLOCK
    qi = pl.program_id(1)
    q0 = qi * QBLK
    qall = q_ref[0]
    gt = gt_ref[0]
    qpos = _tile4(q0 + _iota((1, QBLK), 1))
    eye = (_iota((HEAD_DIM, HEAD_DIM), 0) == _iota((HEAD_DIM, HEAD_DIM), 1)).astype(BF16)
    heads = range(NSA_KV_HEADS)
    qsb, oc = [], []
    for h in heads:
        qs = jnp.concatenate([qall[:, (h * NSA_GROUP + g) * HEAD_DIM:(h * NSA_GROUP + g + 1) * HEAD_DIM]
                              for g in range(NSA_GROUP)], axis=0)
        qsb.append(qs.astype(BF16))
        sl = slice(h * HEAD_DIM, (h + 1) * HEAD_DIM)
        s = _dot_nt(ck_ref[0][:, sl], qs, HI)
        mask = ((_iota((nb, 1), 0) + 1) * NSA_BLOCK - 1) <= qpos
        m = _colmax(jnp.where(mask, s, NEG))
        p = jnp.where(mask, jnp.exp(s - m), 0.0)
        p = p / jnp.maximum(_colsum(p), 1e-30)
        oc.append(_dot_tn(cv_ref[0][:, sl], p, HI))
        imp = p[:, 0:QBLK]
        for g in range(1, NSA_GROUP):
            imp = imp + p[:, g * QBLK:(g + 1) * QBLK]
        sel_sc[h] = (_select_blocks(imp, q0, nb) - 1.0) * (-NEG)

    def body(i, carry, diagonal):
        k0 = pl.multiple_of(i * kt, kt)
        sc = []
        for h in heads:
            brows = sel_sc[h, pl.ds(pl.multiple_of(i * bpt, bpt), bpt), :]
            bias = jnp.concatenate([jnp.broadcast_to(brows[j:j + 1, :], (NSA_BLOCK, QBLK)) for j in range(bpt)],
                                   axis=0)
            s_h = _dot_nt(ks_ref[0, h, pl.ds(k0, kt), :], qsb[h]) + _tile4(bias)
            if diagonal:
                s_h = jnp.where((k0 + _iota((kt, 1), 0)) <= qpos, s_h, NEG)
            sc.append(s_h)
        m_n = [jnp.maximum(carry[h][0], _colmax(sc[h])) for h in heads]
        pp = [jnp.exp((sc[h] - m_n[h]).astype(BF16)) for h in heads]
        pv = [_dot_tn(vs_ref[0, h, pl.ds(k0, kt), :], pp[h]) for h in heads]
        return tuple((m_n[h], carry[h][1] * jnp.exp(carry[h][0] - m_n[h]) + pv[h]) for h in heads)

    ntile = (q0 + QBLK + kt - 1) // kt
    init = (jnp.full((1, NSA_GROUP * QBLK), NEG, F32), jnp.zeros((LANE, NSA_GROUP * QBLK), F32))
    carry = lax.fori_loop(0, ntile - 1, functools.partial(body, diagonal=False), (init,) * NSA_KV_HEADS)
    carry = body(ntile - 1, carry, True)
    w0 = pl.multiple_of(jnp.maximum(q0 + QBLK - wk, 0), QBLK)
    wbias = _tile4(wb_ref[pl.ds(pl.multiple_of(wk - QBLK - (q0 - w0), QBLK), wk), :])
    outs = []
    for h in heads:
        acc_s = carry[h][1]
        osel = acc_s[0:HEAD_DIM] / jnp.maximum(acc_s[HEAD_DIM:HEAD_DIM + 1], 1e-30)
        sw = _dot_nt(kw_ref[0, h, pl.ds(w0, wk), :], qsb[h]) + wbias
        pw = jnp.exp((sw - _colmax(sw)).astype(BF16))
        acc_w = _dot_tn(vw_ref[0, h, pl.ds(w0, wk), :], pw)
        ow = acc_w[0:HEAD_DIM] / jnp.maximum(acc_w[HEAD_DIM:HEAD_DIM + 1], 1e-30)
        gate = [jnp.concatenate([gt[(h * NSA_GROUP + g) * 3 + j:(h * NSA_GROUP + g) * 3 + j + 1, :]
                                 for g in range(NSA_GROUP)], axis=1) for j in range(3)]
        ot = gate[0] * oc[h] + gate[1] * osel + gate[2] * ow
        for g in range(NSA_GROUP):
            outs.append(_dot_tn(ot[:, g * QBLK:(g + 1) * QBLK].astype(BF16), eye))
    o_ref[0] = jnp.concatenate(outs, axis=-1)


def _nsa_prompt(q, ck, cv, ks, vs, kw, vw, gates_t):
    bsz, t, _ = q.shape
    nb = t // NSA_BLOCK
    full = lambda w: pl.BlockSpec((1, NSA_KV_HEADS, t, w), lambda i, j: (i, 0, 0, 0))
    wk = min(NSA_WINDOW + QBLK, t)
    omax = wk - QBLK
    delta = omax + _iota((wk + omax, QBLK), 1) - _iota((wk + omax, QBLK), 0)
    wbias = jnp.where((delta >= 0) & (delta < NSA_WINDOW), 0.0, NEG).astype(F32)
    return pl.pallas_call(
        functools.partial(_nsa_prompt_kernel, t_total=t),
        grid=(bsz, t // QBLK),
        in_specs=[pl.BlockSpec((1, QBLK, NSA_Q), lambda i, j: (i, j, 0)),
                  pl.BlockSpec((1, nb, LANE), lambda i, j: (i, 0, 0)),
                  pl.BlockSpec((1, nb, LANE), lambda i, j: (i, 0, 0)),
                  full(HEAD_DIM), full(LANE), full(HEAD_DIM), full(LANE),
                  pl.BlockSpec((1, 3 * NSA_HEADS, QBLK), lambda i, j: (i, 0, j)),
                  pl.BlockSpec((wk + omax, QBLK), lambda i, j: (0, 0))],
        out_specs=pl.BlockSpec((1, QBLK, NSA_Q), lambda i, j: (i, j, 0)),
        out_shape=jax.ShapeDtypeStruct((bsz, t, NSA_Q), F32),
        scratch_shapes=[pltpu.VMEM((NSA_KV_HEADS, nb, QBLK), F32)],
        compiler_params=_cparams("parallel", "arbitrary"),
        name="nsa_prompt",
    )(q, ck, cv, ks, vs, kw, vw, gates_t, wbias)


def _group_queries(q, h):
    return jnp.concatenate([q[:, (h * NSA_GROUP + g) * HEAD_DIM:(h * NSA_GROUP + g + 1) * HEAD_DIM]
                            for g in range(NSA_GROUP)] + [jnp.zeros((8 - NSA_GROUP, HEAD_DIM), F32)], axis=0)


def _dot_x3(a, w_hi, w_lo):
    a_hi, a_lo = _split_bf16(a)
    return _dot(a_hi, w_hi) + _dot(a_lo, w_hi) + _dot(a_hi, w_lo)


def _compress_t_kernel(pt_ref, *refs, npg):
    del pt_ref
    page_refs = refs[:npg]
    pe_ref, whi_ref, wlo_ref, ck_ref, cv_ref, buf = refs[npg:]
    rows = 2 * LANE
    for i, r in enumerate(page_refs):
        buf[i * rows:(i + 1) * rows, :] = r[0]

    def body(d, carry):
        accs = list(carry)
        for kv in range(2):
            a = jnp.concatenate([buf[pl.ds(kv * LANE + h * HEAD_DIM + d, npg, stride=rows), :]
                                 for h in range(NSA_KV_HEADS)], axis=0) + pe_ref[kv, pl.ds(d, 1), :]
            if kv == 0:
                accs[kv] = accs[kv] + _dot_x3(a, whi_ref[kv, d], wlo_ref[kv, d])
            else:
                accs[kv] = accs[kv] + _dot(a.astype(BF16), whi_ref[kv, d])
        return tuple(accs)

    zero = jnp.zeros((NSA_KV_HEADS * npg, LANE), F32)
    ak, av = lax.fori_loop(0, HEAD_DIM, body, (zero, zero))
    ck_ref[0] = ak
    cv_ref[0] = av


def _compress_t(cache_t, page_table, pe, phi):
    bsz, npg = page_table.shape
    eye2 = jnp.eye(2, dtype=F32)
    w = jnp.einsum('ab,klde->kdalbe', eye2, phi).reshape(2, HEAD_DIM, LANE, LANE)
    w_hi = w.astype(BF16)
    w_lo = (w - w_hi.astype(F32)).astype(BF16)
    pe_t = jnp.tile(jnp.transpose(pe, (0, 2, 1)), (1, 1, 2))
    page_specs = [pl.BlockSpec((1, 2 * LANE, PAGE), lambda b, pt, i=i: (pt[b, i], 0, 0)) for i in range(npg)]
    const = lambda shape: pl.BlockSpec(shape, lambda b, pt: (0,) * len(shape))
    out = pl.BlockSpec((1, NSA_KV_HEADS * npg, LANE), lambda b, pt: (b, 0, 0))
    return pl.pallas_call(
        functools.partial(_compress_t_kernel, npg=npg),
        grid_spec=pltpu.PrefetchScalarGridSpec(
            num_scalar_prefetch=1,
            grid=(bsz,),
            in_specs=page_specs + [const((2, HEAD_DIM, LANE)), const((2, HEAD_DIM, LANE, LANE)),
                                   const((2, HEAD_DIM, LANE, LANE))],
            out_specs=[out, out],
            scratch_shapes=[pltpu.VMEM((npg * 2 * LANE, PAGE), F32)]),
        out_shape=[jax.ShapeDtypeStruct((bsz, NSA_KV_HEADS * npg, LANE), F32)] * 2,
        compiler_params=_cparams("parallel"),
        name="compress_t",
    )(page_table, *([cache_t] * npg), pe_t, w_hi, w_lo)


def _nsa_sample_cmp_kernel(q_ref, ck_ref, cv_ref, oc_ref, idx_ref, *, past, npg):
    nbc = 2 * npg
    cur = past // NSA_BLOCK
    q = q_ref[0]
    lane = _iota((1, nbc), 1)
    blk = 2 * (lane % npg) + lane // npg
    mask = ((blk + 1) * NSA_BLOCK - 1) <= past
    ocs, idxs = [], []
    for h in range(NSA_KV_HEADS):
        qh = _group_queries(q, h)
        ckh = ck_ref[0][h * npg:(h + 1) * npg, :]
        cvh = cv_ref[0][h * npg:(h + 1) * npg, :]
        s = jnp.concatenate([_dot_nt(qh, ckh[:, i * HEAD_DIM:(i + 1) * HEAD_DIM], HI) for i in range(2)], axis=1)
        m = jnp.max(jnp.where(mask, s, NEG), axis=1, keepdims=True)
        p = jnp.where(mask, jnp.exp(s - m), 0.0)
        p = p / jnp.maximum(jnp.sum(p, axis=1, keepdims=True), 1e-30)
        oc = _dot_hi(p[:, 0:npg], cvh[:, 0:HEAD_DIM]) + _dot_hi(p[:, npg:nbc], cvh[:, HEAD_DIM:LANE])
        ocs.append(oc[0:NSA_GROUP])
        imp = p[0:1]
        for g in range(1, NSA_GROUP):
            imp = imp + p[g:g + 1]
        forced = (blk == 0) | (blk == cur) | (blk == cur - 1)
        score = jnp.where(forced, jnp.inf, jnp.where(blk <= cur, imp, -jnp.inf))
        avail = jnp.ones((1, nbc), jnp.bool_)
        slot = _iota((1, NSA_TOPK), 1)
        picked = jnp.full((1, NSA_TOPK), cur, jnp.int32)
        for r in range(NSA_TOPK - 1):
            mm = jnp.max(jnp.where(avail, score, -jnp.inf), axis=1, keepdims=True)
            eq = avail & (score == mm)
            idx = jnp.min(jnp.where(eq, blk, nbc), axis=1, keepdims=True)
            avail = avail & (blk != idx)
            picked = jnp.where(slot == r, idx, picked)
        idxs.append(picked)
    oc_ref[0] = jnp.concatenate(ocs, axis=0)
    idx_ref[0] = jnp.concatenate(idxs, axis=1)


def _nsa_sample_cmp(q3, ck, cv, past):
    bsz = q3.shape[0]
    npg = ck.shape[1] // NSA_KV_HEADS
    return pl.pallas_call(
        functools.partial(_nsa_sample_cmp_kernel, past=past, npg=npg),
        grid=(bsz,),
        in_specs=[pl.BlockSpec((1, 1, NSA_Q), lambda b: (b, 0, 0)),
                  pl.BlockSpec((1, NSA_KV_HEADS * npg, LANE), lambda b: (b, 0, 0)),
                  pl.BlockSpec((1, NSA_KV_HEADS * npg, LANE), lambda b: (b, 0, 0))],
        out_specs=[pl.BlockSpec((1, NSA_HEADS, HEAD_DIM), lambda b: (b, 0, 0)),
                   pl.BlockSpec((1, 1, NSA_KV_HEADS * NSA_TOPK), lambda b: (b, 0, 0))],
        out_shape=[jax.ShapeDtypeStruct((bsz, NSA_HEADS, HEAD_DIM), F32),
                   jax.ShapeDtypeStruct((bsz, 1, NSA_KV_HEADS * NSA_TOPK), jnp.int32)],
        compiler_params=_cparams("parallel"),
        name="nsa_sample_cmp",
    )(q3, ck, cv)


def _nsa_sample_attn_kernel(page_ref, half_ref, *refs, past):
    del page_ref
    nsel = NSA_KV_HEADS * NSA_TOPK
    blk_refs = refs[:nsel]
    q_ref, row_ref, nwin_ref, wc_ref, g_ref, oc_ref, o_ref, win_ref = refs[nsel:]
    b = pl.program_id(0)
    wbl = wc_ref.shape[2]
    q = q_ref[0]
    row = row_ref[0]
    nwin = nwin_ref[0]
    gates = g_ref[0]
    oc = oc_ref[0]
    half_of_lane = _iota((1, PAGE), 1) // NSA_BLOCK
    outs = []
    for h in range(NSA_KV_HEADS):
        sl = slice(h * HEAD_DIM, (h + 1) * HEAD_DIM)
        vsl = slice(LANE + h * HEAD_DIM, LANE + (h + 1) * HEAD_DIM)
        qh = _group_queries(q, h)
        ks_new = row[:, 2 * LANE + h * HEAD_DIM:2 * LANE + (h + 1) * HEAD_DIM]
        vs_new = row[:, 3 * LANE + h * HEAD_DIM:3 * LANE + (h + 1) * HEAD_DIM]
        s_new = jnp.sum(qh * ks_new, axis=1, keepdims=True)
        ss, ms = [], []
        for k in range(NSA_TOPK - 1):
            ss.append(_dot_hi(qh, blk_refs[h * NSA_TOPK + k][0, sl, :]))
            ms.append(half_of_lane == half_ref[b, h * NSA_TOPK + k])
        m = s_new
        for s, mk in zip(ss, ms):
            m = jnp.maximum(m, jnp.max(jnp.where(mk, s, NEG), axis=1, keepdims=True))
        p_new = jnp.exp(s_new - m)
        den = p_new
        acc = p_new * vs_new
        for k, (s, mk) in enumerate(zip(ss, ms)):
            p = jnp.where(mk, jnp.exp(s - m), 0.0)
            den = den + jnp.sum(p, axis=1, keepdims=True)
            acc = acc + _dot_nt(p, blk_refs[h * NSA_TOPK + k][0, vsl, :], HI)
        osel = acc / jnp.maximum(den, 1e-30)
        sw = _dot_hi(qh, wc_ref[0, sl, :])
        delta = wbl - _iota((1, wbl), 1)
        mw = (delta >= 0) & (delta < NSA_WINDOW) & ((past - delta) >= 0)
        kw_new = nwin[:, sl]
        vw_new = nwin[:, vsl]
        sw_new = jnp.sum(qh * kw_new, axis=1, keepdims=True)
        m = jnp.maximum(jnp.max(jnp.where(mw, sw, NEG), axis=1, keepdims=True), sw_new)
        pw = jnp.where(mw, jnp.exp(sw - m), 0.0)
        pw_new = jnp.exp(sw_new - m)
        ow = (_dot_nt(pw, wc_ref[0, vsl, :], HI) + pw_new * vw_new) / jnp.maximum(
            jnp.sum(pw, axis=1, keepdims=True) + pw_new, 1e-30)
        for g in range(NSA_GROUP):
            c0 = 12 + (h * NSA_GROUP + g) * 3
            outs.append(gates[:, c0:c0 + 1] * oc[h * NSA_GROUP + g:h * NSA_GROUP + g + 1, :]
                        + gates[:, c0 + 1:c0 + 2] * osel[g:g + 1, :] + gates[:, c0 + 2:c0 + 3] * ow[g:g + 1, :])
    o_ref[0] = jnp.concatenate(outs, axis=1)
    new_col = _dot_sel(jnp.ones((8, wbl), F32), jnp.concatenate([nwin, jnp.zeros((7, 2 * NSA_KV), F32)], axis=0),
                       ((0,), (0,)), sel_right=True)
    win_ref[0] = jnp.where(_iota((1, wbl), 1) == wbl - 1, new_col, pltpu.roll(wc_ref[0], wbl - 1, 1))


def _nsa_sample_attn(cache_t, page_table, idx, q3, row3, nwin3, win_t, win_off, gates3, oc, past):
    bsz = q3.shape[0]
    wbl = win_t.shape[2]
    ncomplete = past // NSA_BLOCK

    n = jnp.clip(idx, 0, ncomplete - 1)
    page_of = jnp.take_along_axis(page_table, n // 2, axis=1)
    half_of = n % 2

    def blk_map(b, pg, hf, j):
        return (pg[b, j], 1, 0)

    blk_specs = [pl.BlockSpec((1, 2 * LANE, PAGE), functools.partial(blk_map, j=j))
                 for j in range(NSA_KV_HEADS * NSA_TOPK)]
    one = lambda w: pl.BlockSpec((1, 1, w), lambda b, pt, ix: (b, 0, 0))
    win_in = pl.BlockSpec((1, 2 * NSA_KV, wbl), lambda b, pt, ix: (b + win_off, 0, 0))
    win_out = pl.BlockSpec((1, 2 * NSA_KV, wbl), lambda b, pt, ix: (b, 0, 0))
    return pl.pallas_call(
        functools.partial(_nsa_sample_attn_kernel, past=past),
        grid_spec=pltpu.PrefetchScalarGridSpec(
            num_scalar_prefetch=2,
            grid=(bsz,),
            in_specs=blk_specs + [one(NSA_Q), one(4 * NSA_KV), one(2 * NSA_KV), win_in, one(SMALL_W),
                                  pl.BlockSpec((1, NSA_HEADS, HEAD_DIM), lambda b, pt, ix: (b, 0, 0))],
            out_specs=[one(NSA_Q), win_out]),
        out_shape=[jax.ShapeDtypeStruct((bsz, 1, NSA_Q), F32),
                   jax.ShapeDtypeStruct((bsz, 2 * NSA_KV, wbl), F32)],
        compiler_params=_cparams("parallel"),
        name="nsa_sample_attn",
    )(page_of, half_of, *([cache_t] * (NSA_KV_HEADS * NSA_TOPK)), q3, row3, nwin3, win_t, gates3, oc)


def _outproj_kernel(og_ref, on_ref, os_ref, x_ref, wo_ref, g_ref, wq_ref, x1_ref, xn_ref, qh_ref):
    x1 = (x_ref[...] + _dot(og_ref[...].astype(BF16), wo_ref[0:GDN_V, :])
          + _dot(on_ref[...].astype(BF16), wo_ref[GDN_V:GDN_V + NSA_Q, :])
          + _dot(os_ref[...].astype(BF16), wo_ref[GDN_V + NSA_Q:, :]))
    x1_ref[...] = x1
    xn = (x1 * lax.rsqrt(jnp.mean(x1 * x1, axis=-1, keepdims=True) + EPS) * g_ref[...]).astype(BF16)
    xn_ref[...] = xn
    qh_ref[...] = _dot(xn, wq_ref[...])


def _outproj(og, on, os_, x, w_out, norm_g, wq):
    n = x.shape[0]
    tm = min(n, 512)
    dq = PEER_HEADS * PEER_DKEY
    tok = lambda w: pl.BlockSpec((tm, w), lambda i: (i, 0))
    const = lambda a, b: pl.BlockSpec((a, b), lambda i: (0, 0))
    return pl.pallas_call(
        _outproj_kernel,
        grid=(n // tm,),
        in_specs=[tok(GDN_V), tok(NSA_Q), tok(SSM_DI), tok(D_MODEL), const(D_MODEL, D_MODEL), const(1, D_MODEL),
                  const(D_MODEL, dq)],
        out_specs=[tok(D_MODEL), tok(D_MODEL), tok(dq)],
        out_shape=[jax.ShapeDtypeStruct((n, D_MODEL), F32), jax.ShapeDtypeStruct((n, D_MODEL), BF16),
                   jax.ShapeDtypeStruct((n, dq), F32)],
        compiler_params=_cparams("parallel"),
        name="outproj",
    )(og, on, os_, x, w_out.astype(BF16), norm_g.reshape(1, D_MODEL), wq.astype(BF16))


PEER_PAIRS = tuple((a, b) for a in range(PEER_TOPK) for b in range(PEER_TOPK) if (a + 1) * (b + 1) <= PEER_TOPK)


def _topk_rows(s, k):
    n = s.shape[0]
    nio = _iota(s.shape, 0).astype(F32)
    work = s
    rank = jnp.full(s.shape, float(k), F32)
    vals = []
    for r in range(k):
        m = _colmax(work)
        idx = jnp.min(jnp.where(work == m, nio, float(n)), axis=0, keepdims=True)
        pick = nio == idx
        rank = jnp.where(pick, float(r), rank)
        work = jnp.where(pick, -jnp.inf, work)
        vals.append(m)
    return rank, vals


def _dup_bf16_bits(x):
    bits = lax.shift_right_logical(pltpu.bitcast(x.astype(BF16).astype(F32), jnp.uint32), jnp.uint32(16))
    return pltpu.bitcast(bits | lax.shift_left(bits, jnp.uint32(16)), jnp.int32)


def _peer_keys_kernel(qh_ref, sk_ref, r2_ref, e2_ref, lx_ref, e1_ref):
    half = PEER_DKEY // 2
    for h in range(PEER_HEADS):
        q1 = qh_ref[:, h * PEER_DKEY:h * PEER_DKEY + half]
        q2 = qh_ref[:, h * PEER_DKEY + half:(h + 1) * PEER_DKEY]
        s1 = _dot_nt(sk_ref[0], q1, HI)
        s2 = _dot_nt(sk_ref[1], q2, HI)
        rank1, v1 = _topk_rows(s1, PEER_TOPK)
        rank2, v2 = _topk_rows(s2, PEER_TOPK)
        npad = -len(PEER_PAIRS) % 8
        cand = jnp.concatenate([v1[a] + v2[b] for a, b in PEER_PAIRS]
                               + [jnp.full((npad, s1.shape[1]), -jnp.inf, F32)], axis=0)
        crank, _ = _topk_rows(cand, PEER_TOPK)
        chosen = crank < PEER_TOPK
        z = _colsum(jnp.where(chosen, jnp.exp(cand - cand[0:1]), 0.0))
        chosen_f = jnp.where(chosen, 1.0, 0.0)
        lx = jnp.zeros(s1.shape, F32)
        for a in range(PEER_TOPK):
            rows = [i for i, (pa, _) in enumerate(PEER_PAIRS) if pa == a]
            cnt = chosen_f[rows[0]:rows[0] + 1]
            for i in rows[1:]:
                cnt = cnt + chosen_f[i:i + 1]
            lx = jnp.where(rank1 == float(a), cnt, lx)
        r2_ref[h] = rank2.astype(BF16)
        e2_ref[h] = (jnp.exp(s2 - v2[0]) / z).astype(BF16)
        lx_ref[h] = _dup_bf16_bits(lx)
        e1_ref[h] = _dup_bf16_bits(jnp.where(rank1 < PEER_TOPK, jnp.exp(s1 - v1[0]), 0.0))


def _peer_keys(qh, subkeys):
    n = qh.shape[0]
    tk = min(n, 256)
    dq = PEER_HEADS * PEER_DKEY
    half = PEER_DKEY // 2
    out = pl.BlockSpec((PEER_HEADS, PEER_KEYS, tk), lambda i: (0, 0, i))
    return pl.pallas_call(
        _peer_keys_kernel,
        grid=(n // tk,),
        in_specs=[pl.BlockSpec((tk, dq), lambda i: (i, 0)),
                  pl.BlockSpec((2, PEER_KEYS, half), lambda i: (0, 0, 0))],
        out_specs=[out] * 4,
        out_shape=[jax.ShapeDtypeStruct((PEER_HEADS, PEER_KEYS, n), dt) for dt in (BF16, BF16, jnp.int32, jnp.int32)],
        compiler_params=_cparams("parallel"),
        name="peer_keys",
    )(qh, subkeys)


PEER_AC = 8


def _gelu_tanh(x):
    c = math.sqrt(2.0 / math.pi)
    return x * (0.5 + 0.5 * jnp.tanh(x * (c + (c * 0.044715) * (x * x))))


def _row_tile_bf16(rows, a):
    t = pltpu.bitcast(jnp.broadcast_to(rows[a:a + 1, :], rows.shape), BF16)
    return jnp.concatenate([t] * (PEER_KEYS // t.shape[0]), axis=0)


PEER_SUB = 4


def _peer_kernel(xn_ref, u_ref, vt_ref, r2_ref, e2_ref, lx_ref, e1_ref, x1_ref, o_ref, acc, *, nsteps):
    j = pl.program_id(1)

    @pl.when(j == 0)
    def _():
        acc[...] = jnp.zeros_like(acc)

    xn = xn_ref[...]
    nsb = PEER_AC // PEER_SUB
    width = PEER_SUB * PEER_KEYS

    def hidden(sb):
        return _dot_nt(u_ref[sb * width:(sb + 1) * width, :], xn)

    assert PEER_AC == 8
    base = pl.multiple_of(j * PEER_AC, PEER_AC)
    lx_rows = [lx_ref[h, pl.ds(base, PEER_AC), :] for h in range(PEER_HEADS)]
    e1_rows = [e1_ref[h, pl.ds(base, PEER_AC), :] for h in range(PEER_HEADS)]

    def gates(sb):
        out = []
        for a in range(sb * PEER_SUB, (sb + 1) * PEER_SUB):
            gate = None
            for h in range(PEER_HEADS):
                term = jnp.where(r2_ref[h] < _row_tile_bf16(lx_rows[h], a), e2_ref[h],
                                 jnp.zeros((), BF16)) * _row_tile_bf16(e1_rows[h], a)
                gate = term if gate is None else gate + term
            out.append(gate)
        return out

    part = None
    ht_next = hidden(0)
    for sb in range(nsb):
        ht = ht_next
        gs = gates(sb)
        if sb + 1 < nsb:
            ht_next = hidden(sb + 1)
        ws = [gs[a] * _gelu_tanh(ht[a * PEER_KEYS:(a + 1) * PEER_KEYS, :]).astype(BF16) for a in range(PEER_SUB)]
        t = _dot(vt_ref[:, sb * width:(sb + 1) * width], jnp.concatenate(ws, axis=0))
        part = t if part is None else t + part
    acc[...] += part

    @pl.when(j == nsteps - 1)
    def _():
        o_ref[...] = x1_ref[...] + acc[...].T


def _peer(xn, u_bf, vt_bf, r2, e2, lx, e1, x1):
    n = xn.shape[0]
    tm = min(n, 512)
    ec = PEER_AC * PEER_KEYS
    nsteps = PEER_KEYS // PEER_AC
    tab = pl.BlockSpec((PEER_HEADS, PEER_KEYS, tm), lambda i, j: (0, 0, i))
    return pl.pallas_call(
        functools.partial(_peer_kernel, nsteps=nsteps),
        grid=(n // tm, nsteps),
        in_specs=[pl.BlockSpec((tm, D_MODEL), lambda i, j: (i, 0)),
                  pl.BlockSpec((ec, D_MODEL), lambda i, j: (j, 0)),
                  pl.BlockSpec((D_MODEL, ec), lambda i, j: (0, j)),
                  tab, tab, tab, tab,
                  pl.BlockSpec((tm, D_MODEL), lambda i, j: (i, 0))],
        out_specs=pl.BlockSpec((tm, D_MODEL), lambda i, j: (i, 0)),
        out_shape=jax.ShapeDtypeStruct((n, D_MODEL), F32),
        scratch_shapes=[pltpu.VMEM((D_MODEL, tm), F32)],
        compiler_params=_cparams("parallel", "arbitrary"),
        name="peer",
    )(xn, u_bf, vt_bf, r2, e2, lx, e1, x1)


def _pad_axis(a, axis, size):
    if a.shape[axis] == size:
        return a
    pad = [(0, 0)] * a.ndim
    pad[axis] = (0, size - a.shape[axis])
    return jnp.pad(a, pad)


def _mixers(x2d, bsz, t, pos, conv_hist, gdn0, ssm0, lp, tokens_as_time=False):
    conv_in, z, q, kv, small = _inproj(x2d, lp["norm_mix"], lp["w_cat"])
    r3 = lambda a: a.reshape(bsz, t, -1)
    c, new_conv = _conv(r3(conv_in), conv_hist, lp["conv_w"], lp["conv_b"])
    tpad = -(-t // CHUNK) * CHUNK
    t_valid = None if tpad == t else t
    cp, zp, sp = (_pad_axis(a, 1, tpad) for a in (c, r3(z), r3(small)))
    og, s_gdn = _gdn(cp, zp, sp, gdn0, lp["gdn_a_log"], lp["gdn_dt_bias"], lp["gdn_norm"], t_valid=t_valid)
    os_, s_ssm = _ssd(cp, zp, sp, ssm0, lp["ssm_a_log"], lp["ssm_dt_bias"], lp["ssm_d"], lp["ssm_norm"],
                      t_valid=t_valid)
    lead = (1, bsz * t) if tokens_as_time else (bsz, t)
    qn, rows, win, gates, ks, vs, kw, vw = _nsaproj(q.reshape(*lead, -1), kv.reshape(*lead, -1),
                                                    small.reshape(*lead, -1), pos, lp["nsa_qk_norm"])
    return dict(og=og[:, :t], os=os_[:, :t], new_conv=new_conv, s_gdn=s_gdn, s_ssm=s_ssm, q=r3(qn), rows=r3(rows),
                win=r3(win), gates=r3(gates), ks=ks, vs=vs, kw=kw, vw=vw)


def _channel_mix(og, on, os_, x, lp):
    x1, xn, qh = _outproj(og, on, os_, x, lp["w_out"], lp["norm_ffn"], lp["peer_wq"])
    n = x.shape[0]
    npad = -(-n // LANE) * LANE
    x1, xn, qh = (_pad_axis(a, 0, npad) for a in (x1, xn, qh))
    r2, e2, lx, e1 = _peer_keys(qh, lp["peer_subkeys"])
    return _peer(xn, lp["peer_u"], lp["peer_vt"], r2, e2, lx, e1, x1)[:n]


def kernel(x_prompt, x_sample, cache_nsa_kv, cache_nsa_win, state_conv, state_gdn, state_ssm, page_table, norm_mix, w_in, conv_w, conv_b, gdn_a_log, gdn_dt_bias, gdn_norm, ssm_a_log, ssm_dt_bias, ssm_d, ssm_norm, nsa_qk_norm, nsa_pe, nsa_phi, w_out, norm_ffn, peer_wq, peer_subkeys, peer_u, peer_v):
    bp, tp, _ = x_prompt.shape
    bs, ts, _ = x_sample.shape
    past = page_table.shape[1] * cache_nsa_kv.shape[2]
    n_pool = cache_nsa_kv.shape[1]
    yp = x_prompt.reshape(bp * tp, D_MODEL)
    ys = x_sample.reshape(bs * ts, D_MODEL)
    depth = norm_mix.shape[0]
    wbl = cache_nsa_win.shape[2]
    cache_t = jnp.transpose(cache_nsa_kv, (0, 1, 3, 4, 5, 2)).reshape(depth * n_pool, 4 * NSA_KV, PAGE)
    win_t = jnp.transpose(cache_nsa_win, (0, 1, 3, 4, 5, 2)).reshape(depth * bs, 2 * NSA_KV, wbl)
    outs_p, outs_s = [], []
    for l in range(norm_mix.shape[0]):
        lp = dict(norm_mix=norm_mix[l], w_cat=_cat_w_in(w_in[l]), conv_w=conv_w[l], conv_b=conv_b[l],
                  gdn_a_log=gdn_a_log[l], gdn_dt_bias=gdn_dt_bias[l], gdn_norm=gdn_norm[l],
                  ssm_a_log=ssm_a_log[l], ssm_dt_bias=ssm_dt_bias[l], ssm_d=ssm_d[l], ssm_norm=ssm_norm[l],
                  nsa_qk_norm=nsa_qk_norm[l], nsa_pe=nsa_pe[l], nsa_phi=nsa_phi[l], w_out=w_out[l],
                  norm_ffn=norm_ffn[l], peer_wq=peer_wq[l], peer_subkeys=peer_subkeys[l],
                  peer_u=peer_u[l].astype(BF16), peer_vt=peer_v[l].T.astype(BF16))
        m = _mixers(yp, bp, tp, jnp.arange(tp, dtype=jnp.int32), jnp.zeros((bp, CONV_W - 1, CONV_CH), F32),
                    jnp.zeros((bp, GDN_HEADS, GDN_DK, GDN_DV), F32),
                    jnp.zeros((bp, SSM_HEADS, SSM_HEADDIM, SSM_DSTATE), F32), lp)
        pages = m["rows"].reshape(bp * tp // PAGE, PAGE, 4 * NSA_KV)
        ident = jnp.arange(bp * tp // PAGE, dtype=jnp.int32).reshape(bp, tp // PAGE)
        ck, cv = _compress(pages, ident, lp["nsa_pe"], lp["nsa_phi"])
        gates_t = jnp.transpose(m["gates"][..., 12:12 + 3 * NSA_HEADS], (0, 2, 1))
        o_nsa = _nsa_prompt(m["q"], ck, cv, m["ks"], m["vs"], m["kw"], m["vw"], gates_t)
        yp = _channel_mix(m["og"].reshape(bp * tp, -1), o_nsa.reshape(bp * tp, -1), m["os"].reshape(bp * tp, -1),
                          yp, lp)
        keep = min(NSA_WINDOW, tp)
        outs_p.append((m["rows"].reshape(bp, tp, 4, NSA_KV_HEADS, HEAD_DIM),
                       m["win"][:, tp - keep:].reshape(bp, keep, 2, NSA_KV_HEADS, HEAD_DIM),
                       m["new_conv"], m["s_gdn"], m["s_ssm"]))
        m = _mixers(ys, bs, ts, jnp.full((bs * ts,), past, jnp.int32), state_conv[l], state_gdn[l], state_ssm[l], lp,
                    tokens_as_time=True)
        pages_l = page_table + l * n_pool
        ck, cv = _compress_t(cache_t, pages_l, lp["nsa_pe"], lp["nsa_phi"])
        q3 = m["q"].reshape(bs, 1, NSA_Q)
        oc, idx = _nsa_sample_cmp(q3, ck, cv, past)
        o_nsa, new_win_t = _nsa_sample_attn(
            cache_t, pages_l, idx.reshape(bs, NSA_KV_HEADS * NSA_TOPK), q3, m["rows"].reshape(bs, 1, 4 * NSA_KV),
            m["win"].reshape(bs, 1, 2 * NSA_KV), win_t, l * bs, m["gates"].reshape(bs, 1, SMALL_W), oc, past)
        new_win = jnp.transpose(new_win_t.reshape(bs, 2, NSA_KV_HEADS, HEAD_DIM, wbl), (0, 4, 1, 2, 3))
        ys = _channel_mix(m["og"].reshape(bs, -1), o_nsa.reshape(bs, -1), m["os"].reshape(bs, -1), ys, lp)
        outs_s.append((m["rows"].reshape(bs, ts, 4, NSA_KV_HEADS, HEAD_DIM),
                       new_win, m["new_conv"], m["s_gdn"], m["s_ssm"]))
    stack = lambda outs, i: jnp.stack([o[i] for o in outs])
    return (yp.reshape(bp, tp, D_MODEL), ys.reshape(bs, ts, D_MODEL),
            stack(outs_p, 0), stack(outs_p, 1), stack(outs_p, 2), stack(outs_p, 3), stack(outs_p, 4),
            stack(outs_s, 0), stack(outs_s, 1), stack(outs_s, 2), stack(outs_s, 3), stack(outs_s, 4))
```

```python
import functools
import math

import jax
import jax.numpy as jnp
from jax import lax
from jax.experimental import pallas as pl
from jax.experimental.pallas import tpu as pltpu

F32 = jnp.float32
BF16 = jnp.bfloat16
HI = lax.Precision.HIGHEST

D_MODEL = 1024
HEAD_DIM = 64
GDN_HEADS = 4
GDN_DK = 64
GDN_DV = 64
CHUNK = 64
NSA_HEADS = 8
NSA_KV_HEADS = 2
NSA_GROUP = 4
NSA_BLOCK = 64
NSA_TOPK = 16
NSA_WINDOW = 512
SSM_HEADS = 4
SSM_HEADDIM = 64
SSM_GROUPS = 2
SSM_DSTATE = 128
CONV_W = 4
PEER_KEYS = 128
PEER_HEADS = 8
PEER_TOPK = 16
PEER_DKEY = 256
ROPE_THETA = 10000.0
EPS = 1e-6

GDN_QK = GDN_HEADS * GDN_DK
GDN_V = GDN_HEADS * GDN_DV
SSM_DI = SSM_HEADS * SSM_HEADDIM
SSM_BC = SSM_GROUPS * SSM_DSTATE
GDN_CONV_CH = 2 * GDN_QK + GDN_V
CONV_CH = GDN_CONV_CH + SSM_DI + 2 * SSM_BC
NSA_Q = NSA_HEADS * HEAD_DIM
NSA_KV = NSA_KV_HEADS * HEAD_DIM
SMALL_W = 128
LANE = 128
VMEM_LIMIT = 48 * 1024 * 1024


def _cparams(*sem):
    return pltpu.CompilerParams(dimension_semantics=sem, vmem_limit_bytes=VMEM_LIMIT)


def _dot(a, b):
    return jnp.dot(a, b, preferred_element_type=F32)


def _split_bf16(a):
    hi = a.astype(BF16)
    return hi, (a - hi.astype(F32)).astype(BF16)


def _dot3_general(a, b, dims):
    a_hi, a_lo = _split_bf16(a)
    b_hi, b_lo = _split_bf16(b)
    dg = lambda x, y: lax.dot_general(x, y, (dims, ((), ())), preferred_element_type=F32)
    return dg(a_hi, b_hi) + dg(a_lo, b_hi) + dg(a_hi, b_lo)


def _dot_hi(a, b):
    return _dot3_general(a, b, ((1,), (0,)))


def _split3(a):
    hi = a.astype(BF16)
    r = a - hi.astype(F32)
    mid = r.astype(BF16)
    return hi, mid, (r - mid.astype(F32)).astype(BF16)


def _dot_sel(sel, x, dims=((1,), (0,)), sel_right=False):
    out = None
    for piece in _split3(x):
        ops = (piece, sel.astype(BF16)) if sel_right else (sel.astype(BF16), piece)
        t = lax.dot_general(*ops, (dims, ((), ())), preferred_element_type=F32)
        out = t if out is None else out + t
    return out


def _dot_nt(a, b, precision=None):
    if precision is not None:
        return _dot3_general(a, b, ((1,), (1,)))
    return lax.dot_general(a, b, (((1,), (1,)), ((), ())), preferred_element_type=F32)


def _dot_tn(a, b, precision=None):
    if precision is not None:
        return _dot3_general(a, b, ((0,), (0,)))
    return lax.dot_general(a, b, (((0,), (0,)), ((), ())), preferred_element_type=F32)


def _sigmoid(x):
    return 1.0 / (1.0 + jnp.exp(-x))


def _softplus(x):
    return jnp.maximum(x, 0.0) + jnp.log1p(jnp.exp(-jnp.abs(x)))


def _iota(shape, dim):
    return lax.broadcasted_iota(jnp.int32, shape, dim)


IN_GROUPS = (("conv", CONV_CH), ("z", GDN_V + SSM_DI), ("q", NSA_Q), ("kv", 6 * NSA_KV), ("small", SMALL_W))
IN_CAT = sum(w for _, w in IN_GROUPS)


def _inproj_kernel(x_ref, g_ref, w_ref, conv_ref, z_ref, q_ref, kv_ref, small_ref):
    x = x_ref[...]
    ms = jnp.mean(x * x, axis=-1, keepdims=True)
    xn = (x * lax.rsqrt(ms + EPS) * g_ref[...]).astype(BF16)
    off = 0
    for ref, (_, w) in zip((conv_ref, z_ref, q_ref, kv_ref, small_ref), IN_GROUPS):
        ref[...] = _dot(xn, w_ref[:, off:off + w])
        off += w


def _inproj(x2d, norm_g, w_cat):
    n = x2d.shape[0]
    tm = min(n, 512)
    return pl.pallas_call(
        _inproj_kernel,
        grid=(n // tm,),
        in_specs=[pl.BlockSpec((tm, D_MODEL), lambda i: (i, 0)),
                  pl.BlockSpec((1, D_MODEL), lambda i: (0, 0)),
                  pl.BlockSpec((D_MODEL, IN_CAT), lambda i: (0, 0))],
        out_specs=[pl.BlockSpec((tm, w), lambda i: (i, 0)) for _, w in IN_GROUPS],
        out_shape=[jax.ShapeDtypeStruct((n, w), F32) for _, w in IN_GROUPS],
        compiler_params=_cparams("parallel"),
        name="inproj",
    )(x2d, norm_g.reshape(1, D_MODEL), w_cat)


def _cat_w_in(w_in):
    o = 0
    parts = {}
    for name, w in (("conv", CONV_CH), ("gdn_z", GDN_V), ("gdn_b", GDN_HEADS), ("gdn_a", GDN_HEADS),
                    ("ssm_z", SSM_DI), ("ssm_dt", SSM_HEADS), ("nsa_q", NSA_Q), ("nsa_kv", 6 * NSA_KV),
                    ("nsa_g", 3 * NSA_HEADS)):
        parts[name] = w_in[:, o:o + w]
        o += w
    small = jnp.concatenate([parts["gdn_b"], parts["gdn_a"], parts["ssm_dt"], parts["nsa_g"]], axis=1)
    small = jnp.pad(small, ((0, 0), (0, SMALL_W - small.shape[1])))
    return jnp.concatenate([parts["conv"], parts["gdn_z"], parts["ssm_z"], parts["nsa_q"], parts["nsa_kv"], small],
                           axis=1).astype(BF16)


HALO = 8


def _conv_kernel(x_ref, hist_ref, w_ref, b_ref, c_ref, nc_ref, xe, *, tm, nt):
    t = pl.program_id(1)

    @pl.when(t == 0)
    def _():
        xe[HALO - 3:HALO, :] = hist_ref[0]

    xe[HALO:HALO + tm, :] = x_ref[0]
    y = b_ref[...] + xe[HALO - 3:HALO - 3 + tm, :] * w_ref[0:1, :]
    for j in range(1, CONV_W):
        y = y + xe[HALO - 3 + j:HALO - 3 + j + tm, :] * w_ref[j:j + 1, :]
    c_ref[0] = y * _sigmoid(y)
    last = xe[HALO + tm - 3:HALO + tm, :]

    @pl.when(t == nt - 1)
    def _():
        nc_ref[0] = last

    xe[HALO - 3:HALO, :] = last


def _conv(x, hist, w, b):
    bsz, t, ch = x.shape
    tm = min(t, 512)
    nt = t // tm
    return pl.pallas_call(
        functools.partial(_conv_kernel, tm=tm, nt=nt),
        grid=(bsz, nt),
        in_specs=[pl.BlockSpec((1, tm, ch), lambda i, j: (i, j, 0)),
                  pl.BlockSpec((1, CONV_W - 1, ch), lambda i, j: (i, 0, 0)),
                  pl.BlockSpec((CONV_W, ch), lambda i, j: (0, 0)),
                  pl.BlockSpec((1, ch), lambda i, j: (0, 0))],
        out_specs=[pl.BlockSpec((1, tm, ch), lambda i, j: (i, j, 0)),
                   pl.BlockSpec((1, CONV_W - 1, ch), lambda i, j: (i, 0, 0))],
        out_shape=[jax.ShapeDtypeStruct((bsz, t, ch), F32),
                   jax.ShapeDtypeStruct((bsz, CONV_W - 1, ch), F32)],
        scratch_shapes=[pltpu.VMEM((HALO + tm, ch), F32)],
        compiler_params=_cparams("parallel", "arbitrary"),
        name="conv",
    )(x, hist, w, b.reshape(1, ch))


def _tri(n, strict=False):
    r, c = _iota((n, n), 0), _iota((n, n), 1)
    return (r > c) if strict else (r >= c)


def _row_of(col):
    n = col.shape[0]
    eye = (_iota((n, n), 0) == _iota((n, n), 1)).astype(F32)
    return _dot_sel(jnp.ones((n, n), F32), eye * col)


def _gdn_kernel(q_ref, k_ref, v_ref, z_ref, small_ref, s0_ref, alog_ref, dtb_ref, ng_ref, o_ref, s_ref, st,
                *, nc, t_valid, bb):
    c = pl.program_id(1)

    @pl.when(c == 0)
    def _():
        st[...] = s0_ref[...]

    incl = _tri(CHUNK)
    strict = _tri(CHUNK, strict=True)
    if t_valid is not None:
        valid = (c * CHUNK + _iota((CHUNK, 1), 0)) < t_valid
    chains = [(b, h) for b in range(bb) for h in range(GDN_HEADS)]
    beta_all, gc_all = {}, {}
    for b in range(bb):
        sm = small_ref[b]
        bt = _sigmoid(sm[:, 0:GDN_HEADS])
        g_all = -jnp.exp(alog_ref[...]) * _softplus(sm[:, GDN_HEADS:2 * GDN_HEADS] + dtb_ref[...])
        if t_valid is not None:
            bt = jnp.where(valid, bt, 0.0)
            g_all = jnp.where(valid, g_all, 0.0)
        beta_all[b] = bt
        gc_all[b] = _dot_sel(incl.astype(F32), g_all)
    q, k, v, beta, gc = {}, {}, {}, {}, {}
    for c_ in chains:
        b, h = c_
        sl = slice(h * GDN_DK, (h + 1) * GDN_DK)
        qq = q_ref[b][:, sl]
        kk = k_ref[b][:, sl]
        vv = v_ref[b][:, sl]
        qq = qq * lax.rsqrt(jnp.sum(qq * qq, axis=-1, keepdims=True) + EPS) * (GDN_DK ** -0.5)
        kk = kk * lax.rsqrt(jnp.sum(kk * kk, axis=-1, keepdims=True) + EPS)
        if t_valid is not None:
            qq = jnp.where(valid, qq, 0.0)
            kk = jnp.where(valid, kk, 0.0)
            vv = jnp.where(valid, vv, 0.0)
        q[c_], k[c_], v[c_] = qq, kk, vv
        beta[c_] = beta_all[b][:, h:h + 1]
        gc[c_] = gc_all[b][:, h:h + 1]
    diff = {c_: gc[c_] - _row_of(gc[c_]) for c_ in chains}
    kkt = {c_: _dot_nt(k[c_], k[c_], HI) for c_ in chains}
    qkt = {c_: _dot_nt(q[c_], k[c_], HI) for c_ in chains}
    egc = {c_: jnp.exp(gc[c_]) for c_ in chains}
    p = {c_: -(beta[c_] * kkt[c_] * jnp.where(strict, jnp.exp(jnp.where(strict, diff[c_], 0.0)), 0.0))
         for c_ in chains}
    x = {c_: jnp.concatenate([v[c_] * beta[c_], k[c_] * (beta[c_] * egc[c_])], axis=-1) for c_ in chains}
    for it in range(6):
        px = {c_: _dot_hi(p[c_], x[c_]) for c_ in chains}
        if it < 5:
            p = {c_: _dot_hi(p[c_], p[c_]) for c_ in chains}
        x = {c_: x[c_] + px[c_] for c_ in chains}
    aqk = {c_: qkt[c_] * jnp.where(incl, jnp.exp(jnp.where(incl, diff[c_], 0.0)), 0.0) for c_ in chains}
    s = {c_: st[c_[0], c_[1]] for c_ in chains}
    wks = {c_: _dot_hi(x[c_][:, GDN_DV:], s[c_]) for c_ in chains}
    qs = {c_: _dot_hi(q[c_] * egc[c_], s[c_]) for c_ in chains}
    w = {c_: x[c_][:, :GDN_DV] - wks[c_] for c_ in chains}
    gl = {c_: gc[c_][CHUNK - 1:CHUNK, :] for c_ in chains}
    kdw = {c_: _dot_tn(k[c_] * jnp.exp(gl[c_] - gc[c_]), w[c_], HI) for c_ in chains}
    aw = {c_: _dot_hi(aqk[c_], w[c_]) for c_ in chains}
    for c_ in chains:
        st[c_[0], c_[1]] = jnp.exp(gl[c_]) * s[c_] + kdw[c_]
    for b in range(bb):
        outs = []
        for h in range(GDN_HEADS):
            o = qs[(b, h)] + aw[(b, h)]
            o = o * lax.rsqrt(jnp.mean(o * o, axis=-1, keepdims=True) + EPS) * ng_ref[...]
            zh = z_ref[b][:, h * GDN_DK:(h + 1) * GDN_DK]
            outs.append(o * (zh * _sigmoid(zh)))
        o_ref[b] = jnp.concatenate(outs, axis=-1)

    @pl.when(c == nc - 1)
    def _():
        s_ref[...] = st[...]


GDN_BB = 2


def _gdn(c, z, small, s0, a_log, dt_bias, norm_g, t_valid=None):
    bsz, t, _ = c.shape
    nc = t // CHUNK
    bb = GDN_BB
    return pl.pallas_call(
        functools.partial(_gdn_kernel, nc=nc, t_valid=t_valid, bb=bb),
        grid=(bsz // bb, nc),
        in_specs=[pl.BlockSpec((bb, CHUNK, GDN_QK), lambda i, j: (i, j, 0)),
                  pl.BlockSpec((bb, CHUNK, GDN_QK), lambda i, j: (i, j, 1)),
                  pl.BlockSpec((bb, CHUNK, GDN_V), lambda i, j: (i, j, 2)),
                  pl.BlockSpec((bb, CHUNK, GDN_V), lambda i, j: (i, j, 0)),
                  pl.BlockSpec((bb, CHUNK, SMALL_W), lambda i, j: (i, j, 0)),
                  pl.BlockSpec((bb, GDN_HEADS, GDN_DK, GDN_DV), lambda i, j: (i, 0, 0, 0)),
                  pl.BlockSpec((1, GDN_HEADS), lambda i, j: (0, 0)),
                  pl.BlockSpec((1, GDN_HEADS), lambda i, j: (0, 0)),
                  pl.BlockSpec((1, GDN_DV), lambda i, j: (0, 0))],
        out_specs=[pl.BlockSpec((bb, CHUNK, GDN_V), lambda i, j: (i, j, 0)),
                   pl.BlockSpec((bb, GDN_HEADS, GDN_DK, GDN_DV), lambda i, j: (i, 0, 0, 0))],
        out_shape=[jax.ShapeDtypeStruct((bsz, t, GDN_V), F32),
                   jax.ShapeDtypeStruct((bsz, GDN_HEADS, GDN_DK, GDN_DV), F32)],
        scratch_shapes=[pltpu.VMEM((bb, GDN_HEADS, GDN_DK, GDN_DV), F32)],
        compiler_params=_cparams("parallel", "arbitrary"),
        name="gdn",
    )(c, c, c, z, small, s0, a_log.reshape(1, -1), dt_bias.reshape(1, -1), norm_g.reshape(1, -1))


def _ssd_kernel(x_ref, b_ref, c_ref, z_ref, small_ref, h0_ref, alog_ref, dtb_ref, dsk_ref, ng_ref, o_ref, h_ref, st,
                *, nc, t_valid):
    ci = pl.program_id(1)

    @pl.when(ci == 0)
    def _():
        st[...] = h0_ref[0]

    sm = small_ref[0]
    dt_all = _softplus(sm[:, 2 * GDN_HEADS:2 * GDN_HEADS + SSM_HEADS] + dtb_ref[...])
    xs = x_ref[0]
    bm_all = b_ref[0]
    cm_all = c_ref[0]
    if t_valid is not None:
        valid = (ci * CHUNK + _iota((CHUNK, 1), 0)) < t_valid
        dt_all = jnp.where(valid, dt_all, 0.0)
        xs = jnp.where(valid, xs, 0.0)
        bm_all = jnp.where(valid, bm_all, 0.0)
        cm_all = jnp.where(valid, cm_all, 0.0)
    incl = _tri(CHUNK)
    cs_all = _dot_sel(incl.astype(F32), dt_all * (-jnp.exp(alog_ref[...])))
    rep = SSM_HEADS // SSM_GROUPS
    heads = range(SSM_HEADS)
    x = [xs[:, h * SSM_HEADDIM:(h + 1) * SSM_HEADDIM] for h in heads]
    bm = [bm_all[:, (h // rep) * SSM_DSTATE:(h // rep + 1) * SSM_DSTATE] for h in heads]
    cm = [cm_all[:, (h // rep) * SSM_DSTATE:(h // rep + 1) * SSM_DSTATE] for h in heads]
    cs = [cs_all[:, h:h + 1] for h in heads]
    rows = [_row_of(cs[h]) for h in heads]
    cbr = [_dot_nt(cm[g * rep], bm[g * rep], HI) for g in range(SSM_GROUPS)]
    xdt = [x[h] * dt_all[:, h:h + 1] for h in heads]
    cl = [cs[h][CHUNK - 1:CHUNK, :] for h in heads]
    hs = [st[h] for h in heads]
    cb = [cbr[h // rep] * jnp.where(incl, jnp.exp(jnp.where(incl, cs[h] - rows[h], 0.0)), 0.0) for h in heads]
    y_diag = [_dot_hi(cb[h], xdt[h]) for h in heads]
    y_off = [_dot_nt(cm[h] * jnp.exp(cs[h]), hs[h], HI) for h in heads]
    upd = [_dot_tn(xdt[h] * jnp.exp(cl[h] - cs[h]), bm[h], HI) for h in heads]
    for h in heads:
        st[h] = jnp.exp(cl[h]) * hs[h] + upd[h]
    y = jnp.concatenate([y_diag[h] + y_off[h] + dsk_ref[0:1, h:h + 1] * x[h] for h in heads], axis=-1)
    zz = z_ref[0]
    y = y * (zz * _sigmoid(zz))
    gs = SSM_DI // SSM_GROUPS
    outs = []
    for g in range(SSM_GROUPS):
        yg = y[:, g * gs:(g + 1) * gs]
        outs.append(yg * lax.rsqrt(jnp.mean(yg * yg, axis=-1, keepdims=True) + EPS) * ng_ref[0:1, g * gs:(g + 1) * gs])
    o_ref[0] = jnp.concatenate(outs, axis=-1)

    @pl.when(ci == nc - 1)
    def _():
        h_ref[0] = st[...]


def _ssd(c, z, small, h0, a_log, dt_bias, d_skip, norm_g, t_valid=None):
    bsz, t, _ = c.shape
    nc = t // CHUNK
    w = SSM_DI
    return pl.pallas_call(
        functools.partial(_ssd_kernel, nc=nc, t_valid=t_valid),
        grid=(bsz, nc),
        in_specs=[pl.BlockSpec((1, CHUNK, w), lambda i, j: (i, j, 3)),
                  pl.BlockSpec((1, CHUNK, w), lambda i, j: (i, j, 4)),
                  pl.BlockSpec((1, CHUNK, w), lambda i, j: (i, j, 5)),
                  pl.BlockSpec((1, CHUNK, w), lambda i, j: (i, j, 1)),
                  pl.BlockSpec((1, CHUNK, SMALL_W), lambda i, j: (i, j, 0)),
                  pl.BlockSpec((1, SSM_HEADS, SSM_HEADDIM, SSM_DSTATE), lambda i, j: (i, 0, 0, 0)),
                  pl.BlockSpec((1, SSM_HEADS), lambda i, j: (0, 0)),
                  pl.BlockSpec((1, SSM_HEADS), lambda i, j: (0, 0)),
                  pl.BlockSpec((1, SSM_HEADS), lambda i, j: (0, 0)),
                  pl.BlockSpec((1, SSM_DI), lambda i, j: (0, 0))],
        out_specs=[pl.BlockSpec((1, CHUNK, SSM_DI), lambda i, j: (i, j, 0)),
                   pl.BlockSpec((1, SSM_HEADS, SSM_HEADDIM, SSM_DSTATE), lambda i, j: (i, 0, 0, 0))],
        out_shape=[jax.ShapeDtypeStruct((bsz, t, SSM_DI), F32),
                   jax.ShapeDtypeStruct((bsz, SSM_HEADS, SSM_HEADDIM, SSM_DSTATE), F32)],
        scratch_shapes=[pltpu.VMEM((SSM_HEADS, SSM_HEADDIM, SSM_DSTATE), F32)],
        compiler_params=_cparams("parallel", "arbitrary"),
        name="ssd",
    )(c, c, c, z, small, h0, a_log.reshape(1, -1), dt_bias.reshape(1, -1), d_skip.reshape(1, -1),
      norm_g.reshape(1, -1))


PAGE = 128
NEG = -1e30


def _rope_tables(pos):
    half = HEAD_DIM // 2
    inv = ROPE_THETA ** (-jnp.arange(half, dtype=F32) * 2.0 / HEAD_DIM)
    ang = pos.astype(F32)[:, None] * inv[None, :]
    cos, sin = jnp.cos(ang), jnp.sin(ang)
    return jnp.tile(cos, (1, 4)), jnp.tile(jnp.concatenate([-sin, sin], axis=1), (1, 2))


def _nsaproj_kernel(q_ref, kv_ref, small_ref, cos_ref, sin_ref, qn_ref, bd_ref,
                    qo_ref, rows_ref, win_ref, gates_ref, ks_ref, vs_ref, kw_ref, vw_ref):
    cosf = cos_ref[...]
    sins = sin_ref[...]
    bd = bd_ref[...]
    first_half = (_iota((1, LANE), 1) % HEAD_DIM) < (HEAD_DIM // 2)

    def normrope(x, g):
        y = x * lax.rsqrt(_dot_hi(x * x, bd) + EPS) * g
        rot = jnp.where(first_half, pltpu.roll(y, LANE - HEAD_DIM // 2, 1), pltpu.roll(y, HEAD_DIM // 2, 1))
        return y * cosf + rot * sins

    qr = q_ref[0]
    qo_ref[0] = jnp.concatenate(
        [normrope(qr[:, i * LANE:(i + 1) * LANE], qn_ref[0:1, :]) * (HEAD_DIM ** -0.5) for i in range(NSA_Q // LANE)],
        axis=-1)
    kv = kv_ref[0]
    kc = normrope(kv[:, 0:LANE], qn_ref[1:2, :])
    vc = kv[:, LANE:2 * LANE]
    ks = normrope(kv[:, 2 * LANE:3 * LANE], qn_ref[2:3, :])
    vs = kv[:, 3 * LANE:4 * LANE]
    kw = normrope(kv[:, 4 * LANE:5 * LANE], qn_ref[3:4, :])
    vw = kv[:, 5 * LANE:6 * LANE]
    rows_ref[0] = jnp.concatenate([kc, vc, ks, vs], axis=-1)
    win_ref[0] = jnp.concatenate([kw, vw], axis=-1)
    gates_ref[0] = _sigmoid(small_ref[0])
    ones = jnp.ones((ks.shape[0], LANE - HEAD_DIM), BF16)
    for h in range(NSA_KV_HEADS):
        sl = slice(h * HEAD_DIM, (h + 1) * HEAD_DIM)
        ks_ref[0, h] = ks[:, sl].astype(BF16)
        vs_ref[0, h] = jnp.concatenate([vs[:, sl].astype(BF16), ones], axis=-1)
        kw_ref[0, h] = kw[:, sl].astype(BF16)
        vw_ref[0, h] = jnp.concatenate([vw[:, sl].astype(BF16), ones], axis=-1)


def _nsaproj(q_raw, kv_raw, small, pos, qk_norm):
    bsz, t, _ = q_raw.shape
    tm = min(t, 512)
    cos, sin = _rope_tables(pos)
    qn = jnp.tile(qk_norm, (1, 2))
    bd = jnp.kron(jnp.eye(2, dtype=F32), jnp.full((HEAD_DIM, HEAD_DIM), 1.0 / HEAD_DIM, F32))
    tok = lambda w: pl.BlockSpec((1, tm, w), lambda i, j: (i, j, 0))
    return pl.pallas_call(
        _nsaproj_kernel,
        grid=(bsz, t // tm),
        in_specs=[tok(NSA_Q), tok(6 * NSA_KV), tok(SMALL_W),
                  pl.BlockSpec((tm, LANE), lambda i, j: (j, 0)),
                  pl.BlockSpec((tm, LANE), lambda i, j: (j, 0)),
                  pl.BlockSpec((4, LANE), lambda i, j: (0, 0)),
                  pl.BlockSpec((LANE, LANE), lambda i, j: (0, 0))],
        out_specs=[tok(NSA_Q), tok(4 * NSA_KV), tok(2 * NSA_KV), tok(SMALL_W)]
                  + [pl.BlockSpec((1, NSA_KV_HEADS, tm, w), lambda i, j: (i, 0, j, 0))
                     for w in (HEAD_DIM, LANE, HEAD_DIM, LANE)],
        out_shape=[jax.ShapeDtypeStruct((bsz, t, NSA_Q), F32),
                   jax.ShapeDtypeStruct((bsz, t, 4 * NSA_KV), F32),
                   jax.ShapeDtypeStruct((bsz, t, 2 * NSA_KV), F32),
                   jax.ShapeDtypeStruct((bsz, t, SMALL_W), F32)]
                  + [jax.ShapeDtypeStruct((bsz, NSA_KV_HEADS, t, w), BF16) for w in (HEAD_DIM, LANE, HEAD_DIM, LANE)],
        compiler_params=_cparams("parallel", "parallel"),
        name="nsaproj",
    )(q_raw, kv_raw, small, cos, sin, qn, bd)


CMP_PAGES = 16


def _compress_kernel(pt_ref, *refs):
    del pt_ref
    page_refs = refs[:CMP_PAGES]
    pe_ref, phi_ref, ck_ref, cv_ref, bufk, bufv = refs[CMP_PAGES:]
    for i, r in enumerate(page_refs):
        bufk[i * PAGE:(i + 1) * PAGE, :] = r[0, :, 0:LANE]
        bufv[i * PAGE:(i + 1) * PAGE, :] = r[0, :, LANE:2 * LANE]
    nblk = CMP_PAGES * PAGE // NSA_BLOCK

    def body(l, carry):
        ak, av = carry
        kl = bufk[pl.ds(l, nblk, stride=NSA_BLOCK), :] + pe_ref[0, pl.ds(l, 1), :]
        vl = bufv[pl.ds(l, nblk, stride=NSA_BLOCK), :] + pe_ref[1, pl.ds(l, 1), :]
        return ak + _dot_hi(kl, phi_ref[0, l]), av + _dot_hi(vl, phi_ref[1, l])

    zero = jnp.zeros((nblk, LANE), F32)
    ak, av = lax.fori_loop(0, NSA_BLOCK, body, (zero, zero))
    ck_ref[0] = ak
    cv_ref[0] = av


def _compress(pages, page_table, pe, phi):
    bsz, npages = page_table.shape
    steps = npages // CMP_PAGES
    nblk = CMP_PAGES * PAGE // NSA_BLOCK
    pe2 = jnp.tile(pe, (1, 1, 2))
    eye2 = jnp.eye(2, dtype=F32)
    phi_bd = jnp.einsum('ab,klde->kladbe', eye2, phi).reshape(2, NSA_BLOCK, LANE, LANE)
    page_specs = [pl.BlockSpec((1, PAGE, 2 * LANE), lambda b, j, pt, i=i: (pt[b, j * CMP_PAGES + i], 0, 0))
                  for i in range(CMP_PAGES)]
    return pl.pallas_call(
        _compress_kernel,
        grid_spec=pltpu.PrefetchScalarGridSpec(
            num_scalar_prefetch=1,
            grid=(bsz, steps),
            in_specs=page_specs + [pl.BlockSpec((2, NSA_BLOCK, LANE), lambda b, j, pt: (0, 0, 0)),
                                   pl.BlockSpec((2, NSA_BLOCK, LANE, LANE), lambda b, j, pt: (0, 0, 0, 0))],
            out_specs=[pl.BlockSpec((1, nblk, LANE), lambda b, j, pt: (b, j, 0)),
                       pl.BlockSpec((1, nblk, LANE), lambda b, j, pt: (b, j, 0))],
            scratch_shapes=[pltpu.VMEM((CMP_PAGES * PAGE, LANE), F32)] * 2),
        out_shape=[jax.ShapeDtypeStruct((bsz, steps * nblk, LANE), F32)] * 2,
        compiler_params=_cparams("parallel", "parallel"),
        name="compress",
    )(page_table, *([pages] * CMP_PAGES), pe2, phi_bd)


QBLK = 128
KTILE = 1024


def _colmax(x):
    return jnp.max(x, axis=0, keepdims=True)


def _colsum(x):
    return jnp.sum(x, axis=0, keepdims=True)


def _tile4(x):
    return jnp.concatenate([x] * NSA_GROUP, axis=1)


def _select_blocks(imp, q0, nb):
    nio = _iota(imp.shape, 0)
    nio_f = nio.astype(F32)
    cur = (q0 + _iota((1, imp.shape[1]), 1)) // NSA_BLOCK
    forced = (nio == 0) | (nio == cur) | (nio == cur - 1)
    causal = nio <= cur
    work = jnp.where(forced, jnp.inf, jnp.where(causal, imp, -jnp.inf))
    sel = jnp.zeros(imp.shape, F32)
    for _ in range(min(NSA_TOPK, nb)):
        m = _colmax(work)
        idx = jnp.min(jnp.where(work == m, nio_f, float(nb)), axis=0, keepdims=True)
        pick = nio_f == idx
        sel = jnp.where(pick, 1.0, sel)
        work = jnp.where(pick, -jnp.inf, work)
    return jnp.where(causal, sel, 0.0)


def _nsa_prompt_kernel(q_ref, ck_ref, cv_ref, ks_ref, vs_ref, kw_ref, vw_ref, gt_ref, wb_ref, o_ref, sel_sc, *,
                       t_total):
    nb = t_total // NSA_BLOCK
    kt = min(KTILE, t_total)
    wk = min(NSA_WINDOW + QBLK, t_total)
    bpt = kt // NSA_BLOCK
    qi = pl.program_id(1)
    q0 = qi * QBLK
    qall = q_ref[0]
    gt = gt_ref[0]
    qpos = _tile4(q0 + _iota((1, QBLK), 1))
    eye = (_iota((HEAD_DIM, HEAD_DIM), 0) == _iota((HEAD_DIM, HEAD_DIM), 1)).astype(BF16)
    heads = range(NSA_KV_HEADS)
    qsb, oc = [], []
    for h in heads:
        qs = jnp.concatenate([qall[:, (h * NSA_GROUP + g) * HEAD_DIM:(h * NSA_GROUP + g + 1) * HEAD_DIM]
                              for g in range(NSA_GROUP)], axis=0)
        qsb.append(qs.astype(BF16))
        sl = slice(h * HEAD_DIM, (h + 1) * HEAD_DIM)
        s = _dot_nt(ck_ref[0][:, sl], qs, HI)
        mask = ((_iota((nb, 1), 0) + 1) * NSA_BLOCK - 1) <= qpos
        m = _colmax(jnp.where(mask, s, NEG))
        p = jnp.where(mask, jnp.exp(s - m), 0.0)
        p = p / jnp.maximum(_colsum(p), 1e-30)
        oc.append(_dot_tn(cv_ref[0][:, sl], p, HI))
        imp = p[:, 0:QBLK]
        for g in range(1, NSA_GROUP):
            imp = imp + p[:, g * QBLK:(g + 1) * QBLK]
        sel_sc[h] = (_select_blocks(imp, q0, nb) - 1.0) * (-NEG)

    def body(i, carry, diagonal):
        k0 = pl.multiple_of(i * kt, kt)
        sc = []
        for h in heads:
            brows = sel_sc[h, pl.ds(pl.multiple_of(i * bpt, bpt), bpt), :]
            bias = jnp.concatenate([jnp.broadcast_to(brows[j:j + 1, :], (NSA_BLOCK, QBLK)) for j in range(bpt)],
                                   axis=0)
            s_h = _dot_nt(ks_ref[0, h, pl.ds(k0, kt), :], qsb[h]) + _tile4(bias)
            if diagonal:
                s_h = jnp.where((k0 + _iota((kt, 1), 0)) <= qpos, s_h, NEG)
            sc.append(s_h)
        m_n = [jnp.maximum(carry[h][0], _colmax(sc[h])) for h in heads]
        pp = [jnp.exp((sc[h] - m_n[h]).astype(BF16)) for h in heads]
        pv = [_dot_tn(vs_ref[0, h, pl.ds(k0, kt), :], pp[h]) for h in heads]
        return tuple((m_n[h], carry[h][1] * jnp.exp(carry[h][0] - m_n[h]) + pv[h]) for h in heads)

    ntile = (q0 + QBLK + kt - 1) // kt
    init = (jnp.full((1, NSA_GROUP * QBLK), NEG, F32), jnp.zeros((LANE, NSA_GROUP * QBLK), F32))
    carry = lax.fori_loop(0, ntile - 1, functools.partial(body, diagonal=False), (init,) * NSA_KV_HEADS)
    carry = body(ntile - 1, carry, True)
    w0 = pl.multiple_of(jnp.maximum(q0 + QBLK - wk, 0), QBLK)
    wbias = _tile4(wb_ref[pl.ds(pl.multiple_of(wk - QBLK - (q0 - w0), QBLK), wk), :])
    outs = []
    for h in heads:
        acc_s = carry[h][1]
        osel = acc_s[0:HEAD_DIM] / jnp.maximum(acc_s[HEAD_DIM:HEAD_DIM + 1], 1e-30)
        sw = _dot_nt(kw_ref[0, h, pl.ds(w0, wk), :], qsb[h]) + wbias
        pw = jnp.exp((sw - _colmax(sw)).astype(BF16))
        acc_w = _dot_tn(vw_ref[0, h, pl.ds(w0, wk), :], pw)
        ow = acc_w[0:HEAD_DIM] / jnp.maximum(acc_w[HEAD_DIM:HEAD_DIM + 1], 1e-30)
        gate = [jnp.concatenate([gt[(h * NSA_GROUP + g) * 3 + j:(h * NSA_GROUP + g) * 3 + j + 1, :]
                                 for g in range(NSA_GROUP)], axis=1) for j in range(3)]
        ot = gate[0] * oc[h] + gate[1] * osel + gate[2] * ow
        for g in range(NSA_GROUP):
            outs.append(_dot_tn(ot[:, g * QBLK:(g + 1) * QBLK].astype(BF16), eye))
    o_ref[0] = jnp.concatenate(outs, axis=-1)


def _nsa_prompt(q, ck, cv, ks, vs, kw, vw, gates_t):
    bsz, t, _ = q.shape
    nb = t // NSA_BLOCK
    full = lambda w: pl.BlockSpec((1, NSA_KV_HEADS, t, w), lambda i, j: (i, 0, 0, 0))
    wk = min(NSA_WINDOW + QBLK, t)
    omax = wk - QBLK
    delta = omax + _iota((wk + omax, QBLK), 1) - _iota((wk + omax, QBLK), 0)
    wbias = jnp.where((delta >= 0) & (delta < NSA_WINDOW), 0.0, NEG).astype(F32)
    return pl.pallas_call(
        functools.partial(_nsa_prompt_kernel, t_total=t),
        grid=(bsz, t // QBLK),
        in_specs=[pl.BlockSpec((1, QBLK, NSA_Q), lambda i, j: (i, j, 0)),
                  pl.BlockSpec((1, nb, LANE), lambda i, j: (i, 0, 0)),
                  pl.BlockSpec((1, nb, LANE), lambda i, j: (i, 0, 0)),
                  full(HEAD_DIM), full(LANE), full(HEAD_DIM), full(LANE),
                  pl.BlockSpec((1, 3 * NSA_HEADS, QBLK), lambda i, j: (i, 0, j)),
                  pl.BlockSpec((wk + omax, QBLK), lambda i, j: (0, 0))],
        out_specs=pl.BlockSpec((1, QBLK, NSA_Q), lambda i, j: (i, j, 0)),
        out_shape=jax.ShapeDtypeStruct((bsz, t, NSA_Q), F32),
        scratch_shapes=[pltpu.VMEM((NSA_KV_HEADS, nb, QBLK), F32)],
        compiler_params=_cparams("parallel", "arbitrary"),
        name="nsa_prompt",
    )(q, ck, cv, ks, vs, kw, vw, gates_t, wbias)


def _group_queries(q, h):
    return jnp.concatenate([q[:, (h * NSA_GROUP + g) * HEAD_DIM:(h * NSA_GROUP + g + 1) * HEAD_DIM]
                            for g in range(NSA_GROUP)] + [jnp.zeros((8 - NSA_GROUP, HEAD_DIM), F32)], axis=0)


def _dot_x3(a, w_hi, w_lo):
    a_hi, a_lo = _split_bf16(a)
    return _dot(a_hi, w_hi) + _dot(a_lo, w_hi) + _dot(a_hi, w_lo)


def _compress_t_kernel(pt_ref, *refs, npg):
    del pt_ref
    page_refs = refs[:npg]
    pe_ref, whi_ref, wlo_ref, ck_ref, cv_ref, buf = refs[npg:]
    rows = 2 * LANE
    for i, r in enumerate(page_refs):
        buf[i * rows:(i + 1) * rows, :] = r[0]

    def body(d, carry):
        accs = list(carry)
        for kv in range(2):
            a = jnp.concatenate([buf[pl.ds(kv * LANE + h * HEAD_DIM + d, npg, stride=rows), :]
                                 for h in range(NSA_KV_HEADS)], axis=0) + pe_ref[kv, pl.ds(d, 1), :]
            if kv == 0:
                accs[kv] = accs[kv] + _dot_x3(a, whi_ref[kv, d], wlo_ref[kv, d])
            else:
                accs[kv] = accs[kv] + _dot(a.astype(BF16), whi_ref[kv, d])
        return tuple(accs)

    zero = jnp.zeros((NSA_KV_HEADS * npg, LANE), F32)
    ak, av = lax.fori_loop(0, HEAD_DIM, body, (zero, zero))
    ck_ref[0] = ak
    cv_ref[0] = av


def _compress_t(cache_t, page_table, pe, phi):
    bsz, npg = page_table.shape
    eye2 = jnp.eye(2, dtype=F32)
    w = jnp.einsum('ab,klde->kdalbe', eye2, phi).reshape(2, HEAD_DIM, LANE, LANE)
    w_hi = w.astype(BF16)
    w_lo = (w - w_hi.astype(F32)).astype(BF16)
    pe_t = jnp.tile(jnp.transpose(pe, (0, 2, 1)), (1, 1, 2))
    page_specs = [pl.BlockSpec((1, 2 * LANE, PAGE), lambda b, pt, i=i: (pt[b, i], 0, 0)) for i in range(npg)]
    const = lambda shape: pl.BlockSpec(shape, lambda b, pt: (0,) * len(shape))
    out = pl.BlockSpec((1, NSA_KV_HEADS * npg, LANE), lambda b, pt: (b, 0, 0))
    return pl.pallas_call(
        functools.partial(_compress_t_kernel, npg=npg),
        grid_spec=pltpu.PrefetchScalarGridSpec(
            num_scalar_prefetch=1,
            grid=(bsz,),
            in_specs=page_specs + [const((2, HEAD_DIM, LANE)), const((2, HEAD_DIM, LANE, LANE)),
                                   const((2, HEAD_DIM, LANE, LANE))],
            out_specs=[out, out],
            scratch_shapes=[pltpu.VMEM((npg * 2 * LANE, PAGE), F32)]),
        out_shape=[jax.ShapeDtypeStruct((bsz, NSA_KV_HEADS * npg, LANE), F32)] * 2,
        compiler_params=_cparams("parallel"),
        name="compress_t",
    )(page_table, *([cache_t] * npg), pe_t, w_hi, w_lo)


def _nsa_sample_cmp_kernel(q_ref, ck_ref, cv_ref, oc_ref, idx_ref, *, past, npg):
    nbc = 2 * npg
    cur = past // NSA_BLOCK
    q = q_ref[0]
    lane = _iota((1, nbc), 1)
    blk = 2 * (lane % npg) + lane // npg
    mask = ((blk + 1) * NSA_BLOCK - 1) <= past
    ocs, idxs = [], []
    for h in range(NSA_KV_HEADS):
        qh = _group_queries(q, h)
        ckh = ck_ref[0][h * npg:(h + 1) * npg, :]
        cvh = cv_ref[0][h * npg:(h + 1) * npg, :]
        s = jnp.concatenate([_dot_nt(qh, ckh[:, i * HEAD_DIM:(i + 1) * HEAD_DIM], HI) for i in range(2)], axis=1)
        m = jnp.max(jnp.where(mask, s, NEG), axis=1, keepdims=True)
        p = jnp.where(mask, jnp.exp(s - m), 0.0)
        p = p / jnp.maximum(jnp.sum(p, axis=1, keepdims=True), 1e-30)
        oc = _dot_hi(p[:, 0:npg], cvh[:, 0:HEAD_DIM]) + _dot_hi(p[:, npg:nbc], cvh[:, HEAD_DIM:LANE])
        ocs.append(oc[0:NSA_GROUP])
        imp = p[0:1]
        for g in range(1, NSA_GROUP):
            imp = imp + p[g:g + 1]
        forced = (blk == 0) | (blk == cur) | (blk == cur - 1)
        score = jnp.where(forced, jnp.inf, jnp.where(blk <= cur, imp, -jnp.inf))
        avail = jnp.ones((1, nbc), jnp.bool_)
        slot = _iota((1, NSA_TOPK), 1)
        picked = jnp.full((1, NSA_TOPK), cur, jnp.int32)
        for r in range(NSA_TOPK - 1):
            mm = jnp.max(jnp.where(avail, score, -jnp.inf), axis=1, keepdims=True)
            eq = avail & (score == mm)
            idx = jnp.min(jnp.where(eq, blk, nbc), axis=1, keepdims=True)
            avail = avail & (blk != idx)
            picked = jnp.where(slot == r, idx, picked)
        idxs.append(picked)
    oc_ref[0] = jnp.concatenate(ocs, axis=0)
    idx_ref[0] = jnp.concatenate(idxs, axis=1)


def _nsa_sample_cmp(q3, ck, cv, past):
    bsz = q3.shape[0]
    npg = ck.shape[1] // NSA_KV_HEADS
    return pl.pallas_call(
        functools.partial(_nsa_sample_cmp_kernel, past=past, npg=npg),
        grid=(bsz,),
        in_specs=[pl.BlockSpec((1, 1, NSA_Q), lambda b: (b, 0, 0)),
                  pl.BlockSpec((1, NSA_KV_HEADS * npg, LANE), lambda b: (b, 0, 0)),
                  pl.BlockSpec((1, NSA_KV_HEADS * npg, LANE), lambda b: (b, 0, 0))],
        out_specs=[pl.BlockSpec((1, NSA_HEADS, HEAD_DIM), lambda b: (b, 0, 0)),
                   pl.BlockSpec((1, 1, NSA_KV_HEADS * NSA_TOPK), lambda b: (b, 0, 0))],
        out_shape=[jax.ShapeDtypeStruct((bsz, NSA_HEADS, HEAD_DIM), F32),
                   jax.ShapeDtypeStruct((bsz, 1, NSA_KV_HEADS * NSA_TOPK), jnp.int32)],
        compiler_params=_cparams("parallel"),
        name="nsa_sample_cmp",
    )(q3, ck, cv)


def _nsa_sample_attn_kernel(page_ref, half_ref, *refs, past):
    del page_ref
    nsel = NSA_KV_HEADS * NSA_TOPK
    blk_refs = refs[:nsel]
    q_ref, row_ref, nwin_ref, wc_ref, g_ref, oc_ref, o_ref, win_ref = refs[nsel:]
    b = pl.program_id(0)
    wbl = wc_ref.shape[2]
    q = q_ref[0]
    row = row_ref[0]
    nwin = nwin_ref[0]
    gates = g_ref[0]
    oc = oc_ref[0]
    half_of_lane = _iota((1, PAGE), 1) // NSA_BLOCK
    outs = []
    for h in range(NSA_KV_HEADS):
        sl = slice(h * HEAD_DIM, (h + 1) * HEAD_DIM)
        vsl = slice(LANE + h * HEAD_DIM, LANE + (h + 1) * HEAD_DIM)
        qh = _group_queries(q, h)
        ks_new = row[:, 2 * LANE + h * HEAD_DIM:2 * LANE + (h + 1) * HEAD_DIM]
        vs_new = row[:, 3 * LANE + h * HEAD_DIM:3 * LANE + (h + 1) * HEAD_DIM]
        s_new = jnp.sum(qh * ks_new, axis=1, keepdims=True)
        ss, ms = [], []
        for k in range(NSA_TOPK - 1):
            ss.append(_dot_hi(qh, blk_refs[h * NSA_TOPK + k][0, sl, :]))
            ms.append(half_of_lane == half_ref[b, h * NSA_TOPK + k])
        m = s_new
        for s, mk in zip(ss, ms):
            m = jnp.maximum(m, jnp.max(jnp.where(mk, s, NEG), axis=1, keepdims=True))
        p_new = jnp.exp(s_new - m)
        den = p_new
        acc = p_new * vs_new
        for k, (s, mk) in enumerate(zip(ss, ms)):
            p = jnp.where(mk, jnp.exp(s - m), 0.0)
            den = den + jnp.sum(p, axis=1, keepdims=True)
            acc = acc + _dot_nt(p, blk_refs[h * NSA_TOPK + k][0, vsl, :], HI)
        osel = acc / jnp.maximum(den, 1e-30)
        sw = _dot_hi(qh, wc_ref[0, sl, :])
        delta = wbl - _iota((1, wbl), 1)
        mw = (delta >= 0) & (delta < NSA_WINDOW) & ((past - delta) >= 0)
        kw_new = nwin[:, sl]
        vw_new = nwin[:, vsl]
        sw_new = jnp.sum(qh * kw_new, axis=1, keepdims=True)
        m = jnp.maximum(jnp.max(jnp.where(mw, sw, NEG), axis=1, keepdims=True), sw_new)
        pw = jnp.where(mw, jnp.exp(sw - m), 0.0)
        pw_new = jnp.exp(sw_new - m)
        ow = (_dot_nt(pw, wc_ref[0, vsl, :], HI) + pw_new * vw_new) / jnp.maximum(
            jnp.sum(pw, axis=1, keepdims=True) + pw_new, 1e-30)
        for g in range(NSA_GROUP):
            c0 = 12 + (h * NSA_GROUP + g) * 3
            outs.append(gates[:, c0:c0 + 1] * oc[h * NSA_GROUP + g:h * NSA_GROUP + g + 1, :]
                        + gates[:, c0 + 1:c0 + 2] * osel[g:g + 1, :] + gates[:, c0 + 2:c0 + 3] * ow[g:g + 1, :])
    o_ref[0] = jnp.concatenate(outs, axis=1)
    new_col = _dot_sel(jnp.ones((8, wbl), F32), jnp.concatenate([nwin, jnp.zeros((7, 2 * NSA_KV), F32)], axis=0),
                       ((0,), (0,)), sel_right=True)
    win_ref[0] = jnp.where(_iota((1, wbl), 1) == wbl - 1, new_col, pltpu.roll(wc_ref[0], wbl - 1, 1))


def _nsa_sample_attn(cache_t, page_table, idx, q3, row3, nwin3, win_t, win_off, gates3, oc, past):
    bsz = q3.shape[0]
    wbl = win_t.shape[2]
    ncomplete = past // NSA_BLOCK

    n = jnp.clip(idx, 0, ncomplete - 1)
    page_of = jnp.take_along_axis(page_table, n // 2, axis=1)
    half_of = n % 2

    def blk_map(b, pg, hf, j):
        return (pg[b, j], 1, 0)

    blk_specs = [pl.BlockSpec((1, 2 * LANE, PAGE), functools.partial(blk_map, j=j))
                 for j in range(NSA_KV_HEADS * NSA_TOPK)]
    one = lambda w: pl.BlockSpec((1, 1, w), lambda b, pt, ix: (b, 0, 0))
    win_in = pl.BlockSpec((1, 2 * NSA_KV, wbl), lambda b, pt, ix: (b + win_off, 0, 0))
    win_out = pl.BlockSpec((1, 2 * NSA_KV, wbl), lambda b, pt, ix: (b, 0, 0))
    return pl.pallas_call(
        functools.partial(_nsa_sample_attn_kernel, past=past),
        grid_spec=pltpu.PrefetchScalarGridSpec(
            num_scalar_prefetch=2,
            grid=(bsz,),
            in_specs=blk_specs + [one(NSA_Q), one(4 * NSA_KV), one(2 * NSA_KV), win_in, one(SMALL_W),
                                  pl.BlockSpec((1, NSA_HEADS, HEAD_DIM), lambda b, pt, ix: (b, 0, 0))],
            out_specs=[one(NSA_Q), win_out]),
        out_shape=[jax.ShapeDtypeStruct((bsz, 1, NSA_Q), F32),
                   jax.ShapeDtypeStruct((bsz, 2 * NSA_KV, wbl), F32)],
        compiler_params=_cparams("parallel"),
        name="nsa_sample_attn",
    )(page_of, half_of, *([cache_t] * (NSA_KV_HEADS * NSA_TOPK)), q3, row3, nwin3, win_t, gates3, oc)


def _outproj_kernel(og_ref, on_ref, os_ref, x_ref, wo_ref, g_ref, wq_ref, x1_ref, xn_ref, qh_ref):
    x1 = (x_ref[...] + _dot(og_ref[...].astype(BF16), wo_ref[0:GDN_V, :])
          + _dot(on_ref[...].astype(BF16), wo_ref[GDN_V:GDN_V + NSA_Q, :])
          + _dot(os_ref[...].astype(BF16), wo_ref[GDN_V + NSA_Q:, :]))
    x1_ref[...] = x1
    xn = (x1 * lax.rsqrt(jnp.mean(x1 * x1, axis=-1, keepdims=True) + EPS) * g_ref[...]).astype(BF16)
    xn_ref[...] = xn
    qh_ref[...] = _dot(xn, wq_ref[...])


def _outproj(og, on, os_, x, w_out, norm_g, wq):
    n = x.shape[0]
    tm = min(n, 512)
    dq = PEER_HEADS * PEER_DKEY
    tok = lambda w: pl.BlockSpec((tm, w), lambda i: (i, 0))
    const = lambda a, b: pl.BlockSpec((a, b), lambda i: (0, 0))
    return pl.pallas_call(
        _outproj_kernel,
        grid=(n // tm,),
        in_specs=[tok(GDN_V), tok(NSA_Q), tok(SSM_DI), tok(D_MODEL), const(D_MODEL, D_MODEL), const(1, D_MODEL),
                  const(D_MODEL, dq)],
        out_specs=[tok(D_MODEL), tok(D_MODEL), tok(dq)],
        out_shape=[jax.ShapeDtypeStruct((n, D_MODEL), F32), jax.ShapeDtypeStruct((n, D_MODEL), BF16),
                   jax.ShapeDtypeStruct((n, dq), F32)],
        compiler_params=_cparams("parallel"),
        name="outproj",
    )(og, on, os_, x, w_out.astype(BF16), norm_g.reshape(1, D_MODEL), wq.astype(BF16))


PEER_PAIRS = tuple((a, b) for a in range(PEER_TOPK) for b in range(PEER_TOPK) if (a + 1) * (b + 1) <= PEER_TOPK)


def _topk_rows(s, k):
    n = s.shape[0]
    nio = _iota(s.shape, 0).astype(F32)
    work = s
    rank = jnp.full(s.shape, float(k), F32)
    vals = []
    for r in range(k):
        m = _colmax(work)
        idx = jnp.min(jnp.where(work == m, nio, float(n)), axis=0, keepdims=True)
        pick = nio == idx
        rank = jnp.where(pick, float(r), rank)
        work = jnp.where(pick, -jnp.inf, work)
        vals.append(m)
    return rank, vals


def _dup_bf16_bits(x):
    bits = lax.shift_right_logical(pltpu.bitcast(x.astype(BF16).astype(F32), jnp.uint32), jnp.uint32(16))
    return pltpu.bitcast(bits | lax.shift_left(bits, jnp.uint32(16)), jnp.int32)


def _peer_keys_kernel(qh_ref, sk_ref, r2_ref, e2_ref, lx_ref, e1_ref):
    half = PEER_DKEY // 2
    for h in range(PEER_HEADS):
        q1 = qh_ref[:, h * PEER_DKEY:h * PEER_DKEY + half]
        q2 = qh_ref[:, h * PEER_DKEY + half:(h + 1) * PEER_DKEY]
        s1 = _dot_nt(sk_ref[0], q1, HI)
        s2 = _dot_nt(sk_ref[1], q2, HI)
        rank1, v1 = _topk_rows(s1, PEER_TOPK)
        rank2, v2 = _topk_rows(s2, PEER_TOPK)
        npad = -len(PEER_PAIRS) % 8
        cand = jnp.concatenate([v1[a] + v2[b] for a, b in PEER_PAIRS]
                               + [jnp.full((npad, s1.shape[1]), -jnp.inf, F32)], axis=0)
        crank, _ = _topk_rows(cand, PEER_TOPK)
        chosen = crank < PEER_TOPK
        z = _colsum(jnp.where(chosen, jnp.exp(cand - cand[0:1]), 0.0))
        chosen_f = jnp.where(chosen, 1.0, 0.0)
        lx = jnp.zeros(s1.shape, F32)
        for a in range(PEER_TOPK):
            rows = [i for i, (pa, _) in enumerate(PEER_PAIRS) if pa == a]
            cnt = chosen_f[rows[0]:rows[0] + 1]
            for i in rows[1:]:
                cnt = cnt + chosen_f[i:i + 1]
            lx = jnp.where(rank1 == float(a), cnt, lx)
        r2_ref[h] = rank2.astype(BF16)
        e2_ref[h] = (jnp.exp(s2 - v2[0]) / z).astype(BF16)
        lx_ref[h] = _dup_bf16_bits(lx)
        e1_ref[h] = _dup_bf16_bits(jnp.where(rank1 < PEER_TOPK, jnp.exp(s1 - v1[0]), 0.0))


def _peer_keys(qh, subkeys):
    n = qh.shape[0]
    tk = min(n, 256)
    dq = PEER_HEADS * PEER_DKEY
    half = PEER_DKEY // 2
    out = pl.BlockSpec((PEER_HEADS, PEER_KEYS, tk), lambda i: (0, 0, i))
    return pl.pallas_call(
        _peer_keys_kernel,
        grid=(n // tk,),
        in_specs=[pl.BlockSpec((tk, dq), lambda i: (i, 0)),
                  pl.BlockSpec((2, PEER_KEYS, half), lambda i: (0, 0, 0))],
        out_specs=[out] * 4,
        out_shape=[jax.ShapeDtypeStruct((PEER_HEADS, PEER_KEYS, n), dt) for dt in (BF16, BF16, jnp.int32, jnp.int32)],
        compiler_params=_cparams("parallel"),
        name="peer_keys",
    )(qh, subkeys)


PEER_AC = 8


def _gelu_tanh(x):
    c = math.sqrt(2.0 / math.pi)
    return x * (0.5 + 0.5 * jnp.tanh(x * (c + (c * 0.044715) * (x * x))))


def _row_tile_bf16(rows, a):
    t = pltpu.bitcast(jnp.broadcast_to(rows[a:a + 1, :], rows.shape), BF16)
    return jnp.concatenate([t] * (PEER_KEYS // t.shape[0]), axis=0)


PEER_SUB = 4


def _peer_kernel(xn_ref, u_ref, vt_ref, r2_ref, e2_ref, lx_ref, e1_ref, x1_ref, o_ref, acc, *, nsteps):
    j = pl.program_id(1)

    @pl.when(j == 0)
    def _():
        acc[...] = jnp.zeros_like(acc)

    xn = xn_ref[...]
    nsb = PEER_AC // PEER_SUB
    width = PEER_SUB * PEER_KEYS

    def hidden(sb):
        return _dot_nt(u_ref[sb * width:(sb + 1) * width, :], xn)

    assert PEER_AC == 8
    base = pl.multiple_of(j * PEER_AC, PEER_AC)
    lx_rows = [lx_ref[h, pl.ds(base, PEER_AC), :] for h in range(PEER_HEADS)]
    e1_rows = [e1_ref[h, pl.ds(base, PEER_AC), :] for h in range(PEER_HEADS)]

    def gates(sb):
        out = []
        for a in range(sb * PEER_SUB, (sb + 1) * PEER_SUB):
            gate = None
            for h in range(PEER_HEADS):
                term = jnp.where(r2_ref[h] < _row_tile_bf16(lx_rows[h], a), e2_ref[h],
                                 jnp.zeros((), BF16)) * _row_tile_bf16(e1_rows[h], a)
                gate = term if gate is None else gate + term
            out.append(gate)
        return out

    part = None
    ht_next = hidden(0)
    for sb in range(nsb):
        ht = ht_next
        gs = gates(sb)
        if sb + 1 < nsb:
            ht_next = hidden(sb + 1)
        ws = [gs[a] * _gelu_tanh(ht[a * PEER_KEYS:(a + 1) * PEER_KEYS, :]).astype(BF16) for a in range(PEER_SUB)]
        t = _dot(vt_ref[:, sb * width:(sb + 1) * width], jnp.concatenate(ws, axis=0))
        part = t if part is None else t + part
    acc[...] += part

    @pl.when(j == nsteps - 1)
    def _():
        o_ref[...] = x1_ref[...] + acc[...].T


def _peer(xn, u_bf, vt_bf, r2, e2, lx, e1, x1):
    n = xn.shape[0]
    tm = min(n, 512)
    ec = PEER_AC * PEER_KEYS
    nsteps = PEER_KEYS // PEER_AC
    tab = pl.BlockSpec((PEER_HEADS, PEER_KEYS, tm), lambda i, j: (0, 0, i))
    return pl.pallas_call(
        functools.partial(_peer_kernel, nsteps=nsteps),
        grid=(n // tm, nsteps),
        in_specs=[pl.BlockSpec((tm, D_MODEL), lambda i, j: (i, 0)),
                  pl.BlockSpec((ec, D_MODEL), lambda i, j: (j, 0)),
                  pl.BlockSpec((D_MODEL, ec), lambda i, j: (0, j)),
                  tab, tab, tab, tab,
                  pl.BlockSpec((tm, D_MODEL), lambda i, j: (i, 0))],
        out_specs=pl.BlockSpec((tm, D_MODEL), lambda i, j: (i, 0)),
        out_shape=jax.ShapeDtypeStruct((n, D_MODEL), F32),
        scratch_shapes=[pltpu.VMEM((D_MODEL, tm), F32)],
        compiler_params=_cparams("parallel", "arbitrary"),
        name="peer",
    )(xn, u_bf, vt_bf, r2, e2, lx, e1, x1)


def _pad_axis(a, axis, size):
    if a.shape[axis] == size:
        return a
    pad = [(0, 0)] * a.ndim
    pad[axis] = (0, size - a.shape[axis])
    return jnp.pad(a, pad)


def _mixers(x2d, bsz, t, pos, conv_hist, gdn0, ssm0, lp, tokens_as_time=False):
    conv_in, z, q, kv, small = _inproj(x2d, lp["norm_mix"], lp["w_cat"])
    r3 = lambda a: a.reshape(bsz, t, -1)
    c, new_conv = _conv(r3(conv_in), conv_hist, lp["conv_w"], lp["conv_b"])
    tpad = -(-t // CHUNK) * CHUNK
    t_valid = None if tpad == t else t
    cp, zp, sp = (_pad_axis(a, 1, tpad) for a in (c, r3(z), r3(small)))
    og, s_gdn = _gdn(cp, zp, sp, gdn0, lp["gdn_a_log"], lp["gdn_dt_bias"], lp["gdn_norm"], t_valid=t_valid)
    os_, s_ssm = _ssd(cp, zp, sp, ssm0, lp["ssm_a_log"], lp["ssm_dt_bias"], lp["ssm_d"], lp["ssm_norm"],
                      t_valid=t_valid)
    lead = (1, bsz * t) if tokens_as_time else (bsz, t)
    qn, rows, win, gates, ks, vs, kw, vw = _nsaproj(q.reshape(*lead, -1), kv.reshape(*lead, -1),
                                                    small.reshape(*lead, -1), pos, lp["nsa_qk_norm"])
    return dict(og=og[:, :t], os=os_[:, :t], new_conv=new_conv, s_gdn=s_gdn, s_ssm=s_ssm, q=r3(qn), rows=r3(rows),
                win=r3(win), gates=r3(gates), ks=ks, vs=vs, kw=kw, vw=vw)


def _channel_mix(og, on, os_, x, lp):
    x1, xn, qh = _outproj(og, on, os_, x, lp["w_out"], lp["norm_ffn"], lp["peer_wq"])
    n = x.shape[0]
    npad = -(-n // LANE) * LANE
    x1, xn, qh = (_pad_axis(a, 0, npad) for a in (x1, xn, qh))
    r2, e2, lx, e1 = _peer_keys(qh, lp["peer_subkeys"])
    return _peer(xn, lp["peer_u"], lp["peer_vt"], r2, e2, lx, e1, x1)[:n]


def kernel(x_prompt, x_sample, cache_nsa_kv, cache_nsa_win, state_conv, state_gdn, state_ssm, page_table, norm_mix, w_in, conv_w, conv_b, gdn_a_log, gdn_dt_bias, gdn_norm, ssm_a_log, ssm_dt_bias, ssm_d, ssm_norm, nsa_qk_norm, nsa_pe, nsa_phi, w_out, norm_ffn, peer_wq, peer_subkeys, peer_u, peer_v):
    bp, tp, _ = x_prompt.shape
    bs, ts, _ = x_sample.shape
    past = page_table.shape[1] * cache_nsa_kv.shape[2]
    n_pool = cache_nsa_kv.shape[1]
    yp = x_prompt.reshape(bp * tp, D_MODEL)
    ys = x_sample.reshape(bs * ts, D_MODEL)
    depth = norm_mix.shape[0]
    wbl = cache_nsa_win.shape[2]
    cache_t = jnp.transpose(cache_nsa_kv, (0, 1, 3, 4, 5, 2)).reshape(depth * n_pool, 4 * NSA_KV, PAGE)
    win_t = jnp.transpose(cache_nsa_win, (0, 1, 3, 4, 5, 2)).reshape(depth * bs, 2 * NSA_KV, wbl)
    outs_p, outs_s = [], []
    for l in range(norm_mix.shape[0]):
        lp = dict(norm_mix=norm_mix[l], w_cat=_cat_w_in(w_in[l]), conv_w=conv_w[l], conv_b=conv_b[l],
                  gdn_a_log=gdn_a_log[l], gdn_dt_bias=gdn_dt_bias[l], gdn_norm=gdn_norm[l],
                  ssm_a_log=ssm_a_log[l], ssm_dt_bias=ssm_dt_bias[l], ssm_d=ssm_d[l], ssm_norm=ssm_norm[l],
                  nsa_qk_norm=nsa_qk_norm[l], nsa_pe=nsa_pe[l], nsa_phi=nsa_phi[l], w_out=w_out[l],
                  norm_ffn=norm_ffn[l], peer_wq=peer_wq[l], peer_subkeys=peer_subkeys[l],
                  peer_u=peer_u[l].astype(BF16), peer_vt=peer_v[l].T.astype(BF16))
        m = _mixers(yp, bp, tp, jnp.arange(tp, dtype=jnp.int32), jnp.zeros((bp, CONV_W - 1, CONV_CH), F32),
                    jnp.zeros((bp, GDN_HEADS, GDN_DK, GDN_DV), F32),
                    jnp.zeros((bp, SSM_HEADS, SSM_HEADDIM, SSM_DSTATE), F32), lp)
        pages = m["rows"].reshape(bp * tp // PAGE, PAGE, 4 * NSA_KV)
        ident = jnp.arange(bp * tp // PAGE, dtype=jnp.int32).reshape(bp, tp // PAGE)
        ck, cv = _compress(pages, ident, lp["nsa_pe"], lp["nsa_phi"])
        gates_t = jnp.transpose(m["gates"][..., 12:12 + 3 * NSA_HEADS], (0, 2, 1))
        o_nsa = _nsa_prompt(m["q"], ck, cv, m["ks"], m["vs"], m["kw"], m["vw"], gates_t)
        yp = _channel_mix(m["og"].reshape(bp * tp, -1), o_nsa.reshape(bp * tp, -1), m["os"].reshape(bp * tp, -1),
                          yp, lp)
        keep = min(NSA_WINDOW, tp)
        outs_p.append((m["rows"].reshape(bp, tp, 4, NSA_KV_HEADS, HEAD_DIM),
                       m["win"][:, tp - keep:].reshape(bp, keep, 2, NSA_KV_HEADS, HEAD_DIM),
                       m["new_conv"], m["s_gdn"], m["s_ssm"]))
        m = _mixers(ys, bs, ts, jnp.full((bs * ts,), past, jnp.int32), state_conv[l], state_gdn[l], state_ssm[l], lp,
                    tokens_as_time=True)
        pages_l = page_table + l * n_pool
        ck, cv = _compress_t(cache_t, pages_l, lp["nsa_pe"], lp["nsa_phi"])
        q3 = m["q"].reshape(bs, 1, NSA_Q)
        oc, idx = _nsa_sample_cmp(q3, ck, cv, past)
        o_nsa, new_win_t = _nsa_sample_attn(
            cache_t, pages_l, idx.reshape(bs, NSA_KV_HEADS * NSA_TOPK), q3, m["rows"].reshape(bs, 1, 4 * NSA_KV),
            m["win"].reshape(bs, 1, 2 * NSA_KV), win_t, l * bs, m["gates"].reshape(bs, 1, SMALL_W), oc, past)
        new_win = jnp.transpose(new_win_t.reshape(bs, 2, NSA_KV_HEADS, HEAD_DIM, wbl), (0, 4, 1, 2, 3))
        ys = _channel_mix(m["og"].reshape(bs, -1), o_nsa.reshape(bs, -1), m["os"].reshape(bs, -1), ys, lp)
        outs_s.append((m["rows"].reshape(bs, ts, 4, NSA_KV_HEADS, HEAD_DIM),
                       new_win, m["new_conv"], m["s_gdn"], m["s_ssm"]))
    stack = lambda outs, i: jnp.stack([o[i] for o in outs])
    return (yp.reshape(bp, tp, D_MODEL), ys.reshape(bs, ts, D_MODEL),
            stack(outs_p, 0), stack(outs_p, 1), stack(outs_p, 2), stack(outs_p, 3), stack(outs_p, 4),
            stack(outs_s, 0), stack(outs_s, 1), stack(outs_s, 2), stack(outs_s, 3), stack(outs_s, 4))
```

```python
import functools
import math

import jax
import jax.numpy as jnp
from jax import lax
from jax.experimental import pallas as pl
from jax.experimental.pallas import tpu as pltpu

F32 = jnp.float32
BF16 = jnp.bfloat16
HI = lax.Precision.HIGHEST

D_MODEL = 1024
HEAD_DIM = 64
GDN_HEADS = 4
GDN_DK = 64
GDN_DV = 64
CHUNK = 64
NSA_HEADS = 8
NSA_KV_HEADS = 2
NSA_GROUP = 4
NSA_BLOCK = 64
NSA_TOPK = 16
NSA_WINDOW = 512
SSM_HEADS = 4
SSM_HEADDIM = 64
SSM_GROUPS = 2
SSM_DSTATE = 128
CONV_W = 4
PEER_KEYS = 128
PEER_HEADS = 8
PEER_TOPK = 16
PEER_DKEY = 256
ROPE_THETA = 10000.0
EPS = 1e-6

GDN_QK = GDN_HEADS * GDN_DK
GDN_V = GDN_HEADS * GDN_DV
SSM_DI = SSM_HEADS * SSM_HEADDIM
SSM_BC = SSM_GROUPS * SSM_DSTATE
GDN_CONV_CH = 2 * GDN_QK + GDN_V
CONV_CH = GDN_CONV_CH + SSM_DI + 2 * SSM_BC
NSA_Q = NSA_HEADS * HEAD_DIM
NSA_KV = NSA_KV_HEADS * HEAD_DIM
SMALL_W = 128
LANE = 128
VMEM_LIMIT = 48 * 1024 * 1024


def _cparams(*sem):
    return pltpu.CompilerParams(dimension_semantics=sem, vmem_limit_bytes=VMEM_LIMIT)


def _dot(a, b):
    return jnp.dot(a, b, preferred_element_type=F32)


def _split_bf16(a):
    hi = a.astype(BF16)
    return hi, (a - hi.astype(F32)).astype(BF16)


def _dot3_general(a, b, dims):
    a_hi, a_lo = _split_bf16(a)
    b_hi, b_lo = _split_bf16(b)
    dg = lambda x, y: lax.dot_general(x, y, (dims, ((), ())), preferred_element_type=F32)
    return dg(a_hi, b_hi) + dg(a_lo, b_hi) + dg(a_hi, b_lo)


def _dot_hi(a, b):
    return _dot3_general(a, b, ((1,), (0,)))


def _split3(a):
    hi = a.astype(BF16)
    r = a - hi.astype(F32)
    mid = r.astype(BF16)
    return hi, mid, (r - mid.astype(F32)).astype(BF16)


def _dot_sel(sel, x, dims=((1,), (0,)), sel_right=False):
    out = None
    for piece in _split3(x):
        ops = (piece, sel.astype(BF16)) if sel_right else (sel.astype(BF16), piece)
        t = lax.dot_general(*ops, (dims, ((), ())), preferred_element_type=F32)
        out = t if out is None else out + t
    return out


def _dot_nt(a, b, precision=None):
    if precision is not None:
        return _dot3_general(a, b, ((1,), (1,)))
    return lax.dot_general(a, b, (((1,), (1,)), ((), ())), preferred_element_type=F32)


def _dot_tn(a, b, precision=None):
    if precision is not None:
        return _dot3_general(a, b, ((0,), (0,)))
    return lax.dot_general(a, b, (((0,), (0,)), ((), ())), preferred_element_type=F32)


def _sigmoid(x):
    return 1.0 / (1.0 + jnp.exp(-x))


def _softplus(x):
    return jnp.maximum(x, 0.0) + jnp.log1p(jnp.exp(-jnp.abs(x)))


def _iota(shape, dim):
    return lax.broadcasted_iota(jnp.int32, shape, dim)


IN_GROUPS = (("conv", CONV_CH), ("z", GDN_V + SSM_DI), ("q", NSA_Q), ("kv", 6 * NSA_KV), ("small", SMALL_W))
IN_CAT = sum(w for _, w in IN_GROUPS)


def _inproj_kernel(x_ref, g_ref, w_ref, conv_ref, z_ref, q_ref, kv_ref, small_ref):
    x = x_ref[...]
    ms = jnp.mean(x * x, axis=-1, keepdims=True)
    xn = (x * lax.rsqrt(ms + EPS) * g_ref[...]).astype(BF16)
    off = 0
    for ref, (_, w) in zip((conv_ref, z_ref, q_ref, kv_ref, small_ref), IN_GROUPS):
        ref[...] = _dot(xn, w_ref[:, off:off + w])
        off += w


def _inproj(x2d, norm_g, w_cat):
    n = x2d.shape[0]
    tm = min(n, 512)
    return pl.pallas_call(
        _inproj_kernel,
        grid=(n // tm,),
        in_specs=[pl.BlockSpec((tm, D_MODEL), lambda i: (i, 0)),
                  pl.BlockSpec((1, D_MODEL), lambda i: (0, 0)),
                  pl.BlockSpec((D_MODEL, IN_CAT), lambda i: (0, 0))],
        out_specs=[pl.BlockSpec((tm, w), lambda i: (i, 0)) for _, w in IN_GROUPS],
        out_shape=[jax.ShapeDtypeStruct((n, w), F32) for _, w in IN_GROUPS],
        compiler_params=_cparams("parallel"),
        name="inproj",
    )(x2d, norm_g.reshape(1, D_MODEL), w_cat)


def _cat_w_in(w_in):
    o = 0
    parts = {}
    for name, w in (("conv", CONV_CH), ("gdn_z", GDN_V), ("gdn_b", GDN_HEADS), ("gdn_a", GDN_HEADS),
                    ("ssm_z", SSM_DI), ("ssm_dt", SSM_HEADS), ("nsa_q", NSA_Q), ("nsa_kv", 6 * NSA_KV),
                    ("nsa_g", 3 * NSA_HEADS)):
        parts[name] = w_in[:, o:o + w]
        o += w
    small = jnp.concatenate([parts["gdn_b"], parts["gdn_a"], parts["ssm_dt"], parts["nsa_g"]], axis=1)
    small = jnp.pad(small, ((0, 0), (0, SMALL_W - small.shape[1])))
    return jnp.concatenate([parts["conv"], parts["gdn_z"], parts["ssm_z"], parts["nsa_q"], parts["nsa_kv"], small],
                           axis=1).astype(BF16)


HALO = 8


def _conv_kernel(x_ref, hist_ref, w_ref, b_ref, c_ref, nc_ref, xe, *, tm, nt):
    t = pl.program_id(1)

    @pl.when(t == 0)
    def _():
        xe[HALO - 3:HALO, :] = hist_ref[0]

    xe[HALO:HALO + tm, :] = x_ref[0]
    y = b_ref[...] + xe[HALO - 3:HALO - 3 + tm, :] * w_ref[0:1, :]
    for j in range(1, CONV_W):
        y = y + xe[HALO - 3 + j:HALO - 3 + j + tm, :] * w_ref[j:j + 1, :]
    c_ref[0] = y * _sigmoid(y)
    last = xe[HALO + tm - 3:HALO + tm, :]

    @pl.when(t == nt - 1)
    def _():
        nc_ref[0] = last

    xe[HALO - 3:HALO, :] = last


def _conv(x, hist, w, b):
    bsz, t, ch = x.shape
    tm = min(t, 512)
    nt = t // tm
    return pl.pallas_call(
        functools.partial(_conv_kernel, tm=tm, nt=nt),
        grid=(bsz, nt),
        in_specs=[pl.BlockSpec((1, tm, ch), lambda i, j: (i, j, 0)),
                  pl.BlockSpec((1, CONV_W - 1, ch), lambda i, j: (i, 0, 0)),
                  pl.BlockSpec((CONV_W, ch), lambda i, j: (0, 0)),
                  pl.BlockSpec((1, ch), lambda i, j: (0, 0))],
        out_specs=[pl.BlockSpec((1, tm, ch), lambda i, j: (i, j, 0)),
                   pl.BlockSpec((1, CONV_W - 1, ch), lambda i, j: (i, 0, 0))],
        out_shape=[jax.ShapeDtypeStruct((bsz, t, ch), F32),
                   jax.ShapeDtypeStruct((bsz, CONV_W - 1, ch), F32)],
        scratch_shapes=[pltpu.VMEM((HALO + tm, ch), F32)],
        compiler_params=_cparams("parallel", "arbitrary"),
        name="conv",
    )(x, hist, w, b.reshape(1, ch))


def _tri(n, strict=False):
    r, c = _iota((n, n), 0), _iota((n, n), 1)
    return (r > c) if strict else (r >= c)


def _row_of(col):
    n = col.shape[0]
    eye = (_iota((n, n), 0) == _iota((n, n), 1)).astype(F32)
    return _dot_sel(jnp.ones((n, n), F32), eye * col)


def _gdn_kernel(q_ref, k_ref, v_ref, z_ref, small_ref, s0_ref, alog_ref, dtb_ref, ng_ref, o_ref, s_ref, st,
                *, nc, t_valid, bb):
    c = pl.program_id(1)

    @pl.when(c == 0)
    def _():
        st[...] = s0_ref[...]

    incl = _tri(CHUNK)
    strict = _tri(CHUNK, strict=True)
    if t_valid is not None:
        valid = (c * CHUNK + _iota((CHUNK, 1), 0)) < t_valid
    chains = [(b, h) for b in range(bb) for h in range(GDN_HEADS)]
    beta_all, gc_all = {}, {}
    for b in range(bb):
        sm = small_ref[b]
        bt = _sigmoid(sm[:, 0:GDN_HEADS])
        g_all = -jnp.exp(alog_ref[...]) * _softplus(sm[:, GDN_HEADS:2 * GDN_HEADS] + dtb_ref[...])
        if t_valid is not None:
            bt = jnp.where(valid, bt, 0.0)
            g_all = jnp.where(valid, g_all, 0.0)
        beta_all[b] = bt
        gc_all[b] = _dot_sel(incl.astype(F32), g_all)
    q, k, v, beta, gc = {}, {}, {}, {}, {}
    for c_ in chains:
        b, h = c_
        sl = slice(h * GDN_DK, (h + 1) * GDN_DK)
        qq = q_ref[b][:, sl]
        kk = k_ref[b][:, sl]
        vv = v_ref[b][:, sl]
        qq = qq * lax.rsqrt(jnp.sum(qq * qq, axis=-1, keepdims=True) + EPS) * (GDN_DK ** -0.5)
        kk = kk * lax.rsqrt(jnp.sum(kk * kk, axis=-1, keepdims=True) + EPS)
        if t_valid is not None:
            qq = jnp.where(valid, qq, 0.0)
            kk = jnp.where(valid, kk, 0.0)
            vv = jnp.where(valid, vv, 0.0)
        q[c_], k[c_], v[c_] = qq, kk, vv
        beta[c_] = beta_all[b][:, h:h + 1]
        gc[c_] = gc_all[b][:, h:h + 1]
    diff = {c_: gc[c_] - _row_of(gc[c_]) for c_ in chains}
    kkt = {c_: _dot_nt(k[c_], k[c_], HI) for c_ in chains}
    qkt = {c_: _dot_nt(q[c_], k[c_], HI) for c_ in chains}
    egc = {c_: jnp.exp(gc[c_]) for c_ in chains}
    p = {c_: -(beta[c_] * kkt[c_] * jnp.where(strict, jnp.exp(jnp.where(strict, diff[c_], 0.0)), 0.0))
         for c_ in chains}
    x = {c_: jnp.concatenate([v[c_] * beta[c_], k[c_] * (beta[c_] * egc[c_])], axis=-1) for c_ in chains}
    for it in range(6):
        px = {c_: _dot_hi(p[c_], x[c_]) for c_ in chains}
        if it < 5:
            p = {c_: _dot_hi(p[c_], p[c_]) for c_ in chains}
        x = {c_: x[c_] + px[c_] for c_ in chains}
    aqk = {c_: qkt[c_] * jnp.where(incl, jnp.exp(jnp.where(incl, diff[c_], 0.0)), 0.0) for c_ in chains}
    s = {c_: st[c_[0], c_[1]] for c_ in chains}
    wks = {c_: _dot_hi(x[c_][:, GDN_DV:], s[c_]) for c_ in chains}
    qs = {c_: _dot_hi(q[c_] * egc[c_], s[c_]) for c_ in chains}
    w = {c_: x[c_][:, :GDN_DV] - wks[c_] for c_ in chains}
    gl = {c_: gc[c_][CHUNK - 1:CHUNK, :] for c_ in chains}
    kdw = {c_: _dot_tn(k[c_] * jnp.exp(gl[c_] - gc[c_]), w[c_], HI) for c_ in chains}
    aw = {c_: _dot_hi(aqk[c_], w[c_]) for c_ in chains}
    for c_ in chains:
        st[c_[0], c_[1]] = jnp.exp(gl[c_]) * s[c_] + kdw[c_]
    for b in range(bb):
        outs = []
        for h in range(GDN_HEADS):
            o = qs[(b, h)] + aw[(b, h)]
            o = o * lax.rsqrt(jnp.mean(o * o, axis=-1, keepdims=True) + EPS) * ng_ref[...]
            zh = z_ref[b][:, h * GDN_DK:(h + 1) * GDN_DK]
            outs.append(o * (zh * _sigmoid(zh)))
        o_ref[b] = jnp.concatenate(outs, axis=-1)

    @pl.when(c == nc - 1)
    def _():
        s_ref[...] = st[...]


GDN_BB = 2


def _gdn(c, z, small, s0, a_log, dt_bias, norm_g, t_valid=None):
    bsz, t, _ = c.shape
    nc = t // CHUNK
    bb = GDN_BB
    return pl.pallas_call(
        functools.partial(_gdn_kernel, nc=nc, t_valid=t_valid, bb=bb),
        grid=(bsz // bb, nc),
        in_specs=[pl.BlockSpec((bb, CHUNK, GDN_QK), lambda i, j: (i, j, 0)),
                  pl.BlockSpec((bb, CHUNK, GDN_QK), lambda i, j: (i, j, 1)),
                  pl.BlockSpec((bb, CHUNK, GDN_V), lambda i, j: (i, j, 2)),
                  pl.BlockSpec((bb, CHUNK, GDN_V), lambda i, j: (i, j, 0)),
                  pl.BlockSpec((bb, CHUNK, SMALL_W), lambda i, j: (i, j, 0)),
                  pl.BlockSpec((bb, GDN_HEADS, GDN_DK, GDN_DV), lambda i, j: (i, 0, 0, 0)),
                  pl.BlockSpec((1, GDN_HEADS), lambda i, j: (0, 0)),
                  pl.BlockSpec((1, GDN_HEADS), lambda i, j: (0, 0)),
                  pl.BlockSpec((1, GDN_DV), lambda i, j: (0, 0))],
        out_specs=[pl.BlockSpec((bb, CHUNK, GDN_V), lambda i, j: (i, j, 0)),
                   pl.BlockSpec((bb, GDN_HEADS, GDN_DK, GDN_DV), lambda i, j: (i, 0, 0, 0))],
        out_shape=[jax.ShapeDtypeStruct((bsz, t, GDN_V), F32),
                   jax.ShapeDtypeStruct((bsz, GDN_HEADS, GDN_DK, GDN_DV), F32)],
        scratch_shapes=[pltpu.VMEM((bb, GDN_HEADS, GDN_DK, GDN_DV), F32)],
        compiler_params=_cparams("parallel", "arbitrary"),
        name="gdn",
    )(c, c, c, z, small, s0, a_log.reshape(1, -1), dt_bias.reshape(1, -1), norm_g.reshape(1, -1))


def _ssd_kernel(x_ref, b_ref, c_ref, z_ref, small_ref, h0_ref, alog_ref, dtb_ref, dsk_ref, ng_ref, o_ref, h_ref, st,
                *, nc, t_valid):
    ci = pl.program_id(1)

    @pl.when(ci == 0)
    def _():
        st[...] = h0_ref[0]

    sm = small_ref[0]
    dt_all = _softplus(sm[:, 2 * GDN_HEADS:2 * GDN_HEADS + SSM_HEADS] + dtb_ref[...])
    xs = x_ref[0]
    bm_all = b_ref[0]
    cm_all = c_ref[0]
    if t_valid is not None:
        valid = (ci * CHUNK + _iota((CHUNK, 1), 0)) < t_valid
        dt_all = jnp.where(valid, dt_all, 0.0)
        xs = jnp.where(valid, xs, 0.0)
        bm_all = jnp.where(valid, bm_all, 0.0)
        cm_all = jnp.where(valid, cm_all, 0.0)
    incl = _tri(CHUNK)
    cs_all = _dot_sel(incl.astype(F32), dt_all * (-jnp.exp(alog_ref[...])))
    rep = SSM_HEADS // SSM_GROUPS
    heads = range(SSM_HEADS)
    x = [xs[:, h * SSM_HEADDIM:(h + 1) * SSM_HEADDIM] for h in heads]
    bm = [bm_all[:, (h // rep) * SSM_DSTATE:(h // rep + 1) * SSM_DSTATE] for h in heads]
    cm = [cm_all[:, (h // rep) * SSM_DSTATE:(h // rep + 1) * SSM_DSTATE] for h in heads]
    cs = [cs_all[:, h:h + 1] for h in heads]
    rows = [_row_of(cs[h]) for h in heads]
    cbr = [_dot_nt(cm[g * rep], bm[g * rep], HI) for g in range(SSM_GROUPS)]
    xdt = [x[h] * dt_all[:, h:h + 1] for h in heads]
    cl = [cs[h][CHUNK - 1:CHUNK, :] for h in heads]
    hs = [st[h] for h in heads]
    cb = [cbr[h // rep] * jnp.where(incl, jnp.exp(jnp.where(incl, cs[h] - rows[h], 0.0)), 0.0) for h in heads]
    y_diag = [_dot_hi(cb[h], xdt[h]) for h in heads]
    y_off = [_dot_nt(cm[h] * jnp.exp(cs[h]), hs[h], HI) for h in heads]
    upd = [_dot_tn(xdt[h] * jnp.exp(cl[h] - cs[h]), bm[h], HI) for h in heads]
    for h in heads:
        st[h] = jnp.exp(cl[h]) * hs[h] + upd[h]
    y = jnp.concatenate([y_diag[h] + y_off[h] + dsk_ref[0:1, h:h + 1] * x[h] for h in heads], axis=-1)
    zz = z_ref[0]
    y = y * (zz * _sigmoid(zz))
    gs = SSM_DI // SSM_GROUPS
    outs = []
    for g in range(SSM_GROUPS):
        yg = y[:, g * gs:(g + 1) * gs]
        outs.append(yg * lax.rsqrt(jnp.mean(yg * yg, axis=-1, keepdims=True) + EPS) * ng_ref[0:1, g * gs:(g + 1) * gs])
    o_ref[0] = jnp.concatenate(outs, axis=-1)

    @pl.when(ci == nc - 1)
    def _():
        h_ref[0] = st[...]


def _ssd(c, z, small, h0, a_log, dt_bias, d_skip, norm_g, t_valid=None):
    bsz, t, _ = c.shape
    nc = t // CHUNK
    w = SSM_DI
    return pl.pallas_call(
        functools.partial(_ssd_kernel, nc=nc, t_valid=t_valid),
        grid=(bsz, nc),
        in_specs=[pl.BlockSpec((1, CHUNK, w), lambda i, j: (i, j, 3)),
                  pl.BlockSpec((1, CHUNK, w), lambda i, j: (i, j, 4)),
                  pl.BlockSpec((1, CHUNK, w), lambda i, j: (i, j, 5)),
                  pl.BlockSpec((1, CHUNK, w), lambda i, j: (i, j, 1)),
                  pl.BlockSpec((1, CHUNK, SMALL_W), lambda i, j: (i, j, 0)),
                  pl.BlockSpec((1, SSM_HEADS, SSM_HEADDIM, SSM_DSTATE), lambda i, j: (i, 0, 0, 0)),
                  pl.BlockSpec((1, SSM_HEADS), lambda i, j: (0, 0)),
                  pl.BlockSpec((1, SSM_HEADS), lambda i, j: (0, 0)),
                  pl.BlockSpec((1, SSM_HEADS), lambda i, j: (0, 0)),
                  pl.BlockSpec((1, SSM_DI), lambda i, j: (0, 0))],
        out_specs=[pl.BlockSpec((1, CHUNK, SSM_DI), lambda i, j: (i, j, 0)),
                   pl.BlockSpec((1, SSM_HEADS, SSM_HEADDIM, SSM_DSTATE), lambda i, j: (i, 0, 0, 0))],
        out_shape=[jax.ShapeDtypeStruct((bsz, t, SSM_DI), F32),
                   jax.ShapeDtypeStruct((bsz, SSM_HEADS, SSM_HEADDIM, SSM_DSTATE), F32)],
        scratch_shapes=[pltpu.VMEM((SSM_HEADS, SSM_HEADDIM, SSM_DSTATE), F32)],
        compiler_params=_cparams("parallel", "arbitrary"),
        name="ssd",
    )(c, c, c, z, small, h0, a_log.reshape(1, -1), dt_bias.reshape(1, -1), d_skip.reshape(1, -1),
      norm_g.reshape(1, -1))


PAGE = 128
NEG = -1e30


def _rope_tables(pos):
    half = HEAD_DIM // 2
    inv = ROPE_THETA ** (-jnp.arange(half, dtype=F32) * 2.0 / HEAD_DIM)
    ang = pos.astype(F32)[:, None] * inv[None, :]
    cos, sin = jnp.cos(ang), jnp.sin(ang)
    return jnp.tile(cos, (1, 4)), jnp.tile(jnp.concatenate([-sin, sin], axis=1), (1, 2))


def _nsaproj_kernel(q_ref, kv_ref, small_ref, cos_ref, sin_ref, qn_ref, bd_ref,
                    qo_ref, rows_ref, win_ref, gates_ref, ks_ref, vs_ref, kw_ref, vw_ref):
    cosf = cos_ref[...]
    sins = sin_ref[...]
    bd = bd_ref[...]
    first_half = (_iota((1, LANE), 1) % HEAD_DIM) < (HEAD_DIM // 2)

    def normrope(x, g):
        y = x * lax.rsqrt(_dot_hi(x * x, bd) + EPS) * g
        rot = jnp.where(first_half, pltpu.roll(y, LANE - HEAD_DIM // 2, 1), pltpu.roll(y, HEAD_DIM // 2, 1))
        return y * cosf + rot * sins

    qr = q_ref[0]
    qo_ref[0] = jnp.concatenate(
        [normrope(qr[:, i * LANE:(i + 1) * LANE], qn_ref[0:1, :]) * (HEAD_DIM ** -0.5) for i in range(NSA_Q // LANE)],
        axis=-1)
    kv = kv_ref[0]
    kc = normrope(kv[:, 0:LANE], qn_ref[1:2, :])
    vc = kv[:, LANE:2 * LANE]
    ks = normrope(kv[:, 2 * LANE:3 * LANE], qn_ref[2:3, :])
    vs = kv[:, 3 * LANE:4 * LANE]
    kw = normrope(kv[:, 4 * LANE:5 * LANE], qn_ref[3:4, :])
    vw = kv[:, 5 * LANE:6 * LANE]
    rows_ref[0] = jnp.concatenate([kc, vc, ks, vs], axis=-1)
    win_ref[0] = jnp.concatenate([kw, vw], axis=-1)
    gates_ref[0] = _sigmoid(small_ref[0])
    ones = jnp.ones((ks.shape[0], LANE - HEAD_DIM), BF16)
    for h in range(NSA_KV_HEADS):
        sl = slice(h * HEAD_DIM, (h + 1) * HEAD_DIM)
        ks_ref[0, h] = ks[:, sl].astype(BF16)
        vs_ref[0, h] = jnp.concatenate([vs[:, sl].astype(BF16), ones], axis=-1)
        kw_ref[0, h] = kw[:, sl].astype(BF16)
        vw_ref[0, h] = jnp.concatenate([vw[:, sl].astype(BF16), ones], axis=-1)


def _nsaproj(q_raw, kv_raw, small, pos, qk_norm):
    bsz, t, _ = q_raw.shape
    tm = min(t, 512)
    cos, sin = _rope_tables(pos)
    qn = jnp.tile(qk_norm, (1, 2))
    bd = jnp.kron(jnp.eye(2, dtype=F32), jnp.full((HEAD_DIM, HEAD_DIM), 1.0 / HEAD_DIM, F32))
    tok = lambda w: pl.BlockSpec((1, tm, w), lambda i, j: (i, j, 0))
    return pl.pallas_call(
        _nsaproj_kernel,
        grid=(bsz, t // tm),
        in_specs=[tok(NSA_Q), tok(6 * NSA_KV), tok(SMALL_W),
                  pl.BlockSpec((tm, LANE), lambda i, j: (j, 0)),
                  pl.BlockSpec((tm, LANE), lambda i, j: (j, 0)),
                  pl.BlockSpec((4, LANE), lambda i, j: (0, 0)),
                  pl.BlockSpec((LANE, LANE), lambda i, j: (0, 0))],
        out_specs=[tok(NSA_Q), tok(4 * NSA_KV), tok(2 * NSA_KV), tok(SMALL_W)]
                  + [pl.BlockSpec((1, NSA_KV_HEADS, tm, w), lambda i, j: (i, 0, j, 0))
                     for w in (HEAD_DIM, LANE, HEAD_DIM, LANE)],
        out_shape=[jax.ShapeDtypeStruct((bsz, t, NSA_Q), F32),
                   jax.ShapeDtypeStruct((bsz, t, 4 * NSA_KV), F32),
                   jax.ShapeDtypeStruct((bsz, t, 2 * NSA_KV), F32),
                   jax.ShapeDtypeStruct((bsz, t, SMALL_W), F32)]
                  + [jax.ShapeDtypeStruct((bsz, NSA_KV_HEADS, t, w), BF16) for w in (HEAD_DIM, LANE, HEAD_DIM, LANE)],
        compiler_params=_cparams("parallel", "parallel"),
        name="nsaproj",
    )(q_raw, kv_raw, small, cos, sin, qn, bd)


CMP_PAGES = 16


def _compress_kernel(pt_ref, *refs):
    del pt_ref
    page_refs = refs[:CMP_PAGES]
    pe_ref, phi_ref, ck_ref, cv_ref, bufk, bufv = refs[CMP_PAGES:]
    for i, r in enumerate(page_refs):
        bufk[i * PAGE:(i + 1) * PAGE, :] = r[0, :, 0:LANE]
        bufv[i * PAGE:(i + 1) * PAGE, :] = r[0, :, LANE:2 * LANE]
    nblk = CMP_PAGES * PAGE // NSA_BLOCK

    def body(l, carry):
        ak, av = carry
        kl = bufk[pl.ds(l, nblk, stride=NSA_BLOCK), :] + pe_ref[0, pl.ds(l, 1), :]
        vl = bufv[pl.ds(l, nblk, stride=NSA_BLOCK), :] + pe_ref[1, pl.ds(l, 1), :]
        return ak + _dot_hi(kl, phi_ref[0, l]), av + _dot_hi(vl, phi_ref[1, l])

    zero = jnp.zeros((nblk, LANE), F32)
    ak, av = lax.fori_loop(0, NSA_BLOCK, body, (zero, zero))
    ck_ref[0] = ak
    cv_ref[0] = av


def _compress(pages, page_table, pe, phi):
    bsz, npages = page_table.shape
    steps = npages // CMP_PAGES
    nblk = CMP_PAGES * PAGE // NSA_BLOCK
    pe2 = jnp.tile(pe, (1, 1, 2))
    eye2 = jnp.eye(2, dtype=F32)
    phi_bd = jnp.einsum('ab,klde->kladbe', eye2, phi).reshape(2, NSA_BLOCK, LANE, LANE)
    page_specs = [pl.BlockSpec((1, PAGE, 2 * LANE), lambda b, j, pt, i=i: (pt[b, j * CMP_PAGES + i], 0, 0))
                  for i in range(CMP_PAGES)]
    return pl.pallas_call(
        _compress_kernel,
        grid_spec=pltpu.PrefetchScalarGridSpec(
            num_scalar_prefetch=1,
            grid=(bsz, steps),
            in_specs=page_specs + [pl.BlockSpec((2, NSA_BLOCK, LANE), lambda b, j, pt: (0, 0, 0)),
                                   pl.BlockSpec((2, NSA_BLOCK, LANE, LANE), lambda b, j, pt: (0, 0, 0, 0))],
            out_specs=[pl.BlockSpec((1, nblk, LANE), lambda b, j, pt: (b, j, 0)),
                       pl.BlockSpec((1, nblk, LANE), lambda b, j, pt: (b, j, 0))],
            scratch_shapes=[pltpu.VMEM((CMP_PAGES * PAGE, LANE), F32)] * 2),
        out_shape=[jax.ShapeDtypeStruct((bsz, steps * nblk, LANE), F32)] * 2,
        compiler_params=_cparams("parallel", "parallel"),
        name="compress",
    )(page_table, *([pages] * CMP_PAGES), pe2, phi_bd)


QBLK = 256
KTILE = 512


def _colmax(x):
    return jnp.max(x, axis=0, keepdims=True)


def _colsum(x):
    return jnp.sum(x, axis=0, keepdims=True)


def _tile4(x):
    return jnp.concatenate([x] * NSA_GROUP, axis=1)


def _select_blocks(imp, q0, nb):
    nio = _iota(imp.shape, 0)
    nio_f = nio.astype(F32)
    cur = (q0 + _iota((1, imp.shape[1]), 1)) // NSA_BLOCK
    forced = (nio == 0) | (nio == cur) | (nio == cur - 1)
    causal = nio <= cur
    work = jnp.where(forced, jnp.inf, jnp.where(causal, imp, -jnp.inf))
    sel = jnp.zeros(imp.shape, F32)
    for _ in range(min(NSA_TOPK, nb)):
        m = _colmax(work)
        idx = jnp.min(jnp.where(work == m, nio_f, float(nb)), axis=0, keepdims=True)
        pick = nio_f == idx
        sel = jnp.where(pick, 1.0, sel)
        work = jnp.where(pick, -jnp.inf, work)
    return jnp.where(causal, sel, 0.0)


def _nsa_prompt_kernel(q_ref, ck_ref, cv_ref, ks_ref, vs_ref, kw_ref, vw_ref, gt_ref, wb_ref, o_ref, sel_sc, *,
                       t_total):
    nb = t_total // NSA_BLOCK
    kt = min(KTILE, t_total)
    wk = min(NSA_WINDOW + QBLK, t_total)
    bpt = kt // NSA_BLOCK
    qi = pl.program_id(1)
    q0 = qi * QBLK
    qall = q_ref[0]
    gt = gt_ref[0]
    qpos = _tile4(q0 + _iota((1, QBLK), 1))
    eye = (_iota((HEAD_DIM, HEAD_DIM), 0) == _iota((HEAD_DIM, HEAD_DIM), 1)).astype(BF16)
    heads = range(NSA_KV_HEADS)
    qsb, oc = [], []
    for h in heads:
        qs = jnp.concatenate([qall[:, (h * NSA_GROUP + g) * HEAD_DIM:(h * NSA_GROUP + g + 1) * HEAD_DIM]
                              for g in range(NSA_GROUP)], axis=0)
        qsb.append(qs.astype(BF16))
        sl = slice(h * HEAD_DIM, (h + 1) * HEAD_DIM)
        s = _dot_nt(ck_ref[0][:, sl], qs, HI)
        mask = ((_iota((nb, 1), 0) + 1) * NSA_BLOCK - 1) <= qpos
        m = _colmax(jnp.where(mask, s, NEG))
        p = jnp.where(mask, jnp.exp(s - m), 0.0)
        p = p / jnp.maximum(_colsum(p), 1e-30)
        oc.append(_dot_tn(cv_ref[0][:, sl], p, HI))
        imp = p[:, 0:QBLK]
        for g in range(1, NSA_GROUP):
            imp = imp + p[:, g * QBLK:(g + 1) * QBLK]
        sel_sc[h] = (_select_blocks(imp, q0, nb) - 1.0) * (-NEG)

    def body(i, carry, diagonal):
        k0 = pl.multiple_of(i * kt, kt)
        sc = []
        for h in heads:
            brows = sel_sc[h, pl.ds(pl.multiple_of(i * bpt, bpt), bpt), :]
            bias = jnp.concatenate([jnp.broadcast_to(brows[j:j + 1, :], (NSA_BLOCK, QBLK)) for j in range(bpt)],
                                   axis=0)
            s_h = _dot_nt(ks_ref[0, h, pl.ds(k0, kt), :], qsb[h]) + _tile4(bias)
            if diagonal:
                s_h = jnp.where((k0 + _iota((kt, 1), 0)) <= qpos, s_h, NEG)
            sc.append(s_h)
        m_n = [jnp.maximum(carry[h][0], _colmax(sc[h])) for h in heads]
        pp = [jnp.exp((sc[h] - m_n[h]).astype(BF16)) for h in heads]
        pv = [_dot_tn(vs_ref[0, h, pl.ds(k0, kt), :], pp[h]) for h in heads]
        return tuple((m_n[h], carry[h][1] * jnp.exp(carry[h][0] - m_n[h]) + pv[h]) for h in heads)

    ntile = (q0 + QBLK + kt - 1) // kt
    init = (jnp.full((1, NSA_GROUP * QBLK), NEG, F32), jnp.zeros((LANE, NSA_GROUP * QBLK), F32))
    carry = lax.fori_loop(0, ntile - 1, functools.partial(body, diagonal=False), (init,) * NSA_KV_HEADS)
    carry = body(ntile - 1, carry, True)
    w0 = pl.multiple_of(jnp.maximum(q0 + QBLK - wk, 0), QBLK)
    wbias = _tile4(wb_ref[pl.ds(pl.multiple_of(wk - QBLK - (q0 - w0), QBLK), wk), :])
    outs = []
    for h in heads:
        acc_s = carry[h][1]
        osel = acc_s[0:HEAD_DIM] / jnp.maximum(acc_s[HEAD_DIM:HEAD_DIM + 1], 1e-30)
        sw = _dot_nt(kw_ref[0, h, pl.ds(w0, wk), :], qsb[h]) + wbias
        pw = jnp.exp((sw - _colmax(sw)).astype(BF16))
        acc_w = _dot_tn(vw_ref[0, h, pl.ds(w0, wk), :], pw)
        ow = acc_w[0:HEAD_DIM] / jnp.maximum(acc_w[HEAD_DIM:HEAD_DIM + 1], 1e-30)
        gate = [jnp.concatenate([gt[(h * NSA_GROUP + g) * 3 + j:(h * NSA_GROUP + g) * 3 + j + 1, :]
                                 for g in range(NSA_GROUP)], axis=1) for j in range(3)]
        ot = gate[0] * oc[h] + gate[1] * osel + gate[2] * ow
        for g in range(NSA_GROUP):
            outs.append(_dot_tn(ot[:, g * QBLK:(g + 1) * QBLK].astype(BF16), eye))
    o_ref[0] = jnp.concatenate(outs, axis=-1)


def _nsa_prompt(q, ck, cv, ks, vs, kw, vw, gates_t):
    bsz, t, _ = q.shape
    nb = t // NSA_BLOCK
    full = lambda w: pl.BlockSpec((1, NSA_KV_HEADS, t, w), lambda i, j: (i, 0, 0, 0))
    wk = min(NSA_WINDOW + QBLK, t)
    omax = wk - QBLK
    delta = omax + _iota((wk + omax, QBLK), 1) - _iota((wk + omax, QBLK), 0)
    wbias = jnp.where((delta >= 0) & (delta < NSA_WINDOW), 0.0, NEG).astype(F32)
    return pl.pallas_call(
        functools.partial(_nsa_prompt_kernel, t_total=t),
        grid=(bsz, t // QBLK),
        in_specs=[pl.BlockSpec((1, QBLK, NSA_Q), lambda i, j: (i, j, 0)),
                  pl.BlockSpec((1, nb, LANE), lambda i, j: (i, 0, 0)),
                  pl.BlockSpec((1, nb, LANE), lambda i, j: (i, 0, 0)),
                  full(HEAD_DIM), full(LANE), full(HEAD_DIM), full(LANE),
                  pl.BlockSpec((1, 3 * NSA_HEADS, QBLK), lambda i, j: (i, 0, j)),
                  pl.BlockSpec((wk + omax, QBLK), lambda i, j: (0, 0))],
        out_specs=pl.BlockSpec((1, QBLK, NSA_Q), lambda i, j: (i, j, 0)),
        out_shape=jax.ShapeDtypeStruct((bsz, t, NSA_Q), F32),
        scratch_shapes=[pltpu.VMEM((NSA_KV_HEADS, nb, QBLK), F32)],
        compiler_params=_cparams("parallel", "arbitrary"),
        name="nsa_prompt",
    )(q, ck, cv, ks, vs, kw, vw, gates_t, wbias)


def _group_queries(q, h):
    return jnp.concatenate([q[:, (h * NSA_GROUP + g) * HEAD_DIM:(h * NSA_GROUP + g + 1) * HEAD_DIM]
                            for g in range(NSA_GROUP)] + [jnp.zeros((8 - NSA_GROUP, HEAD_DIM), F32)], axis=0)


def _dot_x3(a, w_hi, w_lo):
    a_hi, a_lo = _split_bf16(a)
    return _dot(a_hi, w_hi) + _dot(a_lo, w_hi) + _dot(a_hi, w_lo)


def _compress_t_kernel(pt_ref, *refs, npg):
    del pt_ref
    page_refs = refs[:npg]
    pe_ref, whi_ref, wlo_ref, ck_ref, cv_ref, buf = refs[npg:]
    rows = 2 * LANE
    for i, r in enumerate(page_refs):
        buf[i * rows:(i + 1) * rows, :] = r[0]

    def body(d, carry):
        accs = list(carry)
        for kv in range(2):
            a = jnp.concatenate([buf[pl.ds(kv * LANE + h * HEAD_DIM + d, npg, stride=rows), :]
                                 for h in range(NSA_KV_HEADS)], axis=0) + pe_ref[kv, pl.ds(d, 1), :]
            if kv == 0:
                accs[kv] = accs[kv] + _dot_x3(a, whi_ref[kv, d], wlo_ref[kv, d])
            else:
                accs[kv] = accs[kv] + _dot(a.astype(BF16), whi_ref[kv, d])
        return tuple(accs)

    zero = jnp.zeros((NSA_KV_HEADS * npg, LANE), F32)
    ak, av = lax.fori_loop(0, HEAD_DIM, body, (zero, zero))
    ck_ref[0] = ak
    cv_ref[0] = av


def _compress_t(cache_t, page_table, pe, phi):
    bsz, npg = page_table.shape
    eye2 = jnp.eye(2, dtype=F32)
    w = jnp.einsum('ab,klde->kdalbe', eye2, phi).reshape(2, HEAD_DIM, LANE, LANE)
    w_hi = w.astype(BF16)
    w_lo = (w - w_hi.astype(F32)).astype(BF16)
    pe_t = jnp.tile(jnp.transpose(pe, (0, 2, 1)), (1, 1, 2))
    page_specs = [pl.BlockSpec((1, 2 * LANE, PAGE), lambda b, pt, i=i: (pt[b, i], 0, 0)) for i in range(npg)]
    const = lambda shape: pl.BlockSpec(shape, lambda b, pt: (0,) * len(shape))
    out = pl.BlockSpec((1, NSA_KV_HEADS * npg, LANE), lambda b, pt: (b, 0, 0))
    return pl.pallas_call(
        functools.partial(_compress_t_kernel, npg=npg),
        grid_spec=pltpu.PrefetchScalarGridSpec(
            num_scalar_prefetch=1,
            grid=(bsz,),
            in_specs=page_specs + [const((2, HEAD_DIM, LANE)), const((2, HEAD_DIM, LANE, LANE)),
                                   const((2, HEAD_DIM, LANE, LANE))],
            out_specs=[out, out],
            scratch_shapes=[pltpu.VMEM((npg * 2 * LANE, PAGE), F32)]),
        out_shape=[jax.ShapeDtypeStruct((bsz, NSA_KV_HEADS * npg, LANE), F32)] * 2,
        compiler_params=_cparams("parallel"),
        name="compress_t",
    )(page_table, *([cache_t] * npg), pe_t, w_hi, w_lo)


def _nsa_sample_cmp_kernel(q_ref, ck_ref, cv_ref, oc_ref, idx_ref, *, past, npg):
    nbc = 2 * npg
    cur = past // NSA_BLOCK
    q = q_ref[0]
    lane = _iota((1, nbc), 1)
    blk = 2 * (lane % npg) + lane // npg
    mask = ((blk + 1) * NSA_BLOCK - 1) <= past
    ocs, idxs = [], []
    for h in range(NSA_KV_HEADS):
        qh = _group_queries(q, h)
        ckh = ck_ref[0][h * npg:(h + 1) * npg, :]
        cvh = cv_ref[0][h * npg:(h + 1) * npg, :]
        s = jnp.concatenate([_dot_nt(qh, ckh[:, i * HEAD_DIM:(i + 1) * HEAD_DIM], HI) for i in range(2)], axis=1)
        m = jnp.max(jnp.where(mask, s, NEG), axis=1, keepdims=True)
        p = jnp.where(mask, jnp.exp(s - m), 0.0)
        p = p / jnp.maximum(jnp.sum(p, axis=1, keepdims=True), 1e-30)
        oc = _dot_hi(p[:, 0:npg], cvh[:, 0:HEAD_DIM]) + _dot_hi(p[:, npg:nbc], cvh[:, HEAD_DIM:LANE])
        ocs.append(oc[0:NSA_GROUP])
        imp = p[0:1]
        for g in range(1, NSA_GROUP):
            imp = imp + p[g:g + 1]
        forced = (blk == 0) | (blk == cur) | (blk == cur - 1)
        score = jnp.where(forced, jnp.inf, jnp.where(blk <= cur, imp, -jnp.inf))
        avail = jnp.ones((1, nbc), jnp.bool_)
        slot = _iota((1, NSA_TOPK), 1)
        picked = jnp.full((1, NSA_TOPK), cur, jnp.int32)
        for r in range(NSA_TOPK - 1):
            mm = jnp.max(jnp.where(avail, score, -jnp.inf), axis=1, keepdims=True)
            eq = avail & (score == mm)
            idx = jnp.min(jnp.where(eq, blk, nbc), axis=1, keepdims=True)
            avail = avail & (blk != idx)
            picked = jnp.where(slot == r, idx, picked)
        idxs.append(picked)
    oc_ref[0] = jnp.concatenate(ocs, axis=0)
    idx_ref[0] = jnp.concatenate(idxs, axis=1)


def _nsa_sample_cmp(q3, ck, cv, past):
    bsz = q3.shape[0]
    npg = ck.shape[1] // NSA_KV_HEADS
    return pl.pallas_call(
        functools.partial(_nsa_sample_cmp_kernel, past=past, npg=npg),
        grid=(bsz,),
        in_specs=[pl.BlockSpec((1, 1, NSA_Q), lambda b: (b, 0, 0)),
                  pl.BlockSpec((1, NSA_KV_HEADS * npg, LANE), lambda b: (b, 0, 0)),
                  pl.BlockSpec((1, NSA_KV_HEADS * npg, LANE), lambda b: (b, 0, 0))],
        out_specs=[pl.BlockSpec((1, NSA_HEADS, HEAD_DIM), lambda b: (b, 0, 0)),
                   pl.BlockSpec((1, 1, NSA_KV_HEADS * NSA_TOPK), lambda b: (b, 0, 0))],
        out_shape=[jax.ShapeDtypeStruct((bsz, NSA_HEADS, HEAD_DIM), F32),
                   jax.ShapeDtypeStruct((bsz, 1, NSA_KV_HEADS * NSA_TOPK), jnp.int32)],
        compiler_params=_cparams("parallel"),
        name="nsa_sample_cmp",
    )(q3, ck, cv)


def _nsa_sample_attn_kernel(page_ref, half_ref, *refs, past):
    del page_ref
    nsel = NSA_KV_HEADS * NSA_TOPK
    blk_refs = refs[:nsel]
    q_ref, row_ref, nwin_ref, wc_ref, g_ref, oc_ref, o_ref, win_ref = refs[nsel:]
    b = pl.program_id(0)
    wbl = wc_ref.shape[2]
    q = q_ref[0]
    row = row_ref[0]
    nwin = nwin_ref[0]
    gates = g_ref[0]
    oc = oc_ref[0]
    half_of_lane = _iota((1, PAGE), 1) // NSA_BLOCK
    outs = []
    for h in range(NSA_KV_HEADS):
        sl = slice(h * HEAD_DIM, (h + 1) * HEAD_DIM)
        vsl = slice(LANE + h * HEAD_DIM, LANE + (h + 1) * HEAD_DIM)
        qh = _group_queries(q, h)
        ks_new = row[:, 2 * LANE + h * HEAD_DIM:2 * LANE + (h + 1) * HEAD_DIM]
        vs_new = row[:, 3 * LANE + h * HEAD_DIM:3 * LANE + (h + 1) * HEAD_DIM]
        s_new = jnp.sum(qh * ks_new, axis=1, keepdims=True)
        ss, ms = [], []
        for k in range(NSA_TOPK - 1):
            ss.append(_dot_hi(qh, blk_refs[h * NSA_TOPK + k][0, sl, :]))
            ms.append(half_of_lane == half_ref[b, h * NSA_TOPK + k])
        m = s_new
        for s, mk in zip(ss, ms):
            m = jnp.maximum(m, jnp.max(jnp.where(mk, s, NEG), axis=1, keepdims=True))
        p_new = jnp.exp(s_new - m)
        den = p_new
        acc = p_new * vs_new
        for k, (s, mk) in enumerate(zip(ss, ms)):
            p = jnp.where(mk, jnp.exp(s - m), 0.0)
            den = den + jnp.sum(p, axis=1, keepdims=True)
            acc = acc + _dot_nt(p, blk_refs[h * NSA_TOPK + k][0, vsl, :], HI)
        osel = acc / jnp.maximum(den, 1e-30)
        sw = _dot_hi(qh, wc_ref[0, sl, :])
        delta = wbl - _iota((1, wbl), 1)
        mw = (delta >= 0) & (delta < NSA_WINDOW) & ((past - delta) >= 0)
        kw_new = nwin[:, sl]
        vw_new = nwin[:, vsl]
        sw_new = jnp.sum(qh * kw_new, axis=1, keepdims=True)
        m = jnp.maximum(jnp.max(jnp.where(mw, sw, NEG), axis=1, keepdims=True), sw_new)
        pw = jnp.where(mw, jnp.exp(sw - m), 0.0)
        pw_new = jnp.exp(sw_new - m)
        ow = (_dot_nt(pw, wc_ref[0, vsl, :], HI) + pw_new * vw_new) / jnp.maximum(
            jnp.sum(pw, axis=1, keepdims=True) + pw_new, 1e-30)
        for g in range(NSA_GROUP):
            c0 = 12 + (h * NSA_GROUP + g) * 3
            outs.append(gates[:, c0:c0 + 1] * oc[h * NSA_GROUP + g:h * NSA_GROUP + g + 1, :]
                        + gates[:, c0 + 1:c0 + 2] * osel[g:g + 1, :] + gates[:, c0 + 2:c0 + 3] * ow[g:g + 1, :])
    o_ref[0] = jnp.concatenate(outs, axis=1)
    new_col = _dot_sel(jnp.ones((8, wbl), F32), jnp.concatenate([nwin, jnp.zeros((7, 2 * NSA_KV), F32)], axis=0),
                       ((0,), (0,)), sel_right=True)
    win_ref[0] = jnp.where(_iota((1, wbl), 1) == wbl - 1, new_col, pltpu.roll(wc_ref[0], wbl - 1, 1))


def _nsa_sample_attn(cache_t, page_table, idx, q3, row3, nwin3, win_t, win_off, gates3, oc, past):
    bsz = q3.shape[0]
    wbl = win_t.shape[2]
    ncomplete = past // NSA_BLOCK

    n = jnp.clip(idx, 0, ncomplete - 1)
    page_of = jnp.take_along_axis(page_table, n // 2, axis=1)
    half_of = n % 2

    def blk_map(b, pg, hf, j):
        return (pg[b, j], 1, 0)

    blk_specs = [pl.BlockSpec((1, 2 * LANE, PAGE), functools.partial(blk_map, j=j))
                 for j in range(NSA_KV_HEADS * NSA_TOPK)]
    one = lambda w: pl.BlockSpec((1, 1, w), lambda b, pt, ix: (b, 0, 0))
    win_in = pl.BlockSpec((1, 2 * NSA_KV, wbl), lambda b, pt, ix: (b + win_off, 0, 0))
    win_out = pl.BlockSpec((1, 2 * NSA_KV, wbl), lambda b, pt, ix: (b, 0, 0))
    return pl.pallas_call(
        functools.partial(_nsa_sample_attn_kernel, past=past),
        grid_spec=pltpu.PrefetchScalarGridSpec(
            num_scalar_prefetch=2,
            grid=(bsz,),
            in_specs=blk_specs + [one(NSA_Q), one(4 * NSA_KV), one(2 * NSA_KV), win_in, one(SMALL_W),
                                  pl.BlockSpec((1, NSA_HEADS, HEAD_DIM), lambda b, pt, ix: (b, 0, 0))],
            out_specs=[one(NSA_Q), win_out]),
        out_shape=[jax.ShapeDtypeStruct((bsz, 1, NSA_Q), F32),
                   jax.ShapeDtypeStruct((bsz, 2 * NSA_KV, wbl), F32)],
        compiler_params=_cparams("parallel"),
        name="nsa_sample_attn",
    )(page_of, half_of, *([cache_t] * (NSA_KV_HEADS * NSA_TOPK)), q3, row3, nwin3, win_t, gates3, oc)


def _outproj_kernel(og_ref, on_ref, os_ref, x_ref, wo_ref, g_ref, wq_ref, x1_ref, xn_ref, qh_ref):
    x1 = (x_ref[...] + _dot(og_ref[...].astype(BF16), wo_ref[0:GDN_V, :])
          + _dot(on_ref[...].astype(BF16), wo_ref[GDN_V:GDN_V + NSA_Q, :])
          + _dot(os_ref[...].astype(BF16), wo_ref[GDN_V + NSA_Q:, :]))
    x1_ref[...] = x1
    xn = (x1 * lax.rsqrt(jnp.mean(x1 * x1, axis=-1, keepdims=True) + EPS) * g_ref[...]).astype(BF16)
    xn_ref[...] = xn
    qh_ref[...] = _dot(xn, wq_ref[...])


def _outproj(og, on, os_, x, w_out, norm_g, wq):
    n = x.shape[0]
    tm = min(n, 512)
    dq = PEER_HEADS * PEER_DKEY
    tok = lambda w: pl.BlockSpec((tm, w), lambda i: (i, 0))
    const = lambda a, b: pl.BlockSpec((a, b), lambda i: (0, 0))
    return pl.pallas_call(
        _outproj_kernel,
        grid=(n // tm,),
        in_specs=[tok(GDN_V), tok(NSA_Q), tok(SSM_DI), tok(D_MODEL), const(D_MODEL, D_MODEL), const(1, D_MODEL),
                  const(D_MODEL, dq)],
        out_specs=[tok(D_MODEL), tok(D_MODEL), tok(dq)],
        out_shape=[jax.ShapeDtypeStruct((n, D_MODEL), F32), jax.ShapeDtypeStruct((n, D_MODEL), BF16),
                   jax.ShapeDtypeStruct((n, dq), F32)],
        compiler_params=_cparams("parallel"),
        name="outproj",
    )(og, on, os_, x, w_out.astype(BF16), norm_g.reshape(1, D_MODEL), wq.astype(BF16))


PEER_PAIRS = tuple((a, b) for a in range(PEER_TOPK) for b in range(PEER_TOPK) if (a + 1) * (b + 1) <= PEER_TOPK)


def _topk_rows(s, k):
    n = s.shape[0]
    nio = _iota(s.shape, 0).astype(F32)
    work = s
    rank = jnp.full(s.shape, float(k), F32)
    vals = []
    for r in range(k):
        m = _colmax(work)
        idx = jnp.min(jnp.where(work == m, nio, float(n)), axis=0, keepdims=True)
        pick = nio == idx
        rank = jnp.where(pick, float(r), rank)
        work = jnp.where(pick, -jnp.inf, work)
        vals.append(m)
    return rank, vals


def _dup_bf16_bits(x):
    bits = lax.shift_right_logical(pltpu.bitcast(x.astype(BF16).astype(F32), jnp.uint32), jnp.uint32(16))
    return pltpu.bitcast(bits | lax.shift_left(bits, jnp.uint32(16)), jnp.int32)


def _peer_keys_kernel(qh_ref, sk_ref, r2_ref, e2_ref, lx_ref, e1_ref):
    half = PEER_DKEY // 2
    for h in range(PEER_HEADS):
        q1 = qh_ref[:, h * PEER_DKEY:h * PEER_DKEY + half]
        q2 = qh_ref[:, h * PEER_DKEY + half:(h + 1) * PEER_DKEY]
        s1 = _dot_nt(sk_ref[0], q1, HI)
        s2 = _dot_nt(sk_ref[1], q2, HI)
        rank1, v1 = _topk_rows(s1, PEER_TOPK)
        rank2, v2 = _topk_rows(s2, PEER_TOPK)
        npad = -len(PEER_PAIRS) % 8
        cand = jnp.concatenate([v1[a] + v2[b] for a, b in PEER_PAIRS]
                               + [jnp.full((npad, s1.shape[1]), -jnp.inf, F32)], axis=0)
        crank, _ = _topk_rows(cand, PEER_TOPK)
        chosen = crank < PEER_TOPK
        z = _colsum(jnp.where(chosen, jnp.exp(cand - cand[0:1]), 0.0))
        chosen_f = jnp.where(chosen, 1.0, 0.0)
        lx = jnp.zeros(s1.shape, F32)
        for a in range(PEER_TOPK):
            rows = [i for i, (pa, _) in enumerate(PEER_PAIRS) if pa == a]
            cnt = chosen_f[rows[0]:rows[0] + 1]
            for i in rows[1:]:
                cnt = cnt + chosen_f[i:i + 1]
            lx = jnp.where(rank1 == float(a), cnt, lx)
        r2_ref[h] = rank2.astype(BF16)
        e2_ref[h] = (jnp.exp(s2 - v2[0]) / z).astype(BF16)
        lx_ref[h] = _dup_bf16_bits(lx)
        e1_ref[h] = _dup_bf16_bits(jnp.where(rank1 < PEER_TOPK, jnp.exp(s1 - v1[0]), 0.0))


def _peer_keys(qh, subkeys):
    n = qh.shape[0]
    tk = min(n, 256)
    dq = PEER_HEADS * PEER_DKEY
    half = PEER_DKEY // 2
    out = pl.BlockSpec((PEER_HEADS, PEER_KEYS, tk), lambda i: (0, 0, i))
    return pl.pallas_call(
        _peer_keys_kernel,
        grid=(n // tk,),
        in_specs=[pl.BlockSpec((tk, dq), lambda i: (i, 0)),
                  pl.BlockSpec((2, PEER_KEYS, half), lambda i: (0, 0, 0))],
        out_specs=[out] * 4,
        out_shape=[jax.ShapeDtypeStruct((PEER_HEADS, PEER_KEYS, n), dt) for dt in (BF16, BF16, jnp.int32, jnp.int32)],
        compiler_params=_cparams("parallel"),
        name="peer_keys",
    )(qh, subkeys)


PEER_AC = 8


def _gelu_tanh(x):
    c = math.sqrt(2.0 / math.pi)
    return x * (0.5 + 0.5 * jnp.tanh(x * (c + (c * 0.044715) * (x * x))))


def _row_tile_bf16(rows, a):
    t = pltpu.bitcast(jnp.broadcast_to(rows[a:a + 1, :], rows.shape), BF16)
    return jnp.concatenate([t] * (PEER_KEYS // t.shape[0]), axis=0)


PEER_SUBS = (4, 4)


def _peer_kernel(xn_ref, u_ref, vt_ref, r2_ref, e2_ref, lx_ref, e1_ref, x1_ref, o_ref, acc, *, nsteps):
    j = pl.program_id(1)

    @pl.when(j == 0)
    def _():
        acc[...] = jnp.zeros_like(acc)

    xn = xn_ref[...]
    assert sum(PEER_SUBS) == PEER_AC
    nsb = len(PEER_SUBS)
    first = [sum(PEER_SUBS[:i]) for i in range(nsb + 1)]

    def hidden(sb):
        return _dot_nt(u_ref[first[sb] * PEER_KEYS:first[sb + 1] * PEER_KEYS, :], xn)

    assert PEER_AC == 8
    base = pl.multiple_of(j * PEER_AC, PEER_AC)
    lx_rows = [lx_ref[h, pl.ds(base, PEER_AC), :] for h in range(PEER_HEADS)]
    e1_rows = [e1_ref[h, pl.ds(base, PEER_AC), :] for h in range(PEER_HEADS)]

    def gates(sb):
        out = []
        for a in range(first[sb], first[sb + 1]):
            gate = None
            for h in range(PEER_HEADS):
                term = jnp.where(r2_ref[h] < _row_tile_bf16(lx_rows[h], a), e2_ref[h],
                                 jnp.zeros((), BF16)) * _row_tile_bf16(e1_rows[h], a)
                gate = term if gate is None else gate + term
            out.append(gate)
        return out

    part = None
    ht_next = hidden(0)
    for sb in range(nsb):
        ht = ht_next
        gs = gates(sb)
        if sb + 1 < nsb:
            ht_next = hidden(sb + 1)
        ws = [gs[a] * _gelu_tanh(ht[a * PEER_KEYS:(a + 1) * PEER_KEYS, :].astype(BF16))
              for a in range(PEER_SUBS[sb])]
        w = ws[0] if len(ws) == 1 else jnp.concatenate(ws, axis=0)
        t = _dot(vt_ref[:, first[sb] * PEER_KEYS:first[sb + 1] * PEER_KEYS], w)
        part = t if part is None else t + part
    acc[...] += part

    @pl.when(j == nsteps - 1)
    def _():
        o_ref[...] = x1_ref[...] + acc[...].T


def _peer(xn, u_bf, vt_bf, r2, e2, lx, e1, x1):
    n = xn.shape[0]
    tm = min(n, 512)
    ec = PEER_AC * PEER_KEYS
    nsteps = PEER_KEYS // PEER_AC
    tab = pl.BlockSpec((PEER_HEADS, PEER_KEYS, tm), lambda i, j: (0, 0, i))
    return pl.pallas_call(
        functools.partial(_peer_kernel, nsteps=nsteps),
        grid=(n // tm, nsteps),
        in_specs=[pl.BlockSpec((tm, D_MODEL), lambda i, j: (i, 0)),
                  pl.BlockSpec((ec, D_MODEL), lambda i, j: (j, 0)),
                  pl.BlockSpec((D_MODEL, ec), lambda i, j: (0, j)),
                  tab, tab, tab, tab,
                  pl.BlockSpec((tm, D_MODEL), lambda i, j: (i, 0))],
        out_specs=pl.BlockSpec((tm, D_MODEL), lambda i, j: (i, 0)),
        out_shape=jax.ShapeDtypeStruct((n, D_MODEL), F32),
        scratch_shapes=[pltpu.VMEM((D_MODEL, tm), F32)],
        compiler_params=_cparams("parallel", "arbitrary"),
        name="peer",
    )(xn, u_bf, vt_bf, r2, e2, lx, e1, x1)


def _pad_axis(a, axis, size):
    if a.shape[axis] == size:
        return a
    pad = [(0, 0)] * a.ndim
    pad[axis] = (0, size - a.shape[axis])
    return jnp.pad(a, pad)


def _mixers(x2d, bsz, t, pos, conv_hist, gdn0, ssm0, lp, tokens_as_time=False):
    conv_in, z, q, kv, small = _inproj(x2d, lp["norm_mix"], lp["w_cat"])
    r3 = lambda a: a.reshape(bsz, t, -1)
    c, new_conv = _conv(r3(conv_in), conv_hist, lp["conv_w"], lp["conv_b"])
    tpad = -(-t // CHUNK) * CHUNK
    t_valid = None if tpad == t else t
    cp, zp, sp = (_pad_axis(a, 1, tpad) for a in (c, r3(z), r3(small)))
    og, s_gdn = _gdn(cp, zp, sp, gdn0, lp["gdn_a_log"], lp["gdn_dt_bias"], lp["gdn_norm"], t_valid=t_valid)
    os_, s_ssm = _ssd(cp, zp, sp, ssm0, lp["ssm_a_log"], lp["ssm_dt_bias"], lp["ssm_d"], lp["ssm_norm"],
                      t_valid=t_valid)
    lead = (1, bsz * t) if tokens_as_time else (bsz, t)
    qn, rows, win, gates, ks, vs, kw, vw = _nsaproj(q.reshape(*lead, -1), kv.reshape(*lead, -1),
                                                    small.reshape(*lead, -1), pos, lp["nsa_qk_norm"])
    return dict(og=og[:, :t], os=os_[:, :t], new_conv=new_conv, s_gdn=s_gdn, s_ssm=s_ssm, q=r3(qn), rows=r3(rows),
                win=r3(win), gates=r3(gates), ks=ks, vs=vs, kw=kw, vw=vw)


def _channel_mix(og, on, os_, x, lp):
    x1, xn, qh = _outproj(og, on, os_, x, lp["w_out"], lp["norm_ffn"], lp["peer_wq"])
    n = x.shape[0]
    npad = -(-n // LANE) * LANE
    x1, xn, qh = (_pad_axis(a, 0, npad) for a in (x1, xn, qh))
    r2, e2, lx, e1 = _peer_keys(qh, lp["peer_subkeys"])
    return _peer(xn, lp["peer_u"], lp["peer_vt"], r2, e2, lx, e1, x1)[:n]


def kernel(x_prompt, x_sample, cache_nsa_kv, cache_nsa_win, state_conv, state_gdn, state_ssm, page_table, norm_mix, w_in, conv_w, conv_b, gdn_a_log, gdn_dt_bias, gdn_norm, ssm_a_log, ssm_dt_bias, ssm_d, ssm_norm, nsa_qk_norm, nsa_pe, nsa_phi, w_out, norm_ffn, peer_wq, peer_subkeys, peer_u, peer_v):
    bp, tp, _ = x_prompt.shape
    bs, ts, _ = x_sample.shape
    past = page_table.shape[1] * cache_nsa_kv.shape[2]
    n_pool = cache_nsa_kv.shape[1]
    yp = x_prompt.reshape(bp * tp, D_MODEL)
    ys = x_sample.reshape(bs * ts, D_MODEL)
    depth = norm_mix.shape[0]
    wbl = cache_nsa_win.shape[2]
    cache_t = jnp.transpose(cache_nsa_kv, (0, 1, 3, 4, 5, 2)).reshape(depth * n_pool, 4 * NSA_KV, PAGE)
    win_t = jnp.transpose(cache_nsa_win, (0, 1, 3, 4, 5, 2)).reshape(depth * bs, 2 * NSA_KV, wbl)
    outs_p, outs_s = [], []
    for l in range(norm_mix.shape[0]):
        lp = dict(norm_mix=norm_mix[l], w_cat=_cat_w_in(w_in[l]), conv_w=conv_w[l], conv_b=conv_b[l],
                  gdn_a_log=gdn_a_log[l], gdn_dt_bias=gdn_dt_bias[l], gdn_norm=gdn_norm[l],
                  ssm_a_log=ssm_a_log[l], ssm_dt_bias=ssm_dt_bias[l], ssm_d=ssm_d[l], ssm_norm=ssm_norm[l],
                  nsa_qk_norm=nsa_qk_norm[l], nsa_pe=nsa_pe[l], nsa_phi=nsa_phi[l], w_out=w_out[l],
                  norm_ffn=norm_ffn[l], peer_wq=peer_wq[l], peer_subkeys=peer_subkeys[l],
                  peer_u=peer_u[l].astype(BF16), peer_vt=peer_v[l].T.astype(BF16))
        m = _mixers(yp, bp, tp, jnp.arange(tp, dtype=jnp.int32), jnp.zeros((bp, CONV_W - 1, CONV_CH), F32),
                    jnp.zeros((bp, GDN_HEADS, GDN_DK, GDN_DV), F32),
                    jnp.zeros((bp, SSM_HEADS, SSM_HEADDIM, SSM_DSTATE), F32), lp)
        pages = m["rows"].reshape(bp * tp // PAGE, PAGE, 4 * NSA_KV)
        ident = jnp.arange(bp * tp // PAGE, dtype=jnp.int32).reshape(bp, tp // PAGE)
        ck, cv = _compress(pages, ident, lp["nsa_pe"], lp["nsa_phi"])
        gates_t = jnp.transpose(m["gates"][..., 12:12 + 3 * NSA_HEADS], (0, 2, 1))
        o_nsa = _nsa_prompt(m["q"], ck, cv, m["ks"], m["vs"], m["kw"], m["vw"], gates_t)
        yp = _channel_mix(m["og"].reshape(bp * tp, -1), o_nsa.reshape(bp * tp, -1), m["os"].reshape(bp * tp, -1),
                          yp, lp)
        keep = min(NSA_WINDOW, tp)
        outs_p.append((m["rows"].reshape(bp, tp, 4, NSA_KV_HEADS, HEAD_DIM),
                       m["win"][:, tp - keep:].reshape(bp, keep, 2, NSA_KV_HEADS, HEAD_DIM),
                       m["new_conv"], m["s_gdn"], m["s_ssm"]))
        m = _mixers(ys, bs, ts, jnp.full((bs * ts,), past, jnp.int32), state_conv[l], state_gdn[l], state_ssm[l], lp,
                    tokens_as_time=True)
        pages_l = page_table + l * n_pool
        ck, cv = _compress_t(cache_t, pages_l, lp["nsa_pe"], lp["nsa_phi"])
        q3 = m["q"].reshape(bs, 1, NSA_Q)
        oc, idx = _nsa_sample_cmp(q3, ck, cv, past)
        o_nsa, new_win_t = _nsa_sample_attn(
            cache_t, pages_l, idx.reshape(bs, NSA_KV_HEADS * NSA_TOPK), q3, m["rows"].reshape(bs, 1, 4 * NSA_KV),
            m["win"].reshape(bs, 1, 2 * NSA_KV), win_t, l * bs, m["gates"].reshape(bs, 1, SMALL_W), oc, past)
        new_win = jnp.transpose(new_win_t.reshape(bs, 2, NSA_KV_HEADS, HEAD_DIM, wbl), (0, 4, 1, 2, 3))
        ys = _channel_mix(m["og"].reshape(bs, -1), o_nsa.reshape(bs, -1), m["os"].reshape(bs, -1), ys, lp)
        outs_s.append((m["rows"].reshape(bs, ts, 4, NSA_KV_HEADS, HEAD_DIM),
                       new_win, m["new_conv"], m["s_gdn"], m["s_ssm"]))
    stack = lambda outs, i: jnp.stack([o[i] for o in outs])
    return (yp.reshape(bp, tp, D_MODEL), ys.reshape(bs, ts, D_MODEL),
            stack(outs_p, 0), stack(outs_p, 1), stack(outs_p, 2), stack(outs_p, 3), stack(outs_p, 4),
            stack(outs_s, 0), stack(outs_s, 1), stack(outs_s, 2), stack(outs_s, 3), stack(outs_s, 4))
```

```python
import functools
import math

import jax
import jax.numpy as jnp
from jax import lax
from jax.experimental import pallas as pl
from jax.experimental.pallas import tpu as pltpu

F32 = jnp.float32
BF16 = jnp.bfloat16
HI = lax.Precision.HIGHEST

D_MODEL = 1024
HEAD_DIM = 64
GDN_HEADS = 4
GDN_DK = 64
GDN_DV = 64
CHUNK = 64
NSA_HEADS = 8
NSA_KV_HEADS = 2
NSA_GROUP = 4
NSA_BLOCK = 64
NSA_TOPK = 16
NSA_WINDOW = 512
SSM_HEADS = 4
SSM_HEADDIM = 64
SSM_GROUPS = 2
SSM_DSTATE = 128
CONV_W = 4
PEER_KEYS = 128
PEER_HEADS = 8
PEER_TOPK = 16
PEER_DKEY = 256
ROPE_THETA = 10000.0
EPS = 1e-6

GDN_QK = GDN_HEADS * GDN_DK
GDN_V = GDN_HEADS * GDN_DV
SSM_DI = SSM_HEADS * SSM_HEADDIM
SSM_BC = SSM_GROUPS * SSM_DSTATE
GDN_CONV_CH = 2 * GDN_QK + GDN_V
CONV_CH = GDN_CONV_CH + SSM_DI + 2 * SSM_BC
NSA_Q = NSA_HEADS * HEAD_DIM
NSA_KV = NSA_KV_HEADS * HEAD_DIM
SMALL_W = 128
LANE = 128
VMEM_LIMIT = 48 * 1024 * 1024


def _cparams(*sem):
    return pltpu.CompilerParams(dimension_semantics=sem, vmem_limit_bytes=VMEM_LIMIT)


def _dot(a, b):
    return jnp.dot(a, b, preferred_element_type=F32)


def _split_bf16(a):
    hi = a.astype(BF16)
    return hi, (a - hi.astype(F32)).astype(BF16)


def _dot3_general(a, b, dims):
    a_hi, a_lo = _split_bf16(a)
    b_hi, b_lo = _split_bf16(b)
    dg = lambda x, y: lax.dot_general(x, y, (dims, ((), ())), preferred_element_type=F32)
    return dg(a_hi, b_hi) + dg(a_lo, b_hi) + dg(a_hi, b_lo)


def _dot_hi(a, b):
    return _dot3_general(a, b, ((1,), (0,)))


def _split3(a):
    hi = a.astype(BF16)
    r = a - hi.astype(F32)
    mid = r.astype(BF16)
    return hi, mid, (r - mid.astype(F32)).astype(BF16)


def _dot_sel(sel, x, dims=((1,), (0,)), sel_right=False):
    out = None
    for piece in _split3(x):
        ops = (piece, sel.astype(BF16)) if sel_right else (sel.astype(BF16), piece)
        t = lax.dot_general(*ops, (dims, ((), ())), preferred_element_type=F32)
        out = t if out is None else out + t
    return out


def _dot_nt(a, b, precision=None):
    if precision is not None:
        return _dot3_general(a, b, ((1,), (1,)))
    return lax.dot_general(a, b, (((1,), (1,)), ((), ())), preferred_element_type=F32)


def _dot_tn(a, b, precision=None):
    if precision is not None:
        return _dot3_general(a, b, ((0,), (0,)))
    return lax.dot_general(a, b, (((0,), (0,)), ((), ())), preferred_element_type=F32)


def _sigmoid(x):
    return 1.0 / (1.0 + jnp.exp(-x))


def _softplus(x):
    return jnp.maximum(x, 0.0) + jnp.log1p(jnp.exp(-jnp.abs(x)))


def _iota(shape, dim):
    return lax.broadcasted_iota(jnp.int32, shape, dim)


IN_GROUPS = (("conv", CONV_CH), ("z", GDN_V + SSM_DI), ("q", NSA_Q), ("kv", 6 * NSA_KV), ("small", SMALL_W))
IN_CAT = sum(w for _, w in IN_GROUPS)


def _inproj_kernel(x_ref, g_ref, w_ref, conv_ref, z_ref, q_ref, kv_ref, small_ref):
    x = x_ref[...]
    ms = jnp.mean(x * x, axis=-1, keepdims=True)
    xn = (x * lax.rsqrt(ms + EPS) * g_ref[...]).astype(BF16)
    off = 0
    for ref, (_, w) in zip((conv_ref, z_ref, q_ref, kv_ref, small_ref), IN_GROUPS):
        ref[...] = _dot(xn, w_ref[:, off:off + w])
        off += w


def _inproj(x2d, norm_g, w_cat):
    n = x2d.shape[0]
    tm = min(n, 512)
    return pl.pallas_call(
        _inproj_kernel,
        grid=(n // tm,),
        in_specs=[pl.BlockSpec((tm, D_MODEL), lambda i: (i, 0)),
                  pl.BlockSpec((1, D_MODEL), lambda i: (0, 0)),
                  pl.BlockSpec((D_MODEL, IN_CAT), lambda i: (0, 0))],
        out_specs=[pl.BlockSpec((tm, w), lambda i: (i, 0)) for _, w in IN_GROUPS],
        out_shape=[jax.ShapeDtypeStruct((n, w), F32) for _, w in IN_GROUPS],
        compiler_params=_cparams("parallel"),
        name="inproj",
    )(x2d, norm_g.reshape(1, D_MODEL), w_cat)


def _cat_w_in(w_in):
    o = 0
    parts = {}
    for name, w in (("conv", CONV_CH), ("gdn_z", GDN_V), ("gdn_b", GDN_HEADS), ("gdn_a", GDN_HEADS),
                    ("ssm_z", SSM_DI), ("ssm_dt", SSM_HEADS), ("nsa_q", NSA_Q), ("nsa_kv", 6 * NSA_KV),
                    ("nsa_g", 3 * NSA_HEADS)):
        parts[name] = w_in[:, o:o + w]
        o += w
    small = jnp.concatenate([parts["gdn_b"], parts["gdn_a"], parts["ssm_dt"], parts["nsa_g"]], axis=1)
    small = jnp.pad(small, ((0, 0), (0, SMALL_W - small.shape[1])))
    return jnp.concatenate([parts["conv"], parts["gdn_z"], parts["ssm_z"], parts["nsa_q"], parts["nsa_kv"], small],
                           axis=1).astype(BF16)


HALO = 8


def _conv_kernel(x_ref, hist_ref, w_ref, b_ref, c_ref, nc_ref, xe, *, tm, nt):
    t = pl.program_id(1)

    @pl.when(t == 0)
    def _():
        xe[HALO - 3:HALO, :] = hist_ref[0]

    xe[HALO:HALO + tm, :] = x_ref[0]
    y = b_ref[...] + xe[HALO - 3:HALO - 3 + tm, :] * w_ref[0:1, :]
    for j in range(1, CONV_W):
        y = y + xe[HALO - 3 + j:HALO - 3 + j + tm, :] * w_ref[j:j + 1, :]
    c_ref[0] = y * _sigmoid(y)
    last = xe[HALO + tm - 3:HALO + tm, :]

    @pl.when(t == nt - 1)
    def _():
        nc_ref[0] = last

    xe[HALO - 3:HALO, :] = last


def _conv(x, hist, w, b):
    bsz, t, ch = x.shape
    tm = min(t, 512)
    nt = t // tm
    return pl.pallas_call(
        functools.partial(_conv_kernel, tm=tm, nt=nt),
        grid=(bsz, nt),
        in_specs=[pl.BlockSpec((1, tm, ch), lambda i, j: (i, j, 0)),
                  pl.BlockSpec((1, CONV_W - 1, ch), lambda i, j: (i, 0, 0)),
                  pl.BlockSpec((CONV_W, ch), lambda i, j: (0, 0)),
                  pl.BlockSpec((1, ch), lambda i, j: (0, 0))],
        out_specs=[pl.BlockSpec((1, tm, ch), lambda i, j: (i, j, 0)),
                   pl.BlockSpec((1, CONV_W - 1, ch), lambda i, j: (i, 0, 0))],
        out_shape=[jax.ShapeDtypeStruct((bsz, t, ch), F32),
                   jax.ShapeDtypeStruct((bsz, CONV_W - 1, ch), F32)],
        scratch_shapes=[pltpu.VMEM((HALO + tm, ch), F32)],
        compiler_params=_cparams("parallel", "arbitrary"),
        name="conv",
    )(x, hist, w, b.reshape(1, ch))


def _tri(n, strict=False):
    r, c = _iota((n, n), 0), _iota((n, n), 1)
    return (r > c) if strict else (r >= c)


def _row_of(col):
    n = col.shape[0]
    eye = (_iota((n, n), 0) == _iota((n, n), 1)).astype(F32)
    return _dot_sel(jnp.ones((n, n), F32), eye * col)


def _gdn_kernel(q_ref, k_ref, v_ref, z_ref, small_ref, s0_ref, alog_ref, dtb_ref, ng_ref, o_ref, s_ref, st,
                *, nc, t_valid, bb):
    c = pl.program_id(1)

    @pl.when(c == 0)
    def _():
        st[...] = s0_ref[...]

    incl = _tri(CHUNK)
    strict = _tri(CHUNK, strict=True)
    if t_valid is not None:
        valid = (c * CHUNK + _iota((CHUNK, 1), 0)) < t_valid
    chains = [(b, h) for b in range(bb) for h in range(GDN_HEADS)]
    beta_all, gc_all = {}, {}
    for b in range(bb):
        sm = small_ref[b]
        bt = _sigmoid(sm[:, 0:GDN_HEADS])
        g_all = -jnp.exp(alog_ref[...]) * _softplus(sm[:, GDN_HEADS:2 * GDN_HEADS] + dtb_ref[...])
        if t_valid is not None:
            bt = jnp.where(valid, bt, 0.0)
            g_all = jnp.where(valid, g_all, 0.0)
        beta_all[b] = bt
        gc_all[b] = _dot_sel(incl.astype(F32), g_all)
    q, k, v, beta, gc = {}, {}, {}, {}, {}
    for c_ in chains:
        b, h = c_
        sl = slice(h * GDN_DK, (h + 1) * GDN_DK)
        qq = q_ref[b][:, sl]
        kk = k_ref[b][:, sl]
        vv = v_ref[b][:, sl]
        qq = qq * lax.rsqrt(jnp.sum(qq * qq, axis=-1, keepdims=True) + EPS) * (GDN_DK ** -0.5)
        kk = kk * lax.rsqrt(jnp.sum(kk * kk, axis=-1, keepdims=True) + EPS)
        if t_valid is not None:
            qq = jnp.where(valid, qq, 0.0)
            kk = jnp.where(valid, kk, 0.0)
            vv = jnp.where(valid, vv, 0.0)
        q[c_], k[c_], v[c_] = qq, kk, vv
        beta[c_] = beta_all[b][:, h:h + 1]
        gc[c_] = gc_all[b][:, h:h + 1]
    diff = {c_: gc[c_] - _row_of(gc[c_]) for c_ in chains}
    kkt = {c_: _dot_nt(k[c_], k[c_], HI) for c_ in chains}
    qkt = {c_: _dot_nt(q[c_], k[c_], HI) for c_ in chains}
    egc = {c_: jnp.exp(gc[c_]) for c_ in chains}
    p = {c_: -(beta[c_] * kkt[c_] * jnp.where(strict, jnp.exp(jnp.where(strict, diff[c_], 0.0)), 0.0))
         for c_ in chains}
    x = {c_: jnp.concatenate([v[c_] * beta[c_], k[c_] * (beta[c_] * egc[c_])], axis=-1) for c_ in chains}
    for it in range(6):
        px = {c_: _dot_hi(p[c_], x[c_]) for c_ in chains}
        if it < 5:
            p = {c_: _dot_hi(p[c_], p[c_]) for c_ in chains}
        x = {c_: x[c_] + px[c_] for c_ in chains}
    aqk = {c_: qkt[c_] * jnp.where(incl, jnp.exp(jnp.where(incl, diff[c_], 0.0)), 0.0) for c_ in chains}
    s = {c_: st[c_[0], c_[1]] for c_ in chains}
    wks = {c_: _dot_hi(x[c_][:, GDN_DV:], s[c_]) for c_ in chains}
    qs = {c_: _dot_hi(q[c_] * egc[c_], s[c_]) for c_ in chains}
    w = {c_: x[c_][:, :GDN_DV] - wks[c_] for c_ in chains}
    gl = {c_: gc[c_][CHUNK - 1:CHUNK, :] for c_ in chains}
    kdw = {c_: _dot_tn(k[c_] * jnp.exp(gl[c_] - gc[c_]), w[c_], HI) for c_ in chains}
    aw = {c_: _dot_hi(aqk[c_], w[c_]) for c_ in chains}
    for c_ in chains:
        st[c_[0], c_[1]] = jnp.exp(gl[c_]) * s[c_] + kdw[c_]
    for b in range(bb):
        outs = []
        for h in range(GDN_HEADS):
            o = qs[(b, h)] + aw[(b, h)]
            o = o * lax.rsqrt(jnp.mean(o * o, axis=-1, keepdims=True) + EPS) * ng_ref[...]
            zh = z_ref[b][:, h * GDN_DK:(h + 1) * GDN_DK]
            outs.append(o * (zh * _sigmoid(zh)))
        o_ref[b] = jnp.concatenate(outs, axis=-1)

    @pl.when(c == nc - 1)
    def _():
        s_ref[...] = st[...]


GDN_BB = 2


def _gdn(c, z, small, s0, a_log, dt_bias, norm_g, t_valid=None):
    bsz, t, _ = c.shape
    nc = t // CHUNK
    bb = GDN_BB
    return pl.pallas_call(
        functools.partial(_gdn_kernel, nc=nc, t_valid=t_valid, bb=bb),
        grid=(bsz // bb, nc),
        in_specs=[pl.BlockSpec((bb, CHUNK, GDN_QK), lambda i, j: (i, j, 0)),
                  pl.BlockSpec((bb, CHUNK, GDN_QK), lambda i, j: (i, j, 1)),
                  pl.BlockSpec((bb, CHUNK, GDN_V), lambda i, j: (i, j, 2)),
                  pl.BlockSpec((bb, CHUNK, GDN_V), lambda i, j: (i, j, 0)),
                  pl.BlockSpec((bb, CHUNK, SMALL_W), lambda i, j: (i, j, 0)),
                  pl.BlockSpec((bb, GDN_HEADS, GDN_DK, GDN_DV), lambda i, j: (i, 0, 0, 0)),
                  pl.BlockSpec((1, GDN_HEADS), lambda i, j: (0, 0)),
                  pl.BlockSpec((1, GDN_HEADS), lambda i, j: (0, 0)),
                  pl.BlockSpec((1, GDN_DV), lambda i, j: (0, 0))],
        out_specs=[pl.BlockSpec((bb, CHUNK, GDN_V), lambda i, j: (i, j, 0)),
                   pl.BlockSpec((bb, GDN_HEADS, GDN_DK, GDN_DV), lambda i, j: (i, 0, 0, 0))],
        out_shape=[jax.ShapeDtypeStruct((bsz, t, GDN_V), F32),
                   jax.ShapeDtypeStruct((bsz, GDN_HEADS, GDN_DK, GDN_DV), F32)],
        scratch_shapes=[pltpu.VMEM((bb, GDN_HEADS, GDN_DK, GDN_DV), F32)],
        compiler_params=_cparams("parallel", "arbitrary"),
        name="gdn",
    )(c, c, c, z, small, s0, a_log.reshape(1, -1), dt_bias.reshape(1, -1), norm_g.reshape(1, -1))


def _ssd_kernel(x_ref, b_ref, c_ref, z_ref, small_ref, h0_ref, alog_ref, dtb_ref, dsk_ref, ng_ref, o_ref, h_ref, st,
                *, nc, t_valid):
    ci = pl.program_id(1)

    @pl.when(ci == 0)
    def _():
        st[...] = h0_ref[0]

    sm = small_ref[0]
    dt_all = _softplus(sm[:, 2 * GDN_HEADS:2 * GDN_HEADS + SSM_HEADS] + dtb_ref[...])
    xs = x_ref[0]
    bm_all = b_ref[0]
    cm_all = c_ref[0]
    if t_valid is not None:
        valid = (ci * CHUNK + _iota((CHUNK, 1), 0)) < t_valid
        dt_all = jnp.where(valid, dt_all, 0.0)
        xs = jnp.where(valid, xs, 0.0)
        bm_all = jnp.where(valid, bm_all, 0.0)
        cm_all = jnp.where(valid, cm_all, 0.0)
    incl = _tri(CHUNK)
    cs_all = _dot_sel(incl.astype(F32), dt_all * (-jnp.exp(alog_ref[...])))
    rep = SSM_HEADS // SSM_GROUPS
    heads = range(SSM_HEADS)
    x = [xs[:, h * SSM_HEADDIM:(h + 1) * SSM_HEADDIM] for h in heads]
    bm = [bm_all[:, (h // rep) * SSM_DSTATE:(h // rep + 1) * SSM_DSTATE] for h in heads]
    cm = [cm_all[:, (h // rep) * SSM_DSTATE:(h // rep + 1) * SSM_DSTATE] for h in heads]
    cs = [cs_all[:, h:h + 1] for h in heads]
    rows = [_row_of(cs[h]) for h in heads]
    cbr = [_dot_nt(cm[g * rep], bm[g * rep], HI) for g in range(SSM_GROUPS)]
    xdt = [x[h] * dt_all[:, h:h + 1] for h in heads]
    cl = [cs[h][CHUNK - 1:CHUNK, :] for h in heads]
    hs = [st[h] for h in heads]
    cb = [cbr[h // rep] * jnp.where(incl, jnp.exp(jnp.where(incl, cs[h] - rows[h], 0.0)), 0.0) for h in heads]
    y_diag = [_dot_hi(cb[h], xdt[h]) for h in heads]
    y_off = [_dot_nt(cm[h] * jnp.exp(cs[h]), hs[h], HI) for h in heads]
    upd = [_dot_tn(xdt[h] * jnp.exp(cl[h] - cs[h]), bm[h], HI) for h in heads]
    for h in heads:
        st[h] = jnp.exp(cl[h]) * hs[h] + upd[h]
    y = jnp.concatenate([y_diag[h] + y_off[h] + dsk_ref[0:1, h:h + 1] * x[h] for h in heads], axis=-1)
    zz = z_ref[0]
    y = y * (zz * _sigmoid(zz))
    gs = SSM_DI // SSM_GROUPS
    outs = []
    for g in range(SSM_GROUPS):
        yg = y[:, g * gs:(g + 1) * gs]
        outs.append(yg * lax.rsqrt(jnp.mean(yg * yg, axis=-1, keepdims=True) + EPS) * ng_ref[0:1, g * gs:(g + 1) * gs])
    o_ref[0] = jnp.concatenate(outs, axis=-1)

    @pl.when(ci == nc - 1)
    def _():
        h_ref[0] = st[...]


def _ssd(c, z, small, h0, a_log, dt_bias, d_skip, norm_g, t_valid=None):
    bsz, t, _ = c.shape
    nc = t // CHUNK
    w = SSM_DI
    return pl.pallas_call(
        functools.partial(_ssd_kernel, nc=nc, t_valid=t_valid),
        grid=(bsz, nc),
        in_specs=[pl.BlockSpec((1, CHUNK, w), lambda i, j: (i, j, 3)),
                  pl.BlockSpec((1, CHUNK, w), lambda i, j: (i, j, 4)),
                  pl.BlockSpec((1, CHUNK, w), lambda i, j: (i, j, 5)),
                  pl.BlockSpec((1, CHUNK, w), lambda i, j: (i, j, 1)),
                  pl.BlockSpec((1, CHUNK, SMALL_W), lambda i, j: (i, j, 0)),
                  pl.BlockSpec((1, SSM_HEADS, SSM_HEADDIM, SSM_DSTATE), lambda i, j: (i, 0, 0, 0)),
                  pl.BlockSpec((1, SSM_HEADS), lambda i, j: (0, 0)),
                  pl.BlockSpec((1, SSM_HEADS), lambda i, j: (0, 0)),
                  pl.BlockSpec((1, SSM_HEADS), lambda i, j: (0, 0)),
                  pl.BlockSpec((1, SSM_DI), lambda i, j: (0, 0))],
        out_specs=[pl.BlockSpec((1, CHUNK, SSM_DI), lambda i, j: (i, j, 0)),
                   pl.BlockSpec((1, SSM_HEADS, SSM_HEADDIM, SSM_DSTATE), lambda i, j: (i, 0, 0, 0))],
        out_shape=[jax.ShapeDtypeStruct((bsz, t, SSM_DI), F32),
                   jax.ShapeDtypeStruct((bsz, SSM_HEADS, SSM_HEADDIM, SSM_DSTATE), F32)],
        scratch_shapes=[pltpu.VMEM((SSM_HEADS, SSM_HEADDIM, SSM_DSTATE), F32)],
        compiler_params=_cparams("parallel", "arbitrary"),
        name="ssd",
    )(c, c, c, z, small, h0, a_log.reshape(1, -1), dt_bias.reshape(1, -1), d_skip.reshape(1, -1),
      norm_g.reshape(1, -1))


PAGE = 128
NEG = -1e30


def _rope_tables(pos):
    half = HEAD_DIM // 2
    inv = ROPE_THETA ** (-jnp.arange(half, dtype=F32) * 2.0 / HEAD_DIM)
    ang = pos.astype(F32)[:, None] * inv[None, :]
    cos, sin = jnp.cos(ang), jnp.sin(ang)
    return jnp.tile(cos, (1, 4)), jnp.tile(jnp.concatenate([-sin, sin], axis=1), (1, 2))


def _nsaproj_kernel(q_ref, kv_ref, small_ref, cos_ref, sin_ref, qn_ref, bd_ref,
                    qo_ref, rows_ref, win_ref, gates_ref, ks_ref, vs_ref, kw_ref, vw_ref):
    cosf = cos_ref[...]
    sins = sin_ref[...]
    bd = bd_ref[...]
    first_half = (_iota((1, LANE), 1) % HEAD_DIM) < (HEAD_DIM // 2)

    def normrope(x, g):
        y = x * lax.rsqrt(_dot_hi(x * x, bd) + EPS) * g
        rot = jnp.where(first_half, pltpu.roll(y, LANE - HEAD_DIM // 2, 1), pltpu.roll(y, HEAD_DIM // 2, 1))
        return y * cosf + rot * sins

    qr = q_ref[0]
    qo_ref[0] = jnp.concatenate(
        [normrope(qr[:, i * LANE:(i + 1) * LANE], qn_ref[0:1, :]) * (HEAD_DIM ** -0.5) for i in range(NSA_Q // LANE)],
        axis=-1)
    kv = kv_ref[0]
    kc = normrope(kv[:, 0:LANE], qn_ref[1:2, :])
    vc = kv[:, LANE:2 * LANE]
    ks = normrope(kv[:, 2 * LANE:3 * LANE], qn_ref[2:3, :])
    vs = kv[:, 3 * LANE:4 * LANE]
    kw = normrope(kv[:, 4 * LANE:5 * LANE], qn_ref[3:4, :])
    vw = kv[:, 5 * LANE:6 * LANE]
    rows_ref[0] = jnp.concatenate([kc, vc, ks, vs], axis=-1)
    win_ref[0] = jnp.concatenate([kw, vw], axis=-1)
    gates_ref[0] = _sigmoid(small_ref[0])
    ones = jnp.ones((ks.shape[0], LANE - HEAD_DIM), BF16)
    for h in range(NSA_KV_HEADS):
        sl = slice(h * HEAD_DIM, (h + 1) * HEAD_DIM)
        ks_ref[0, h] = ks[:, sl].astype(BF16)
        vs_ref[0, h] = jnp.concatenate([vs[:, sl].astype(BF16), ones], axis=-1)
        kw_ref[0, h] = kw[:, sl].astype(BF16)
        vw_ref[0, h] = jnp.concatenate([vw[:, sl].astype(BF16), ones], axis=-1)


def _nsaproj(q_raw, kv_raw, small, pos, qk_norm):
    bsz, t, _ = q_raw.shape
    tm = min(t, 512)
    cos, sin = _rope_tables(pos)
    qn = jnp.tile(qk_norm, (1, 2))
    bd = jnp.kron(jnp.eye(2, dtype=F32), jnp.full((HEAD_DIM, HEAD_DIM), 1.0 / HEAD_DIM, F32))
    tok = lambda w: pl.BlockSpec((1, tm, w), lambda i, j: (i, j, 0))
    return pl.pallas_call(
        _nsaproj_kernel,
        grid=(bsz, t // tm),
        in_specs=[tok(NSA_Q), tok(6 * NSA_KV), tok(SMALL_W),
                  pl.BlockSpec((tm, LANE), lambda i, j: (j, 0)),
                  pl.BlockSpec((tm, LANE), lambda i, j: (j, 0)),
                  pl.BlockSpec((4, LANE), lambda i, j: (0, 0)),
                  pl.BlockSpec((LANE, LANE), lambda i, j: (0, 0))],
        out_specs=[tok(NSA_Q), tok(4 * NSA_KV), tok(2 * NSA_KV), tok(SMALL_W)]
                  + [pl.BlockSpec((1, NSA_KV_HEADS, tm, w), lambda i, j: (i, 0, j, 0))
                     for w in (HEAD_DIM, LANE, HEAD_DIM, LANE)],
        out_shape=[jax.ShapeDtypeStruct((bsz, t, NSA_Q), F32),
                   jax.ShapeDtypeStruct((bsz, t, 4 * NSA_KV), F32),
                   jax.ShapeDtypeStruct((bsz, t, 2 * NSA_KV), F32),
                   jax.ShapeDtypeStruct((bsz, t, SMALL_W), F32)]
                  + [jax.ShapeDtypeStruct((bsz, NSA_KV_HEADS, t, w), BF16) for w in (HEAD_DIM, LANE, HEAD_DIM, LANE)],
        compiler_params=_cparams("parallel", "parallel"),
        name="nsaproj",
    )(q_raw, kv_raw, small, cos, sin, qn, bd)


CMP_PAGES = 16


def _compress_kernel(pt_ref, *refs):
    del pt_ref
    page_refs = refs[:CMP_PAGES]
    pe_ref, phi_ref, ck_ref, cv_ref, bufk, bufv = refs[CMP_PAGES:]
    for i, r in enumerate(page_refs):
        bufk[i * PAGE:(i + 1) * PAGE, :] = r[0, :, 0:LANE]
        bufv[i * PAGE:(i + 1) * PAGE, :] = r[0, :, LANE:2 * LANE]
    nblk = CMP_PAGES * PAGE // NSA_BLOCK

    def body(l, carry):
        ak, av = carry
        kl = bufk[pl.ds(l, nblk, stride=NSA_BLOCK), :] + pe_ref[0, pl.ds(l, 1), :]
        vl = bufv[pl.ds(l, nblk, stride=NSA_BLOCK), :] + pe_ref[1, pl.ds(l, 1), :]
        return ak + _dot_hi(kl, phi_ref[0, l]), av + _dot_hi(vl, phi_ref[1, l])

    zero = jnp.zeros((nblk, LANE), F32)
    ak, av = lax.fori_loop(0, NSA_BLOCK, body, (zero, zero), unroll=4)
    ck_ref[0] = ak
    cv_ref[0] = av


def _compress(pages, page_table, pe, phi):
    bsz, npages = page_table.shape
    steps = npages // CMP_PAGES
    nblk = CMP_PAGES * PAGE // NSA_BLOCK
    pe2 = jnp.tile(pe, (1, 1, 2))
    eye2 = jnp.eye(2, dtype=F32)
    phi_bd = jnp.einsum('ab,klde->kladbe', eye2, phi).reshape(2, NSA_BLOCK, LANE, LANE)
    page_specs = [pl.BlockSpec((1, PAGE, 2 * LANE), lambda b, j, pt, i=i: (pt[b, j * CMP_PAGES + i], 0, 0))
                  for i in range(CMP_PAGES)]
    return pl.pallas_call(
        _compress_kernel,
        grid_spec=pltpu.PrefetchScalarGridSpec(
            num_scalar_prefetch=1,
            grid=(bsz, steps),
            in_specs=page_specs + [pl.BlockSpec((2, NSA_BLOCK, LANE), lambda b, j, pt: (0, 0, 0)),
                                   pl.BlockSpec((2, NSA_BLOCK, LANE, LANE), lambda b, j, pt: (0, 0, 0, 0))],
            out_specs=[pl.BlockSpec((1, nblk, LANE), lambda b, j, pt: (b, j, 0)),
                       pl.BlockSpec((1, nblk, LANE), lambda b, j, pt: (b, j, 0))],
            scratch_shapes=[pltpu.VMEM((CMP_PAGES * PAGE, LANE), F32)] * 2),
        out_shape=[jax.ShapeDtypeStruct((bsz, steps * nblk, LANE), F32)] * 2,
        compiler_params=_cparams("parallel", "parallel"),
        name="compress",
    )(page_table, *([pages] * CMP_PAGES), pe2, phi_bd)


QBLK = 256
KTILE = 512


def _colmax(x):
    return jnp.max(x, axis=0, keepdims=True)


def _colsum(x):
    return jnp.sum(x, axis=0, keepdims=True)


def _tile4(x):
    return jnp.concatenate([x] * NSA_GROUP, axis=1)


def _select_blocks(imp, q0, nb):
    nio = _iota(imp.shape, 0)
    nio_f = nio.astype(F32)
    cur = (q0 + _iota((1, imp.shape[1]), 1)) // NSA_BLOCK
    forced = (nio == 0) | (nio == cur) | (nio == cur - 1)
    causal = nio <= cur
    work = jnp.where(forced, jnp.inf, jnp.where(causal, imp, -jnp.inf))
    sel = jnp.zeros(imp.shape, F32)
    for _ in range(min(NSA_TOPK, nb)):
        m = _colmax(work)
        idx = jnp.min(jnp.where(work == m, nio_f, float(nb)), axis=0, keepdims=True)
        pick = nio_f == idx
        sel = jnp.where(pick, 1.0, sel)
        work = jnp.where(pick, -jnp.inf, work)
    return jnp.where(causal, sel, 0.0)


def _nsa_prompt_kernel(q_ref, ck_ref, cv_ref, ks_ref, vs_ref, kw_ref, vw_ref, gt_ref, wb_ref, o_ref, sel_sc, *,
                       t_total):
    nb = t_total // NSA_BLOCK
    kt = min(KTILE, t_total)
    wk = min(NSA_WINDOW + QBLK, t_total)
    bpt = kt // NSA_BLOCK
    qi = pl.program_id(1)
    q0 = qi * QBLK
    qall = q_ref[0]
    gt = gt_ref[0]
    qpos = _tile4(q0 + _iota((1, QBLK), 1))
    eye = (_iota((HEAD_DIM, HEAD_DIM), 0) == _iota((HEAD_DIM, HEAD_DIM), 1)).astype(BF16)
    heads = range(NSA_KV_HEADS)
    qsb, oc = [], []
    for h in heads:
        qs = jnp.concatenate([qall[:, (h * NSA_GROUP + g) * HEAD_DIM:(h * NSA_GROUP + g + 1) * HEAD_DIM]
                              for g in range(NSA_GROUP)], axis=0)
        qsb.append(qs.astype(BF16))
        sl = slice(h * HEAD_DIM, (h + 1) * HEAD_DIM)
        s = _dot_nt(ck_ref[0][:, sl], qs, HI)
        mask = ((_iota((nb, 1), 0) + 1) * NSA_BLOCK - 1) <= qpos
        m = _colmax(jnp.where(mask, s, NEG))
        p = jnp.where(mask, jnp.exp(s - m), 0.0)
        p = p / jnp.maximum(_colsum(p), 1e-30)
        oc.append(_dot_tn(cv_ref[0][:, sl], p, HI))
        imp = p[:, 0:QBLK]
        for g in range(1, NSA_GROUP):
            imp = imp + p[:, g * QBLK:(g + 1) * QBLK]
        sel_sc[h] = (_select_blocks(imp, q0, nb) - 1.0) * (-NEG)

    def body(i, carry, diagonal):
        k0 = pl.multiple_of(i * kt, kt)
        sc = []
        for h in heads:
            brows = sel_sc[h, pl.ds(pl.multiple_of(i * bpt, bpt), bpt), :]
            bias = jnp.concatenate([jnp.broadcast_to(brows[j:j + 1, :], (NSA_BLOCK, QBLK)) for j in range(bpt)],
                                   axis=0)
            s_h = _dot_nt(ks_ref[0, h, pl.ds(k0, kt), :], qsb[h]) + _tile4(bias)
            if diagonal:
                s_h = jnp.where((k0 + _iota((kt, 1), 0)) <= qpos, s_h, NEG)
            sc.append(s_h)
        m_n = [jnp.maximum(carry[h][0], _colmax(sc[h])) for h in heads]
        pp = [jnp.exp((sc[h] - m_n[h]).astype(BF16)) for h in heads]
        pv = [_dot_tn(vs_ref[0, h, pl.ds(k0, kt), :], pp[h]) for h in heads]
        return tuple((m_n[h], carry[h][1] * jnp.exp(carry[h][0] - m_n[h]) + pv[h]) for h in heads)

    ntile = (q0 + QBLK + kt - 1) // kt
    init = (jnp.full((1, NSA_GROUP * QBLK), NEG, F32), jnp.zeros((LANE, NSA_GROUP * QBLK), F32))
    carry = lax.fori_loop(0, ntile - 1, functools.partial(body, diagonal=False), (init,) * NSA_KV_HEADS)
    carry = body(ntile - 1, carry, True)
    w0 = pl.multiple_of(jnp.maximum(q0 + QBLK - wk, 0), QBLK)
    wbias = _tile4(wb_ref[pl.ds(pl.multiple_of(wk - QBLK - (q0 - w0), QBLK), wk), :])
    outs = []
    for h in heads:
        acc_s = carry[h][1]
        osel = acc_s[0:HEAD_DIM] / jnp.maximum(acc_s[HEAD_DIM:HEAD_DIM + 1], 1e-30)
        sw = _dot_nt(kw_ref[0, h, pl.ds(w0, wk), :], qsb[h]) + wbias
        pw = jnp.exp((sw - _colmax(sw)).astype(BF16))
        acc_w = _dot_tn(vw_ref[0, h, pl.ds(w0, wk), :], pw)
        ow = acc_w[0:HEAD_DIM] / jnp.maximum(acc_w[HEAD_DIM:HEAD_DIM + 1], 1e-30)
        gate = [jnp.concatenate([gt[(h * NSA_GROUP + g) * 3 + j:(h * NSA_GROUP + g) * 3 + j + 1, :]
                                 for g in range(NSA_GROUP)], axis=1) for j in range(3)]
        ot = gate[0] * oc[h] + gate[1] * osel + gate[2] * ow
        for g in range(NSA_GROUP):
            outs.append(_dot_tn(ot[:, g * QBLK:(g + 1) * QBLK].astype(BF16), eye))
    o_ref[0] = jnp.concatenate(outs, axis=-1)


def _nsa_prompt(q, ck, cv, ks, vs, kw, vw, gates_t):
    bsz, t, _ = q.shape
    nb = t // NSA_BLOCK
    full = lambda w: pl.BlockSpec((1, NSA_KV_HEADS, t, w), lambda i, j: (i, 0, 0, 0))
    wk = min(NSA_WINDOW + QBLK, t)
    omax = wk - QBLK
    delta = omax + _iota((wk + omax, QBLK), 1) - _iota((wk + omax, QBLK), 0)
    wbias = jnp.where((delta >= 0) & (delta < NSA_WINDOW), 0.0, NEG).astype(F32)
    return pl.pallas_call(
        functools.partial(_nsa_prompt_kernel, t_total=t),
        grid=(bsz, t // QBLK),
        in_specs=[pl.BlockSpec((1, QBLK, NSA_Q), lambda i, j: (i, j, 0)),
                  pl.BlockSpec((1, nb, LANE), lambda i, j: (i, 0, 0)),
                  pl.BlockSpec((1, nb, LANE), lambda i, j: (i, 0, 0)),
                  full(HEAD_DIM), full(LANE), full(HEAD_DIM), full(LANE),
                  pl.BlockSpec((1, 3 * NSA_HEADS, QBLK), lambda i, j: (i, 0, j)),
                  pl.BlockSpec((wk + omax, QBLK), lambda i, j: (0, 0))],
        out_specs=pl.BlockSpec((1, QBLK, NSA_Q), lambda i, j: (i, j, 0)),
        out_shape=jax.ShapeDtypeStruct((bsz, t, NSA_Q), F32),
        scratch_shapes=[pltpu.VMEM((NSA_KV_HEADS, nb, QBLK), F32)],
        compiler_params=_cparams("parallel", "arbitrary"),
        name="nsa_prompt",
    )(q, ck, cv, ks, vs, kw, vw, gates_t, wbias)


def _group_queries(q, h):
    return jnp.concatenate([q[:, (h * NSA_GROUP + g) * HEAD_DIM:(h * NSA_GROUP + g + 1) * HEAD_DIM]
                            for g in range(NSA_GROUP)] + [jnp.zeros((8 - NSA_GROUP, HEAD_DIM), F32)], axis=0)


def _dot_x3(a, w_hi, w_lo):
    a_hi, a_lo = _split_bf16(a)
    return _dot(a_hi, w_hi) + _dot(a_lo, w_hi) + _dot(a_hi, w_lo)


def _compress_t_kernel(pt_ref, *refs, npg):
    del pt_ref
    page_refs = refs[:npg]
    pe_ref, whi_ref, wlo_ref, ck_ref, cv_ref, buf = refs[npg:]
    rows = 2 * LANE
    for i, r in enumerate(page_refs):
        buf[i * rows:(i + 1) * rows, :] = r[0]

    def body(d, carry):
        accs = list(carry)
        for kv in range(2):
            a = jnp.concatenate([buf[pl.ds(kv * LANE + h * HEAD_DIM + d, npg, stride=rows), :]
                                 for h in range(NSA_KV_HEADS)], axis=0) + pe_ref[kv, pl.ds(d, 1), :]
            if kv == 0:
                accs[kv] = accs[kv] + _dot_x3(a, whi_ref[kv, d], wlo_ref[kv, d])
            else:
                accs[kv] = accs[kv] + _dot(a.astype(BF16), whi_ref[kv, d])
        return tuple(accs)

    zero = jnp.zeros((NSA_KV_HEADS * npg, LANE), F32)
    ak, av = lax.fori_loop(0, HEAD_DIM, body, (zero, zero), unroll=4)
    ck_ref[0] = ak
    cv_ref[0] = av


def _compress_t(cache_t, page_table, pe, phi):
    bsz, npg = page_table.shape
    eye2 = jnp.eye(2, dtype=F32)
    w = jnp.einsum('ab,klde->kdalbe', eye2, phi).reshape(2, HEAD_DIM, LANE, LANE)
    w_hi = w.astype(BF16)
    w_lo = (w - w_hi.astype(F32)).astype(BF16)
    pe_t = jnp.tile(jnp.transpose(pe, (0, 2, 1)), (1, 1, 2))
    page_specs = [pl.BlockSpec((1, 2 * LANE, PAGE), lambda b, pt, i=i: (pt[b, i], 0, 0)) for i in range(npg)]
    const = lambda shape: pl.BlockSpec(shape, lambda b, pt: (0,) * len(shape))
    out = pl.BlockSpec((1, NSA_KV_HEADS * npg, LANE), lambda b, pt: (b, 0, 0))
    return pl.pallas_call(
        functools.partial(_compress_t_kernel, npg=npg),
        grid_spec=pltpu.PrefetchScalarGridSpec(
            num_scalar_prefetch=1,
            grid=(bsz,),
            in_specs=page_specs + [const((2, HEAD_DIM, LANE)), const((2, HEAD_DIM, LANE, LANE)),
                                   const((2, HEAD_DIM, LANE, LANE))],
            out_specs=[out, out],
            scratch_shapes=[pltpu.VMEM((npg * 2 * LANE, PAGE), F32)]),
        out_shape=[jax.ShapeDtypeStruct((bsz, NSA_KV_HEADS * npg, LANE), F32)] * 2,
        compiler_params=_cparams("parallel"),
        name="compress_t",
    )(page_table, *([cache_t] * npg), pe_t, w_hi, w_lo)


def _nsa_sample_cmp_kernel(q_ref, ck_ref, cv_ref, oc_ref, idx_ref, *, past, npg):
    nbc = 2 * npg
    cur = past // NSA_BLOCK
    q = q_ref[0]
    lane = _iota((1, nbc), 1)
    blk = 2 * (lane % npg) + lane // npg
    mask = ((blk + 1) * NSA_BLOCK - 1) <= past
    ocs, idxs = [], []
    for h in range(NSA_KV_HEADS):
        qh = _group_queries(q, h)
        ckh = ck_ref[0][h * npg:(h + 1) * npg, :]
        cvh = cv_ref[0][h * npg:(h + 1) * npg, :]
        s = jnp.concatenate([_dot_nt(qh, ckh[:, i * HEAD_DIM:(i + 1) * HEAD_DIM], HI) for i in range(2)], axis=1)
        m = jnp.max(jnp.where(mask, s, NEG), axis=1, keepdims=True)
        p = jnp.where(mask, jnp.exp(s - m), 0.0)
        p = p / jnp.maximum(jnp.sum(p, axis=1, keepdims=True), 1e-30)
        oc = _dot_hi(p[:, 0:npg], cvh[:, 0:HEAD_DIM]) + _dot_hi(p[:, npg:nbc], cvh[:, HEAD_DIM:LANE])
        ocs.append(oc[0:NSA_GROUP])
        imp = p[0:1]
        for g in range(1, NSA_GROUP):
            imp = imp + p[g:g + 1]
        forced = (blk == 0) | (blk == cur) | (blk == cur - 1)
        score = jnp.where(forced, jnp.inf, jnp.where(blk <= cur, imp, -jnp.inf))
        avail = jnp.ones((1, nbc), jnp.bool_)
        slot = _iota((1, NSA_TOPK), 1)
        picked = jnp.full((1, NSA_TOPK), cur, jnp.int32)
        for r in range(NSA_TOPK - 1):
            mm = jnp.max(jnp.where(avail, score, -jnp.inf), axis=1, keepdims=True)
            eq = avail & (score == mm)
            idx = jnp.min(jnp.where(eq, blk, nbc), axis=1, keepdims=True)
            avail = avail & (blk != idx)
            picked = jnp.where(slot == r, idx, picked)
        idxs.append(picked)
    oc_ref[0] = jnp.concatenate(ocs, axis=0)
    idx_ref[0] = jnp.concatenate(idxs, axis=1)


def _nsa_sample_cmp(q3, ck, cv, past):
    bsz = q3.shape[0]
    npg = ck.shape[1] // NSA_KV_HEADS
    return pl.pallas_call(
        functools.partial(_nsa_sample_cmp_kernel, past=past, npg=npg),
        grid=(bsz,),
        in_specs=[pl.BlockSpec((1, 1, NSA_Q), lambda b: (b, 0, 0)),
                  pl.BlockSpec((1, NSA_KV_HEADS * npg, LANE), lambda b: (b, 0, 0)),
                  pl.BlockSpec((1, NSA_KV_HEADS * npg, LANE), lambda b: (b, 0, 0))],
        out_specs=[pl.BlockSpec((1, NSA_HEADS, HEAD_DIM), lambda b: (b, 0, 0)),
                   pl.BlockSpec((1, 1, NSA_KV_HEADS * NSA_TOPK), lambda b: (b, 0, 0))],
        out_shape=[jax.ShapeDtypeStruct((bsz, NSA_HEADS, HEAD_DIM), F32),
                   jax.ShapeDtypeStruct((bsz, 1, NSA_KV_HEADS * NSA_TOPK), jnp.int32)],
        compiler_params=_cparams("parallel"),
        name="nsa_sample_cmp",
    )(q3, ck, cv)


def _nsa_sample_attn_kernel(page_ref, half_ref, *refs, past):
    del page_ref
    nsel = NSA_KV_HEADS * NSA_TOPK
    blk_refs = refs[:nsel]
    q_ref, row_ref, nwin_ref, wc_ref, g_ref, oc_ref, o_ref, win_ref = refs[nsel:]
    b = pl.program_id(0)
    wbl = wc_ref.shape[2]
    q = q_ref[0]
    row = row_ref[0]
    nwin = nwin_ref[0]
    gates = g_ref[0]
    oc = oc_ref[0]
    half_of_lane = _iota((1, PAGE), 1) // NSA_BLOCK
    outs = []
    for h in range(NSA_KV_HEADS):
        sl = slice(h * HEAD_DIM, (h + 1) * HEAD_DIM)
        vsl = slice(LANE + h * HEAD_DIM, LANE + (h + 1) * HEAD_DIM)
        qh = _group_queries(q, h)
        ks_new = row[:, 2 * LANE + h * HEAD_DIM:2 * LANE + (h + 1) * HEAD_DIM]
        vs_new = row[:, 3 * LANE + h * HEAD_DIM:3 * LANE + (h + 1) * HEAD_DIM]
        s_new = jnp.sum(qh * ks_new, axis=1, keepdims=True)
        ss, ms = [], []
        for k in range(NSA_TOPK - 1):
            ss.append(_dot_hi(qh, blk_refs[h * NSA_TOPK + k][0, sl, :]))
            ms.append(half_of_lane == half_ref[b, h * NSA_TOPK + k])
        m = s_new
        for s, mk in zip(ss, ms):
            m = jnp.maximum(m, jnp.max(jnp.where(mk, s, NEG), axis=1, keepdims=True))
        p_new = jnp.exp(s_new - m)
        den = p_new
        acc = p_new * vs_new
        for k, (s, mk) in enumerate(zip(ss, ms)):
            p = jnp.where(mk, jnp.exp(s - m), 0.0)
            den = den + jnp.sum(p, axis=1, keepdims=True)
            acc = acc + _dot_nt(p, blk_refs[h * NSA_TOPK + k][0, vsl, :], HI)
        osel = acc / jnp.maximum(den, 1e-30)
        sw = _dot_hi(qh, wc_ref[0, sl, :])
        delta = wbl - _iota((1, wbl), 1)
        mw = (delta >= 0) & (delta < NSA_WINDOW) & ((past - delta) >= 0)
        kw_new = nwin[:, sl]
        vw_new = nwin[:, vsl]
        sw_new = jnp.sum(qh * kw_new, axis=1, keepdims=True)
        m = jnp.maximum(jnp.max(jnp.where(mw, sw, NEG), axis=1, keepdims=True), sw_new)
        pw = jnp.where(mw, jnp.exp(sw - m), 0.0)
        pw_new = jnp.exp(sw_new - m)
        ow = (_dot_nt(pw, wc_ref[0, vsl, :], HI) + pw_new * vw_new) / jnp.maximum(
            jnp.sum(pw, axis=1, keepdims=True) + pw_new, 1e-30)
        for g in range(NSA_GROUP):
            c0 = 12 + (h * NSA_GROUP + g) * 3
            outs.append(gates[:, c0:c0 + 1] * oc[h * NSA_GROUP + g:h * NSA_GROUP + g + 1, :]
                        + gates[:, c0 + 1:c0 + 2] * osel[g:g + 1, :] + gates[:, c0 + 2:c0 + 3] * ow[g:g + 1, :])
    o_ref[0] = jnp.concatenate(outs, axis=1)
    new_col = _dot_sel(jnp.ones((8, wbl), F32), jnp.concatenate([nwin, jnp.zeros((7, 2 * NSA_KV), F32)], axis=0),
                       ((0,), (0,)), sel_right=True)
    win_ref[0] = jnp.where(_iota((1, wbl), 1) == wbl - 1, new_col, pltpu.roll(wc_ref[0], wbl - 1, 1))


def _nsa_sample_attn(cache_t, page_table, idx, q3, row3, nwin3, win_t, win_off, gates3, oc, past):
    bsz = q3.shape[0]
    wbl = win_t.shape[2]
    ncomplete = past // NSA_BLOCK

    n = jnp.clip(idx, 0, ncomplete - 1)
    page_of = jnp.take_along_axis(page_table, n // 2, axis=1)
    half_of = n % 2

    def blk_map(b, pg, hf, j):
        return (pg[b, j], 1, 0)

    blk_specs = [pl.BlockSpec((1, 2 * LANE, PAGE), functools.partial(blk_map, j=j))
                 for j in range(NSA_KV_HEADS * NSA_TOPK)]
    one = lambda w: pl.BlockSpec((1, 1, w), lambda b, pt, ix: (b, 0, 0))
    win_in = pl.BlockSpec((1, 2 * NSA_KV, wbl), lambda b, pt, ix: (b + win_off, 0, 0))
    win_out = pl.BlockSpec((1, 2 * NSA_KV, wbl), lambda b, pt, ix: (b, 0, 0))
    return pl.pallas_call(
        functools.partial(_nsa_sample_attn_kernel, past=past),
        grid_spec=pltpu.PrefetchScalarGridSpec(
            num_scalar_prefetch=2,
            grid=(bsz,),
            in_specs=blk_specs + [one(NSA_Q), one(4 * NSA_KV), one(2 * NSA_KV), win_in, one(SMALL_W),
                                  pl.BlockSpec((1, NSA_HEADS, HEAD_DIM), lambda b, pt, ix: (b, 0, 0))],
            out_specs=[one(NSA_Q), win_out]),
        out_shape=[jax.ShapeDtypeStruct((bsz, 1, NSA_Q), F32),
                   jax.ShapeDtypeStruct((bsz, 2 * NSA_KV, wbl), F32)],
        compiler_params=_cparams("parallel"),
        name="nsa_sample_attn",
    )(page_of, half_of, *([cache_t] * (NSA_KV_HEADS * NSA_TOPK)), q3, row3, nwin3, win_t, gates3, oc)


def _outproj_kernel(og_ref, on_ref, os_ref, x_ref, wo_ref, g_ref, wq_ref, x1_ref, xn_ref, qh_ref):
    x1 = (x_ref[...] + _dot(og_ref[...].astype(BF16), wo_ref[0:GDN_V, :])
          + _dot(on_ref[...].astype(BF16), wo_ref[GDN_V:GDN_V + NSA_Q, :])
          + _dot(os_ref[...].astype(BF16), wo_ref[GDN_V + NSA_Q:, :]))
    x1_ref[...] = x1
    xn = (x1 * lax.rsqrt(jnp.mean(x1 * x1, axis=-1, keepdims=True) + EPS) * g_ref[...]).astype(BF16)
    xn_ref[...] = xn
    qh_ref[...] = _dot(xn, wq_ref[...])


def _outproj(og, on, os_, x, w_out, norm_g, wq):
    n = x.shape[0]
    tm = min(n, 512)
    dq = PEER_HEADS * PEER_DKEY
    tok = lambda w: pl.BlockSpec((tm, w), lambda i: (i, 0))
    const = lambda a, b: pl.BlockSpec((a, b), lambda i: (0, 0))
    return pl.pallas_call(
        _outproj_kernel,
        grid=(n // tm,),
        in_specs=[tok(GDN_V), tok(NSA_Q), tok(SSM_DI), tok(D_MODEL), const(D_MODEL, D_MODEL), const(1, D_MODEL),
                  const(D_MODEL, dq)],
        out_specs=[tok(D_MODEL), tok(D_MODEL), tok(dq)],
        out_shape=[jax.ShapeDtypeStruct((n, D_MODEL), F32), jax.ShapeDtypeStruct((n, D_MODEL), BF16),
                   jax.ShapeDtypeStruct((n, dq), F32)],
        compiler_params=_cparams("parallel"),
        name="outproj",
    )(og, on, os_, x, w_out.astype(BF16), norm_g.reshape(1, D_MODEL), wq.astype(BF16))


PEER_PAIRS = tuple((a, b) for a in range(PEER_TOPK) for b in range(PEER_TOPK) if (a + 1) * (b + 1) <= PEER_TOPK)


def _topk_rows(s, k):
    n = s.shape[0]
    nio = _iota(s.shape, 0).astype(F32)
    work = s
    rank = jnp.full(s.shape, float(k), F32)
    vals = []
    for r in range(k):
        m = _colmax(work)
        idx = jnp.min(jnp.where(work == m, nio, float(n)), axis=0, keepdims=True)
        pick = nio == idx
        rank = jnp.where(pick, float(r), rank)
        work = jnp.where(pick, -jnp.inf, work)
        vals.append(m)
    return rank, vals


def _dup_bf16_bits(x):
    bits = lax.shift_right_logical(pltpu.bitcast(x.astype(BF16).astype(F32), jnp.uint32), jnp.uint32(16))
    return pltpu.bitcast(bits | lax.shift_left(bits, jnp.uint32(16)), jnp.int32)


def _peer_keys_kernel(qh_ref, sk_ref, r2_ref, e2_ref, lx_ref, e1_ref):
    half = PEER_DKEY // 2
    for h in range(PEER_HEADS):
        q1 = qh_ref[:, h * PEER_DKEY:h * PEER_DKEY + half]
        q2 = qh_ref[:, h * PEER_DKEY + half:(h + 1) * PEER_DKEY]
        s1 = _dot_nt(sk_ref[0], q1, HI)
        s2 = _dot_nt(sk_ref[1], q2, HI)
        rank1, v1 = _topk_rows(s1, PEER_TOPK)
        rank2, v2 = _topk_rows(s2, PEER_TOPK)
        npad = -len(PEER_PAIRS) % 8
        cand = jnp.concatenate([v1[a] + v2[b] for a, b in PEER_PAIRS]
                               + [jnp.full((npad, s1.shape[1]), -jnp.inf, F32)], axis=0)
        crank, _ = _topk_rows(cand, PEER_TOPK)
        chosen = crank < PEER_TOPK
        z = _colsum(jnp.where(chosen, jnp.exp(cand - cand[0:1]), 0.0))
        chosen_f = jnp.where(chosen, 1.0, 0.0)
        lx = jnp.zeros(s1.shape, F32)
        for a in range(PEER_TOPK):
            rows = [i for i, (pa, _) in enumerate(PEER_PAIRS) if pa == a]
            cnt = chosen_f[rows[0]:rows[0] + 1]
            for i in rows[1:]:
                cnt = cnt + chosen_f[i:i + 1]
            lx = jnp.where(rank1 == float(a), cnt, lx)
        r2_ref[h] = rank2.astype(BF16)
        e2_ref[h] = (jnp.exp(s2 - v2[0]) / z).astype(BF16)
        lx_ref[h] = _dup_bf16_bits(lx)
        e1_ref[h] = _dup_bf16_bits(jnp.where(rank1 < PEER_TOPK, jnp.exp(s1 - v1[0]), 0.0))


def _peer_keys(qh, subkeys):
    n = qh.shape[0]
    tk = min(n, 256)
    dq = PEER_HEADS * PEER_DKEY
    half = PEER_DKEY // 2
    out = pl.BlockSpec((PEER_HEADS, PEER_KEYS, tk), lambda i: (0, 0, i))
    return pl.pallas_call(
        _peer_keys_kernel,
        grid=(n // tk,),
        in_specs=[pl.BlockSpec((tk, dq), lambda i: (i, 0)),
                  pl.BlockSpec((2, PEER_KEYS, half), lambda i: (0, 0, 0))],
        out_specs=[out] * 4,
        out_shape=[jax.ShapeDtypeStruct((PEER_HEADS, PEER_KEYS, n), dt) for dt in (BF16, BF16, jnp.int32, jnp.int32)],
        compiler_params=_cparams("parallel"),
        name="peer_keys",
    )(qh, subkeys)


PEER_AC = 16


def _gelu_tanh(x):
    c = math.sqrt(2.0 / math.pi)
    return x * (0.5 + 0.5 * jnp.tanh(x * (c + (c * 0.044715) * (x * x))))


def _row_tile_bf16(rows, a):
    t = pltpu.bitcast(jnp.broadcast_to(rows[a:a + 1, :], (8, rows.shape[1])), BF16)
    return jnp.concatenate([t] * (PEER_KEYS // t.shape[0]), axis=0)


PEER_SUBS = (4, 4, 4, 4)


def _peer_kernel(xn_ref, u_ref, vt_ref, r2_ref, e2_ref, lx_ref, e1_ref, x1_ref, o_ref, acc, *, nsteps):
    j = pl.program_id(1)

    @pl.when(j == 0)
    def _():
        acc[...] = jnp.zeros_like(acc)

    xn = xn_ref[...]
    assert sum(PEER_SUBS) == PEER_AC
    nsb = len(PEER_SUBS)
    first = [sum(PEER_SUBS[:i]) for i in range(nsb + 1)]

    def hidden(sb):
        return _dot_nt(u_ref[first[sb] * PEER_KEYS:first[sb + 1] * PEER_KEYS, :], xn)

    lx_rows = [lx_ref[h] for h in range(PEER_HEADS)]
    e1_rows = [e1_ref[h] for h in range(PEER_HEADS)]

    def gates(sb):
        out = []
        for a in range(first[sb], first[sb + 1]):
            gate = None
            for h in range(PEER_HEADS):
                term = jnp.where(r2_ref[h] < _row_tile_bf16(lx_rows[h], a), e2_ref[h],
                                 jnp.zeros((), BF16)) * _row_tile_bf16(e1_rows[h], a)
                gate = term if gate is None else gate + term
            out.append(gate)
        return out

    part = None
    ht_next = hidden(0)
    for sb in range(nsb):
        ht = ht_next
        gs = gates(sb)
        if sb + 1 < nsb:
            ht_next = hidden(sb + 1)
        ws = [gs[a] * _gelu_tanh(ht[a * PEER_KEYS:(a + 1) * PEER_KEYS, :].astype(BF16))
              for a in range(PEER_SUBS[sb])]
        w = ws[0] if len(ws) == 1 else jnp.concatenate(ws, axis=0)
        t = _dot(vt_ref[:, first[sb] * PEER_KEYS:first[sb + 1] * PEER_KEYS], w)
        part = t if part is None else t + part
    acc[...] += part

    @pl.when(j == nsteps - 1)
    def _():
        o_ref[...] = x1_ref[...] + acc[...].T


def _peer(xn, u_bf, vt_bf, r2, e2, lx, e1, x1):
    n = xn.shape[0]
    tm = min(n, 512)
    ec = PEER_AC * PEER_KEYS
    nsteps = PEER_KEYS // PEER_AC
    tab = pl.BlockSpec((PEER_HEADS, PEER_KEYS, tm), lambda i, j: (0, 0, i))
    rows = pl.BlockSpec((PEER_HEADS, PEER_AC, tm), lambda i, j: (0, j, i))
    return pl.pallas_call(
        functools.partial(_peer_kernel, nsteps=nsteps),
        grid=(n // tm, nsteps),
        in_specs=[pl.BlockSpec((tm, D_MODEL), lambda i, j: (i, 0)),
                  pl.BlockSpec((ec, D_MODEL), lambda i, j: (j, 0)),
                  pl.BlockSpec((D_MODEL, ec), lambda i, j: (0, j)),
                  tab, tab, rows, rows,
                  pl.BlockSpec((tm, D_MODEL), lambda i, j: (i, 0))],
        out_specs=pl.BlockSpec((tm, D_MODEL), lambda i, j: (i, 0)),
        out_shape=jax.ShapeDtypeStruct((n, D_MODEL), F32),
        scratch_shapes=[pltpu.VMEM((D_MODEL, tm), F32)],
        compiler_params=_cparams("parallel", "arbitrary"),
        name="peer",
    )(xn, u_bf, vt_bf, r2, e2, lx, e1, x1)


def _pad_axis(a, axis, size):
    if a.shape[axis] == size:
        return a
    pad = [(0, 0)] * a.ndim
    pad[axis] = (0, size - a.shape[axis])
    return jnp.pad(a, pad)


def _mixers(x2d, bsz, t, pos, conv_hist, gdn0, ssm0, lp, tokens_as_time=False):
    conv_in, z, q, kv, small = _inproj(x2d, lp["norm_mix"], lp["w_cat"])
    r3 = lambda a: a.reshape(bsz, t, -1)
    c, new_conv = _conv(r3(conv_in), conv_hist, lp["conv_w"], lp["conv_b"])
    tpad = -(-t // CHUNK) * CHUNK
    t_valid = None if tpad == t else t
    cp, zp, sp = (_pad_axis(a, 1, tpad) for a in (c, r3(z), r3(small)))
    og, s_gdn = _gdn(cp, zp, sp, gdn0, lp["gdn_a_log"], lp["gdn_dt_bias"], lp["gdn_norm"], t_valid=t_valid)
    os_, s_ssm = _ssd(cp, zp, sp, ssm0, lp["ssm_a_log"], lp["ssm_dt_bias"], lp["ssm_d"], lp["ssm_norm"],
                      t_valid=t_valid)
    lead = (1, bsz * t) if tokens_as_time else (bsz, t)
    qn, rows, win, gates, ks, vs, kw, vw = _nsaproj(q.reshape(*lead, -1), kv.reshape(*lead, -1),
                                                    small.reshape(*lead, -1), pos, lp["nsa_qk_norm"])
    return dict(og=og[:, :t], os=os_[:, :t], new_conv=new_conv, s_gdn=s_gdn, s_ssm=s_ssm, q=r3(qn), rows=r3(rows),
                win=r3(win), gates=r3(gates), ks=ks, vs=vs, kw=kw, vw=vw)


def _channel_mix(og, on, os_, x, lp):
    x1, xn, qh = _outproj(og, on, os_, x, lp["w_out"], lp["norm_ffn"], lp["peer_wq"])
    n = x.shape[0]
    npad = -(-n // LANE) * LANE
    x1, xn, qh = (_pad_axis(a, 0, npad) for a in (x1, xn, qh))
    r2, e2, lx, e1 = _peer_keys(qh, lp["peer_subkeys"])
    return _peer(xn, lp["peer_u"], lp["peer_vt"], r2, e2, lx, e1, x1)[:n]


def kernel(x_prompt, x_sample, cache_nsa_kv, cache_nsa_win, state_conv, state_gdn, state_ssm, page_table, norm_mix, w_in, conv_w, conv_b, gdn_a_log, gdn_dt_bias, gdn_norm, ssm_a_log, ssm_dt_bias, ssm_d, ssm_norm, nsa_qk_norm, nsa_pe, nsa_phi, w_out, norm_ffn, peer_wq, peer_subkeys, peer_u, peer_v):
    bp, tp, _ = x_prompt.shape
    bs, ts, _ = x_sample.shape
    past = page_table.shape[1] * cache_nsa_kv.shape[2]
    n_pool = cache_nsa_kv.shape[1]
    yp = x_prompt.reshape(bp * tp, D_MODEL)
    ys = x_sample.reshape(bs * ts, D_MODEL)
    depth = norm_mix.shape[0]
    wbl = cache_nsa_win.shape[2]
    cache_t = jnp.transpose(cache_nsa_kv, (0, 1, 3, 4, 5, 2)).reshape(depth * n_pool, 4 * NSA_KV, PAGE)
    win_t = jnp.transpose(cache_nsa_win, (0, 1, 3, 4, 5, 2)).reshape(depth * bs, 2 * NSA_KV, wbl)
    outs_p, outs_s = [], []
    for l in range(norm_mix.shape[0]):
        lp = dict(norm_mix=norm_mix[l], w_cat=_cat_w_in(w_in[l]), conv_w=conv_w[l], conv_b=conv_b[l],
                  gdn_a_log=gdn_a_log[l], gdn_dt_bias=gdn_dt_bias[l], gdn_norm=gdn_norm[l],
                  ssm_a_log=ssm_a_log[l], ssm_dt_bias=ssm_dt_bias[l], ssm_d=ssm_d[l], ssm_norm=ssm_norm[l],
                  nsa_qk_norm=nsa_qk_norm[l], nsa_pe=nsa_pe[l], nsa_phi=nsa_phi[l], w_out=w_out[l],
                  norm_ffn=norm_ffn[l], peer_wq=peer_wq[l], peer_subkeys=peer_subkeys[l],
                  peer_u=peer_u[l].astype(BF16), peer_vt=peer_v[l].T.astype(BF16))
        m = _mixers(yp, bp, tp, jnp.arange(tp, dtype=jnp.int32), jnp.zeros((bp, CONV_W - 1, CONV_CH), F32),
                    jnp.zeros((bp, GDN_HEADS, GDN_DK, GDN_DV), F32),
                    jnp.zeros((bp, SSM_HEADS, SSM_HEADDIM, SSM_DSTATE), F32), lp)
        pages = m["rows"].reshape(bp * tp // PAGE, PAGE, 4 * NSA_KV)
        ident = jnp.arange(bp * tp // PAGE, dtype=jnp.int32).reshape(bp, tp // PAGE)
        ck, cv = _compress(pages, ident, lp["nsa_pe"], lp["nsa_phi"])
        gates_t = jnp.transpose(m["gates"][..., 12:12 + 3 * NSA_HEADS], (0, 2, 1))
        o_nsa = _nsa_prompt(m["q"], ck, cv, m["ks"], m["vs"], m["kw"], m["vw"], gates_t)
        yp = _channel_mix(m["og"].reshape(bp * tp, -1), o_nsa.reshape(bp * tp, -1), m["os"].reshape(bp * tp, -1),
                          yp, lp)
        keep = min(NSA_WINDOW, tp)
        outs_p.append((m["rows"].reshape(bp, tp, 4, NSA_KV_HEADS, HEAD_DIM),
                       m["win"][:, tp - keep:].reshape(bp, keep, 2, NSA_KV_HEADS, HEAD_DIM),
                       m["new_conv"], m["s_gdn"], m["s_ssm"]))
        m = _mixers(ys, bs, ts, jnp.full((bs * ts,), past, jnp.int32), state_conv[l], state_gdn[l], state_ssm[l], lp,
                    tokens_as_time=True)
        pages_l = page_table + l * n_pool
        ck, cv = _compress_t(cache_t, pages_l, lp["nsa_pe"], lp["nsa_phi"])
        q3 = m["q"].reshape(bs, 1, NSA_Q)
        oc, idx = _nsa_sample_cmp(q3, ck, cv, past)
        o_nsa, new_win_t = _nsa_sample_attn(
            cache_t, pages_l, idx.reshape(bs, NSA_KV_HEADS * NSA_TOPK), q3, m["rows"].reshape(bs, 1, 4 * NSA_KV),
            m["win"].reshape(bs, 1, 2 * NSA_KV), win_t, l * bs, m["gates"].reshape(bs, 1, SMALL_W), oc, past)
        new_win = jnp.transpose(new_win_t.reshape(bs, 2, NSA_KV_HEADS, HEAD_DIM, wbl), (0, 4, 1, 2, 3))
        ys = _channel_mix(m["og"].reshape(bs, -1), o_nsa.reshape(bs, -1), m["os"].reshape(bs, -1), ys, lp)
        outs_s.append((m["rows"].reshape(bs, ts, 4, NSA_KV_HEADS, HEAD_DIM),
                       new_win, m["new_conv"], m["s_gdn"], m["s_ssm"]))
    stack = lambda outs, i: jnp.stack([o[i] for o in outs])
    return (yp.reshape(bp, tp, D_MODEL), ys.reshape(bs, ts, D_MODEL),
            stack(outs_p, 0), stack(outs_p, 1), stack(outs_p, 2), stack(outs_p, 3), stack(outs_p, 4),
            stack(outs_s, 0), stack(outs_s, 1), stack(outs_s, 2), stack(outs_s, 3), stack(outs_s, 4))
```

```python
import functools
import math

import jax
import jax.numpy as jnp
from jax import lax
from jax.experimental import pallas as pl
from jax.experimental.pallas import tpu as pltpu

F32 = jnp.float32
BF16 = jnp.bfloat16
HI = lax.Precision.HIGHEST

D_MODEL = 1024
HEAD_DIM = 64
GDN_HEADS = 4
GDN_DK = 64
GDN_DV = 64
CHUNK = 64
NSA_HEADS = 8
NSA_KV_HEADS = 2
NSA_GROUP = 4
NSA_BLOCK = 64
NSA_TOPK = 16
NSA_WINDOW = 512
SSM_HEADS = 4
SSM_HEADDIM = 64
SSM_GROUPS = 2
SSM_DSTATE = 128
CONV_W = 4
PEER_KEYS = 128
PEER_HEADS = 8
PEER_TOPK = 16
PEER_DKEY = 256
ROPE_THETA = 10000.0
EPS = 1e-6

GDN_QK = GDN_HEADS * GDN_DK
GDN_V = GDN_HEADS * GDN_DV
SSM_DI = SSM_HEADS * SSM_HEADDIM
SSM_BC = SSM_GROUPS * SSM_DSTATE
GDN_CONV_CH = 2 * GDN_QK + GDN_V
CONV_CH = GDN_CONV_CH + SSM_DI + 2 * SSM_BC
NSA_Q = NSA_HEADS * HEAD_DIM
NSA_KV = NSA_KV_HEADS * HEAD_DIM
SMALL_W = 128
LANE = 128
VMEM_LIMIT = 48 * 1024 * 1024


def _cparams(*sem):
    return pltpu.CompilerParams(dimension_semantics=sem, vmem_limit_bytes=VMEM_LIMIT)


def _dot(a, b):
    return jnp.dot(a, b, preferred_element_type=F32)


def _split_bf16(a):
    hi = a.astype(BF16)
    return hi, (a - hi.astype(F32)).astype(BF16)


def _dot3_general(a, b, dims):
    a_hi, a_lo = _split_bf16(a)
    b_hi, b_lo = _split_bf16(b)
    dg = lambda x, y: lax.dot_general(x, y, (dims, ((), ())), preferred_element_type=F32)
    return dg(a_hi, b_hi) + dg(a_lo, b_hi) + dg(a_hi, b_lo)


def _dot_hi(a, b):
    return _dot3_general(a, b, ((1,), (0,)))


def _split3(a):
    hi = a.astype(BF16)
    r = a - hi.astype(F32)
    mid = r.astype(BF16)
    return hi, mid, (r - mid.astype(F32)).astype(BF16)


def _dot_sel(sel, x, dims=((1,), (0,)), sel_right=False):
    out = None
    for piece in _split3(x):
        ops = (piece, sel.astype(BF16)) if sel_right else (sel.astype(BF16), piece)
        t = lax.dot_general(*ops, (dims, ((), ())), preferred_element_type=F32)
        out = t if out is None else out + t
    return out


def _dot_nt(a, b, precision=None):
    if precision is not None:
        return _dot3_general(a, b, ((1,), (1,)))
    return lax.dot_general(a, b, (((1,), (1,)), ((), ())), preferred_element_type=F32)


def _dot_tn(a, b, precision=None):
    if precision is not None:
        return _dot3_general(a, b, ((0,), (0,)))
    return lax.dot_general(a, b, (((0,), (0,)), ((), ())), preferred_element_type=F32)


def _sigmoid(x):
    return 1.0 / (1.0 + jnp.exp(-x))


def _softplus(x):
    return jnp.maximum(x, 0.0) + jnp.log1p(jnp.exp(-jnp.abs(x)))


def _iota(shape, dim):
    return lax.broadcasted_iota(jnp.int32, shape, dim)


IN_GROUPS = (("conv", CONV_CH), ("z", GDN_V + SSM_DI), ("q", NSA_Q), ("kv", 6 * NSA_KV), ("small", SMALL_W))
IN_CAT = sum(w for _, w in IN_GROUPS)


def _inproj_kernel(x_ref, g_ref, w_ref, conv_ref, z_ref, q_ref, kv_ref, small_ref):
    x = x_ref[...]
    ms = jnp.mean(x * x, axis=-1, keepdims=True)
    xn = (x * lax.rsqrt(ms + EPS) * g_ref[...]).astype(BF16)
    off = 0
    for ref, (_, w) in zip((conv_ref, z_ref, q_ref, kv_ref, small_ref), IN_GROUPS):
        ref[...] = _dot(xn, w_ref[:, off:off + w])
        off += w


def _inproj(x2d, norm_g, w_cat):
    n = x2d.shape[0]
    tm = min(n, 512)
    return pl.pallas_call(
        _inproj_kernel,
        grid=(n // tm,),
        in_specs=[pl.BlockSpec((tm, D_MODEL), lambda i: (i, 0)),
                  pl.BlockSpec((1, D_MODEL), lambda i: (0, 0)),
                  pl.BlockSpec((D_MODEL, IN_CAT), lambda i: (0, 0))],
        out_specs=[pl.BlockSpec((tm, w), lambda i: (i, 0)) for _, w in IN_GROUPS],
        out_shape=[jax.ShapeDtypeStruct((n, w), F32) for _, w in IN_GROUPS],
        compiler_params=_cparams("parallel"),
        name="inproj",
    )(x2d, norm_g.reshape(1, D_MODEL), w_cat)


def _cat_w_in(w_in):
    o = 0
    parts = {}
    for name, w in (("conv", CONV_CH), ("gdn_z", GDN_V), ("gdn_b", GDN_HEADS), ("gdn_a", GDN_HEADS),
                    ("ssm_z", SSM_DI), ("ssm_dt", SSM_HEADS), ("nsa_q", NSA_Q), ("nsa_kv", 6 * NSA_KV),
                    ("nsa_g", 3 * NSA_HEADS)):
        parts[name] = w_in[:, o:o + w]
        o += w
    small = jnp.concatenate([parts["gdn_b"], parts["gdn_a"], parts["ssm_dt"], parts["nsa_g"]], axis=1)
    small = jnp.pad(small, ((0, 0), (0, SMALL_W - small.shape[1])))
    return jnp.concatenate([parts["conv"], parts["gdn_z"], parts["ssm_z"], parts["nsa_q"], parts["nsa_kv"], small],
                           axis=1).astype(BF16)


HALO = 8


def _conv_kernel(x_ref, hist_ref, w_ref, b_ref, c_ref, nc_ref, xe, *, tm, nt):
    t = pl.program_id(1)

    @pl.when(t == 0)
    def _():
        xe[HALO - 3:HALO, :] = hist_ref[0]

    xe[HALO:HALO + tm, :] = x_ref[0]
    y = b_ref[...] + xe[HALO - 3:HALO - 3 + tm, :] * w_ref[0:1, :]
    for j in range(1, CONV_W):
        y = y + xe[HALO - 3 + j:HALO - 3 + j + tm, :] * w_ref[j:j + 1, :]
    c_ref[0] = y * _sigmoid(y)
    last = xe[HALO + tm - 3:HALO + tm, :]

    @pl.when(t == nt - 1)
    def _():
        nc_ref[0] = last

    xe[HALO - 3:HALO, :] = last


def _conv(x, hist, w, b):
    bsz, t, ch = x.shape
    tm = min(t, 512)
    nt = t // tm
    return pl.pallas_call(
        functools.partial(_conv_kernel, tm=tm, nt=nt),
        grid=(bsz, nt),
        in_specs=[pl.BlockSpec((1, tm, ch), lambda i, j: (i, j, 0)),
                  pl.BlockSpec((1, CONV_W - 1, ch), lambda i, j: (i, 0, 0)),
                  pl.BlockSpec((CONV_W, ch), lambda i, j: (0, 0)),
                  pl.BlockSpec((1, ch), lambda i, j: (0, 0))],
        out_specs=[pl.BlockSpec((1, tm, ch), lambda i, j: (i, j, 0)),
                   pl.BlockSpec((1, CONV_W - 1, ch), lambda i, j: (i, 0, 0))],
        out_shape=[jax.ShapeDtypeStruct((bsz, t, ch), F32),
                   jax.ShapeDtypeStruct((bsz, CONV_W - 1, ch), F32)],
        scratch_shapes=[pltpu.VMEM((HALO + tm, ch), F32)],
        compiler_params=_cparams("parallel", "arbitrary"),
        name="conv",
    )(x, hist, w, b.reshape(1, ch))


def _tri(n, strict=False):
    r, c = _iota((n, n), 0), _iota((n, n), 1)
    return (r > c) if strict else (r >= c)


def _row_of(col):
    n = col.shape[0]
    eye = (_iota((n, n), 0) == _iota((n, n), 1)).astype(F32)
    return _dot_sel(jnp.ones((n, n), F32), eye * col)


def _gdn_kernel(q_ref, k_ref, v_ref, z_ref, small_ref, s0_ref, alog_ref, dtb_ref, ng_ref, o_ref, s_ref, st,
                *, nc, t_valid, bb):
    c = pl.program_id(1)

    @pl.when(c == 0)
    def _():
        st[...] = s0_ref[...]

    incl = _tri(CHUNK)
    strict = _tri(CHUNK, strict=True)
    if t_valid is not None:
        valid = (c * CHUNK + _iota((CHUNK, 1), 0)) < t_valid
    chains = [(b, h) for b in range(bb) for h in range(GDN_HEADS)]
    beta_all, gc_all = {}, {}
    for b in range(bb):
        sm = small_ref[b]
        bt = _sigmoid(sm[:, 0:GDN_HEADS])
        g_all = -jnp.exp(alog_ref[...]) * _softplus(sm[:, GDN_HEADS:2 * GDN_HEADS] + dtb_ref[...])
        if t_valid is not None:
            bt = jnp.where(valid, bt, 0.0)
            g_all = jnp.where(valid, g_all, 0.0)
        beta_all[b] = bt
        gc_all[b] = _dot_sel(incl.astype(F32), g_all)
    q, k, v, beta, gc = {}, {}, {}, {}, {}
    for c_ in chains:
        b, h = c_
        sl = slice(h * GDN_DK, (h + 1) * GDN_DK)
        qq = q_ref[b][:, sl]
        kk = k_ref[b][:, sl]
        vv = v_ref[b][:, sl]
        qq = qq * lax.rsqrt(jnp.sum(qq * qq, axis=-1, keepdims=True) + EPS) * (GDN_DK ** -0.5)
        kk = kk * lax.rsqrt(jnp.sum(kk * kk, axis=-1, keepdims=True) + EPS)
        if t_valid is not None:
            qq = jnp.where(valid, qq, 0.0)
            kk = jnp.where(valid, kk, 0.0)
            vv = jnp.where(valid, vv, 0.0)
        q[c_], k[c_], v[c_] = qq, kk, vv
        beta[c_] = beta_all[b][:, h:h + 1]
        gc[c_] = gc_all[b][:, h:h + 1]
    diff = {c_: gc[c_] - _row_of(gc[c_]) for c_ in chains}
    kkt = {c_: _dot_nt(k[c_], k[c_], HI) for c_ in chains}
    qkt = {c_: _dot_nt(q[c_], k[c_], HI) for c_ in chains}
    egc = {c_: jnp.exp(gc[c_]) for c_ in chains}
    p = {c_: -(beta[c_] * kkt[c_] * jnp.where(strict, jnp.exp(jnp.where(strict, diff[c_], 0.0)), 0.0))
         for c_ in chains}
    x = {c_: jnp.concatenate([v[c_] * beta[c_], k[c_] * (beta[c_] * egc[c_])], axis=-1) for c_ in chains}
    for it in range(6):
        px = {c_: _dot_hi(p[c_], x[c_]) for c_ in chains}
        if it < 5:
            p = {c_: _dot_hi(p[c_], p[c_]) for c_ in chains}
        x = {c_: x[c_] + px[c_] for c_ in chains}
    aqk = {c_: qkt[c_] * jnp.where(incl, jnp.exp(jnp.where(incl, diff[c_], 0.0)), 0.0) for c_ in chains}
    s = {c_: st[c_[0], c_[1]] for c_ in chains}
    wks = {c_: _dot_hi(x[c_][:, GDN_DV:], s[c_]) for c_ in chains}
    qs = {c_: _dot_hi(q[c_] * egc[c_], s[c_]) for c_ in chains}
    w = {c_: x[c_][:, :GDN_DV] - wks[c_] for c_ in chains}
    gl = {c_: gc[c_][CHUNK - 1:CHUNK, :] for c_ in chains}
    kdw = {c_: _dot_tn(k[c_] * jnp.exp(gl[c_] - gc[c_]), w[c_], HI) for c_ in chains}
    aw = {c_: _dot_hi(aqk[c_], w[c_]) for c_ in chains}
    for c_ in chains:
        st[c_[0], c_[1]] = jnp.exp(gl[c_]) * s[c_] + kdw[c_]
    for b in range(bb):
        outs = []
        for h in range(GDN_HEADS):
            o = qs[(b, h)] + aw[(b, h)]
            o = o * lax.rsqrt(jnp.mean(o * o, axis=-1, keepdims=True) + EPS) * ng_ref[...]
            zh = z_ref[b][:, h * GDN_DK:(h + 1) * GDN_DK]
            outs.append(o * (zh * _sigmoid(zh)))
        o_ref[b] = jnp.concatenate(outs, axis=-1)

    @pl.when(c == nc - 1)
    def _():
        s_ref[...] = st[...]


GDN_BB = 2


def _gdn(c, z, small, s0, a_log, dt_bias, norm_g, t_valid=None):
    bsz, t, _ = c.shape
    nc = t // CHUNK
    bb = GDN_BB
    return pl.pallas_call(
        functools.partial(_gdn_kernel, nc=nc, t_valid=t_valid, bb=bb),
        grid=(bsz // bb, nc),
        in_specs=[pl.BlockSpec((bb, CHUNK, GDN_QK), lambda i, j: (i, j, 0)),
                  pl.BlockSpec((bb, CHUNK, GDN_QK), lambda i, j: (i, j, 1)),
                  pl.BlockSpec((bb, CHUNK, GDN_V), lambda i, j: (i, j, 2)),
                  pl.BlockSpec((bb, CHUNK, GDN_V), lambda i, j: (i, j, 0)),
                  pl.BlockSpec((bb, CHUNK, SMALL_W), lambda i, j: (i, j, 0)),
                  pl.BlockSpec((bb, GDN_HEADS, GDN_DK, GDN_DV), lambda i, j: (i, 0, 0, 0)),
                  pl.BlockSpec((1, GDN_HEADS), lambda i, j: (0, 0)),
                  pl.BlockSpec((1, GDN_HEADS), lambda i, j: (0, 0)),
                  pl.BlockSpec((1, GDN_DV), lambda i, j: (0, 0))],
        out_specs=[pl.BlockSpec((bb, CHUNK, GDN_V), lambda i, j: (i, j, 0)),
                   pl.BlockSpec((bb, GDN_HEADS, GDN_DK, GDN_DV), lambda i, j: (i, 0, 0, 0))],
        out_shape=[jax.ShapeDtypeStruct((bsz, t, GDN_V), F32),
                   jax.ShapeDtypeStruct((bsz, GDN_HEADS, GDN_DK, GDN_DV), F32)],
        scratch_shapes=[pltpu.VMEM((bb, GDN_HEADS, GDN_DK, GDN_DV), F32)],
        compiler_params=_cparams("parallel", "arbitrary"),
        name="gdn",
    )(c, c, c, z, small, s0, a_log.reshape(1, -1), dt_bias.reshape(1, -1), norm_g.reshape(1, -1))


def _ssd_kernel(x_ref, b_ref, c_ref, z_ref, small_ref, h0_ref, alog_ref, dtb_ref, dsk_ref, ng_ref, o_ref, h_ref, st,
                *, nc, t_valid):
    ci = pl.program_id(1)

    @pl.when(ci == 0)
    def _():
        st[...] = h0_ref[0]

    sm = small_ref[0]
    dt_all = _softplus(sm[:, 2 * GDN_HEADS:2 * GDN_HEADS + SSM_HEADS] + dtb_ref[...])
    xs = x_ref[0]
    bm_all = b_ref[0]
    cm_all = c_ref[0]
    if t_valid is not None:
        valid = (ci * CHUNK + _iota((CHUNK, 1), 0)) < t_valid
        dt_all = jnp.where(valid, dt_all, 0.0)
        xs = jnp.where(valid, xs, 0.0)
        bm_all = jnp.where(valid, bm_all, 0.0)
        cm_all = jnp.where(valid, cm_all, 0.0)
    incl = _tri(CHUNK)
    cs_all = _dot_sel(incl.astype(F32), dt_all * (-jnp.exp(alog_ref[...])))
    rep = SSM_HEADS // SSM_GROUPS
    heads = range(SSM_HEADS)
    x = [xs[:, h * SSM_HEADDIM:(h + 1) * SSM_HEADDIM] for h in heads]
    bm = [bm_all[:, (h // rep) * SSM_DSTATE:(h // rep + 1) * SSM_DSTATE] for h in heads]
    cm = [cm_all[:, (h // rep) * SSM_DSTATE:(h // rep + 1) * SSM_DSTATE] for h in heads]
    cs = [cs_all[:, h:h + 1] for h in heads]
    rows = [_row_of(cs[h]) for h in heads]
    cbr = [_dot_nt(cm[g * rep], bm[g * rep], HI) for g in range(SSM_GROUPS)]
    xdt = [x[h] * dt_all[:, h:h + 1] for h in heads]
    cl = [cs[h][CHUNK - 1:CHUNK, :] for h in heads]
    hs = [st[h] for h in heads]
    cb = [cbr[h // rep] * jnp.where(incl, jnp.exp(jnp.where(incl, cs[h] - rows[h], 0.0)), 0.0) for h in heads]
    y_diag = [_dot_hi(cb[h], xdt[h]) for h in heads]
    y_off = [_dot_nt(cm[h] * jnp.exp(cs[h]), hs[h], HI) for h in heads]
    upd = [_dot_tn(xdt[h] * jnp.exp(cl[h] - cs[h]), bm[h], HI) for h in heads]
    for h in heads:
        st[h] = jnp.exp(cl[h]) * hs[h] + upd[h]
    y = jnp.concatenate([y_diag[h] + y_off[h] + dsk_ref[0:1, h:h + 1] * x[h] for h in heads], axis=-1)
    zz = z_ref[0]
    y = y * (zz * _sigmoid(zz))
    gs = SSM_DI // SSM_GROUPS
    outs = []
    for g in range(SSM_GROUPS):
        yg = y[:, g * gs:(g + 1) * gs]
        outs.append(yg * lax.rsqrt(jnp.mean(yg * yg, axis=-1, keepdims=True) + EPS) * ng_ref[0:1, g * gs:(g + 1) * gs])
    o_ref[0] = jnp.concatenate(outs, axis=-1)

    @pl.when(ci == nc - 1)
    def _():
        h_ref[0] = st[...]


def _ssd(c, z, small, h0, a_log, dt_bias, d_skip, norm_g, t_valid=None):
    bsz, t, _ = c.shape
    nc = t // CHUNK
    w = SSM_DI
    return pl.pallas_call(
        functools.partial(_ssd_kernel, nc=nc, t_valid=t_valid),
        grid=(bsz, nc),
        in_specs=[pl.BlockSpec((1, CHUNK, w), lambda i, j: (i, j, 3)),
                  pl.BlockSpec((1, CHUNK, w), lambda i, j: (i, j, 4)),
                  pl.BlockSpec((1, CHUNK, w), lambda i, j: (i, j, 5)),
                  pl.BlockSpec((1, CHUNK, w), lambda i, j: (i, j, 1)),
                  pl.BlockSpec((1, CHUNK, SMALL_W), lambda i, j: (i, j, 0)),
                  pl.BlockSpec((1, SSM_HEADS, SSM_HEADDIM, SSM_DSTATE), lambda i, j: (i, 0, 0, 0)),
                  pl.BlockSpec((1, SSM_HEADS), lambda i, j: (0, 0)),
                  pl.BlockSpec((1, SSM_HEADS), lambda i, j: (0, 0)),
                  pl.BlockSpec((1, SSM_HEADS), lambda i, j: (0, 0)),
                  pl.BlockSpec((1, SSM_DI), lambda i, j: (0, 0))],
        out_specs=[pl.BlockSpec((1, CHUNK, SSM_DI), lambda i, j: (i, j, 0)),
                   pl.BlockSpec((1, SSM_HEADS, SSM_HEADDIM, SSM_DSTATE), lambda i, j: (i, 0, 0, 0))],
        out_shape=[jax.ShapeDtypeStruct((bsz, t, SSM_DI), F32),
                   jax.ShapeDtypeStruct((bsz, SSM_HEADS, SSM_HEADDIM, SSM_DSTATE), F32)],
        scratch_shapes=[pltpu.VMEM((SSM_HEADS, SSM_HEADDIM, SSM_DSTATE), F32)],
        compiler_params=_cparams("parallel", "arbitrary"),
        name="ssd",
    )(c, c, c, z, small, h0, a_log.reshape(1, -1), dt_bias.reshape(1, -1), d_skip.reshape(1, -1),
      norm_g.reshape(1, -1))


PAGE = 128
NEG = -1e30


def _rope_tables(pos):
    half = HEAD_DIM // 2
    inv = ROPE_THETA ** (-jnp.arange(half, dtype=F32) * 2.0 / HEAD_DIM)
    ang = pos.astype(F32)[:, None] * inv[None, :]
    cos, sin = jnp.cos(ang), jnp.sin(ang)
    return jnp.tile(cos, (1, 4)), jnp.tile(jnp.concatenate([-sin, sin], axis=1), (1, 2))


def _nsaproj_kernel(q_ref, kv_ref, small_ref, cos_ref, sin_ref, qn_ref, bd_ref,
                    qo_ref, rows_ref, win_ref, gates_ref, ks_ref, vs_ref, kw_ref, vw_ref):
    cosf = cos_ref[...]
    sins = sin_ref[...]
    bd = bd_ref[...]
    first_half = (_iota((1, LANE), 1) % HEAD_DIM) < (HEAD_DIM // 2)

    def normrope(x, g):
        y = x * lax.rsqrt(_dot_hi(x * x, bd) + EPS) * g
        rot = jnp.where(first_half, pltpu.roll(y, LANE - HEAD_DIM // 2, 1), pltpu.roll(y, HEAD_DIM // 2, 1))
        return y * cosf + rot * sins

    qr = q_ref[0]
    qo_ref[0] = jnp.concatenate(
        [normrope(qr[:, i * LANE:(i + 1) * LANE], qn_ref[0:1, :]) * (HEAD_DIM ** -0.5) for i in range(NSA_Q // LANE)],
        axis=-1)
    kv = kv_ref[0]
    kc = normrope(kv[:, 0:LANE], qn_ref[1:2, :])
    vc = kv[:, LANE:2 * LANE]
    ks = normrope(kv[:, 2 * LANE:3 * LANE], qn_ref[2:3, :])
    vs = kv[:, 3 * LANE:4 * LANE]
    kw = normrope(kv[:, 4 * LANE:5 * LANE], qn_ref[3:4, :])
    vw = kv[:, 5 * LANE:6 * LANE]
    rows_ref[0] = jnp.concatenate([kc, vc, ks, vs], axis=-1)
    win_ref[0] = jnp.concatenate([kw, vw], axis=-1)
    gates_ref[0] = _sigmoid(small_ref[0])
    ones = jnp.ones((ks.shape[0], LANE - HEAD_DIM), BF16)
    for h in range(NSA_KV_HEADS):
        sl = slice(h * HEAD_DIM, (h + 1) * HEAD_DIM)
        ks_ref[0, h] = ks[:, sl].astype(BF16)
        vs_ref[0, h] = jnp.concatenate([vs[:, sl].astype(BF16), ones], axis=-1)
        kw_ref[0, h] = kw[:, sl].astype(BF16)
        vw_ref[0, h] = jnp.concatenate([vw[:, sl].astype(BF16), ones], axis=-1)


def _nsaproj(q_raw, kv_raw, small, pos, qk_norm):
    bsz, t, _ = q_raw.shape
    tm = min(t, 512)
    cos, sin = _rope_tables(pos)
    qn = jnp.tile(qk_norm, (1, 2))
    bd = jnp.kron(jnp.eye(2, dtype=F32), jnp.full((HEAD_DIM, HEAD_DIM), 1.0 / HEAD_DIM, F32))
    tok = lambda w: pl.BlockSpec((1, tm, w), lambda i, j: (i, j, 0))
    return pl.pallas_call(
        _nsaproj_kernel,
        grid=(bsz, t // tm),
        in_specs=[tok(NSA_Q), tok(6 * NSA_KV), tok(SMALL_W),
                  pl.BlockSpec((tm, LANE), lambda i, j: (j, 0)),
                  pl.BlockSpec((tm, LANE), lambda i, j: (j, 0)),
                  pl.BlockSpec((4, LANE), lambda i, j: (0, 0)),
                  pl.BlockSpec((LANE, LANE), lambda i, j: (0, 0))],
        out_specs=[tok(NSA_Q), tok(4 * NSA_KV), tok(2 * NSA_KV), tok(SMALL_W)]
                  + [pl.BlockSpec((1, NSA_KV_HEADS, tm, w), lambda i, j: (i, 0, j, 0))
                     for w in (HEAD_DIM, LANE, HEAD_DIM, LANE)],
        out_shape=[jax.ShapeDtypeStruct((bsz, t, NSA_Q), F32),
                   jax.ShapeDtypeStruct((bsz, t, 4 * NSA_KV), F32),
                   jax.ShapeDtypeStruct((bsz, t, 2 * NSA_KV), F32),
                   jax.ShapeDtypeStruct((bsz, t, SMALL_W), F32)]
                  + [jax.ShapeDtypeStruct((bsz, NSA_KV_HEADS, t, w), BF16) for w in (HEAD_DIM, LANE, HEAD_DIM, LANE)],
        compiler_params=_cparams("parallel", "parallel"),
        name="nsaproj",
    )(q_raw, kv_raw, small, cos, sin, qn, bd)


CMP_PAGES = 16


def _compress_kernel(pt_ref, *refs):
    del pt_ref
    page_refs = refs[:CMP_PAGES]
    pe_ref, phi_ref, ck_ref, cv_ref, bufk, bufv = refs[CMP_PAGES:]
    for i, r in enumerate(page_refs):
        bufk[i * PAGE:(i + 1) * PAGE, :] = r[0, :, 0:LANE]
        bufv[i * PAGE:(i + 1) * PAGE, :] = r[0, :, LANE:2 * LANE]
    nblk = CMP_PAGES * PAGE // NSA_BLOCK

    def body(l, carry):
        ak, av = carry
        kl = bufk[pl.ds(l, nblk, stride=NSA_BLOCK), :] + pe_ref[0, pl.ds(l, 1), :]
        vl = bufv[pl.ds(l, nblk, stride=NSA_BLOCK), :] + pe_ref[1, pl.ds(l, 1), :]
        return ak + _dot_hi(kl, phi_ref[0, l]), av + _dot_hi(vl, phi_ref[1, l])

    zero = jnp.zeros((nblk, LANE), F32)
    ak, av = lax.fori_loop(0, NSA_BLOCK, body, (zero, zero), unroll=4)
    ck_ref[0] = ak
    cv_ref[0] = av


def _compress(pages, page_table, pe, phi):
    bsz, npages = page_table.shape
    steps = npages // CMP_PAGES
    nblk = CMP_PAGES * PAGE // NSA_BLOCK
    pe2 = jnp.tile(pe, (1, 1, 2))
    eye2 = jnp.eye(2, dtype=F32)
    phi_bd = jnp.einsum('ab,klde->kladbe', eye2, phi).reshape(2, NSA_BLOCK, LANE, LANE)
    page_specs = [pl.BlockSpec((1, PAGE, 2 * LANE), lambda b, j, pt, i=i: (pt[b, j * CMP_PAGES + i], 0, 0))
                  for i in range(CMP_PAGES)]
    return pl.pallas_call(
        _compress_kernel,
        grid_spec=pltpu.PrefetchScalarGridSpec(
            num_scalar_prefetch=1,
            grid=(bsz, steps),
            in_specs=page_specs + [pl.BlockSpec((2, NSA_BLOCK, LANE), lambda b, j, pt: (0, 0, 0)),
                                   pl.BlockSpec((2, NSA_BLOCK, LANE, LANE), lambda b, j, pt: (0, 0, 0, 0))],
            out_specs=[pl.BlockSpec((1, nblk, LANE), lambda b, j, pt: (b, j, 0)),
                       pl.BlockSpec((1, nblk, LANE), lambda b, j, pt: (b, j, 0))],
            scratch_shapes=[pltpu.VMEM((CMP_PAGES * PAGE, LANE), F32)] * 2),
        out_shape=[jax.ShapeDtypeStruct((bsz, steps * nblk, LANE), F32)] * 2,
        compiler_params=_cparams("parallel", "parallel"),
        name="compress",
    )(page_table, *([pages] * CMP_PAGES), pe2, phi_bd)


QBLK = 256
KTILE = 512


def _colmax(x):
    return jnp.max(x, axis=0, keepdims=True)


def _colsum(x):
    return jnp.sum(x, axis=0, keepdims=True)


def _tile4(x):
    return jnp.concatenate([x] * NSA_GROUP, axis=1)


def _select_blocks(imp, q0, nb):
    nio = _iota(imp.shape, 0)
    nio_f = nio.astype(F32)
    cur = (q0 + _iota((1, imp.shape[1]), 1)) // NSA_BLOCK
    forced = (nio == 0) | (nio == cur) | (nio == cur - 1)
    causal = nio <= cur
    work = jnp.where(forced, jnp.inf, jnp.where(causal, imp, -jnp.inf))
    sel = jnp.zeros(imp.shape, F32)
    for _ in range(min(NSA_TOPK, nb)):
        m = _colmax(work)
        idx = jnp.min(jnp.where(work == m, nio_f, float(nb)), axis=0, keepdims=True)
        pick = nio_f == idx
        sel = jnp.where(pick, 1.0, sel)
        work = jnp.where(pick, -jnp.inf, work)
    return jnp.where(causal, sel, 0.0)


def _nsa_prompt_kernel(q_ref, ck_ref, cv_ref, ks_ref, vs_ref, kw_ref, vw_ref, gt_ref, wb_ref, o_ref, sel_sc, *,
                       t_total):
    nb = t_total // NSA_BLOCK
    kt = min(KTILE, t_total)
    wk = min(NSA_WINDOW + QBLK, t_total)
    bpt = kt // NSA_BLOCK
    qi = pl.program_id(1)
    q0 = qi * QBLK
    qall = q_ref[0]
    gt = gt_ref[0]
    qpos = _tile4(q0 + _iota((1, QBLK), 1))
    eye = (_iota((HEAD_DIM, HEAD_DIM), 0) == _iota((HEAD_DIM, HEAD_DIM), 1)).astype(BF16)
    heads = range(NSA_KV_HEADS)
    qsb, oc = [], []
    for h in heads:
        qs = jnp.concatenate([qall[:, (h * NSA_GROUP + g) * HEAD_DIM:(h * NSA_GROUP + g + 1) * HEAD_DIM]
                              for g in range(NSA_GROUP)], axis=0)
        qsb.append(qs.astype(BF16))
        sl = slice(h * HEAD_DIM, (h + 1) * HEAD_DIM)
        s = _dot_nt(ck_ref[0][:, sl], qs, HI)
        mask = ((_iota((nb, 1), 0) + 1) * NSA_BLOCK - 1) <= qpos
        m = _colmax(jnp.where(mask, s, NEG))
        p = jnp.where(mask, jnp.exp(s - m), 0.0)
        p = p / jnp.maximum(_colsum(p), 1e-30)
        oc.append(_dot_tn(cv_ref[0][:, sl], p, HI))
        imp = p[:, 0:QBLK]
        for g in range(1, NSA_GROUP):
            imp = imp + p[:, g * QBLK:(g + 1) * QBLK]
        sel_sc[h] = (_select_blocks(imp, q0, nb) - 1.0) * (-NEG)

    def body(i, carry, diagonal):
        k0 = pl.multiple_of(i * kt, kt)
        sc = []
        for h in heads:
            brows = sel_sc[h, pl.ds(pl.multiple_of(i * bpt, bpt), bpt), :]
            bias = jnp.concatenate([jnp.broadcast_to(brows[j:j + 1, :], (NSA_BLOCK, QBLK)) for j in range(bpt)],
                                   axis=0)
            s_h = _dot_nt(ks_ref[0, h, pl.ds(k0, kt), :], qsb[h]) + _tile4(bias)
            if diagonal:
                s_h = jnp.where((k0 + _iota((kt, 1), 0)) <= qpos, s_h, NEG)
            sc.append(s_h)
        m_n = [jnp.maximum(carry[h][0], _colmax(sc[h])) for h in heads]
        pp = [jnp.exp((sc[h] - m_n[h]).astype(BF16)) for h in heads]
        pv = [_dot_tn(vs_ref[0, h, pl.ds(k0, kt), :], pp[h]) for h in heads]
        return tuple((m_n[h], carry[h][1] * jnp.exp(carry[h][0] - m_n[h]) + pv[h]) for h in heads)

    ntile = (q0 + QBLK + kt - 1) // kt
    init = (jnp.full((1, NSA_GROUP * QBLK), NEG, F32), jnp.zeros((LANE, NSA_GROUP * QBLK), F32))
    carry = lax.fori_loop(0, ntile - 1, functools.partial(body, diagonal=False), (init,) * NSA_KV_HEADS)
    carry = body(ntile - 1, carry, True)
    w0 = pl.multiple_of(jnp.maximum(q0 + QBLK - wk, 0), QBLK)
    wbias = _tile4(wb_ref[pl.ds(pl.multiple_of(wk - QBLK - (q0 - w0), QBLK), wk), :])
    outs = []
    for h in heads:
        acc_s = carry[h][1]
        osel = acc_s[0:HEAD_DIM] / jnp.maximum(acc_s[HEAD_DIM:HEAD_DIM + 1], 1e-30)
        sw = _dot_nt(kw_ref[0, h, pl.ds(w0, wk), :], qsb[h]) + wbias
        pw = jnp.exp((sw - _colmax(sw)).astype(BF16))
        acc_w = _dot_tn(vw_ref[0, h, pl.ds(w0, wk), :], pw)
        ow = acc_w[0:HEAD_DIM] / jnp.maximum(acc_w[HEAD_DIM:HEAD_DIM + 1], 1e-30)
        gate = [jnp.concatenate([gt[(h * NSA_GROUP + g) * 3 + j:(h * NSA_GROUP + g) * 3 + j + 1, :]
                                 for g in range(NSA_GROUP)], axis=1) for j in range(3)]
        ot = gate[0] * oc[h] + gate[1] * osel + gate[2] * ow
        for g in range(NSA_GROUP):
            outs.append(_dot_tn(ot[:, g * QBLK:(g + 1) * QBLK].astype(BF16), eye))
    o_ref[0] = jnp.concatenate(outs, axis=-1)


def _nsa_prompt(q, ck, cv, ks, vs, kw, vw, gates_t):
    bsz, t, _ = q.shape
    nb = t // NSA_BLOCK
    full = lambda w: pl.BlockSpec((1, NSA_KV_HEADS, t, w), lambda i, j: (i, 0, 0, 0))
    wk = min(NSA_WINDOW + QBLK, t)
    omax = wk - QBLK
    delta = omax + _iota((wk + omax, QBLK), 1) - _iota((wk + omax, QBLK), 0)
    wbias = jnp.where((delta >= 0) & (delta < NSA_WINDOW), 0.0, NEG).astype(F32)
    return pl.pallas_call(
        functools.partial(_nsa_prompt_kernel, t_total=t),
        grid=(bsz, t // QBLK),
        in_specs=[pl.BlockSpec((1, QBLK, NSA_Q), lambda i, j: (i, j, 0)),
                  pl.BlockSpec((1, nb, LANE), lambda i, j: (i, 0, 0)),
                  pl.BlockSpec((1, nb, LANE), lambda i, j: (i, 0, 0)),
                  full(HEAD_DIM), full(LANE), full(HEAD_DIM), full(LANE),
                  pl.BlockSpec((1, 3 * NSA_HEADS, QBLK), lambda i, j: (i, 0, j)),
                  pl.BlockSpec((wk + omax, QBLK), lambda i, j: (0, 0))],
        out_specs=pl.BlockSpec((1, QBLK, NSA_Q), lambda i, j: (i, j, 0)),
        out_shape=jax.ShapeDtypeStruct((bsz, t, NSA_Q), F32),
        scratch_shapes=[pltpu.VMEM((NSA_KV_HEADS, nb, QBLK), F32)],
        compiler_params=_cparams("parallel", "arbitrary"),
        name="nsa_prompt",
    )(q, ck, cv, ks, vs, kw, vw, gates_t, wbias)


def _group_queries(q, h):
    return jnp.concatenate([q[:, (h * NSA_GROUP + g) * HEAD_DIM:(h * NSA_GROUP + g + 1) * HEAD_DIM]
                            for g in range(NSA_GROUP)] + [jnp.zeros((8 - NSA_GROUP, HEAD_DIM), F32)], axis=0)


def _dot_x3(a, w_hi, w_lo):
    a_hi, a_lo = _split_bf16(a)
    return _dot(a_hi, w_hi) + _dot(a_lo, w_hi) + _dot(a_hi, w_lo)


def _compress_t_kernel(pt_ref, *refs, npg):
    del pt_ref
    page_refs = refs[:npg]
    pe_ref, whi_ref, wlo_ref, ck_ref, cv_ref, buf = refs[npg:]
    rows = 2 * LANE
    for i, r in enumerate(page_refs):
        buf[i * rows:(i + 1) * rows, :] = r[0]

    def body(d, carry):
        accs = list(carry)
        for kv in range(2):
            a = jnp.concatenate([buf[pl.ds(kv * LANE + h * HEAD_DIM + d, npg, stride=rows), :]
                                 for h in range(NSA_KV_HEADS)], axis=0) + pe_ref[kv, pl.ds(d, 1), :]
            if kv == 0:
                accs[kv] = accs[kv] + _dot_x3(a, whi_ref[kv, d], wlo_ref[kv, d])
            else:
                accs[kv] = accs[kv] + _dot(a.astype(BF16), whi_ref[kv, d])
        return tuple(accs)

    zero = jnp.zeros((NSA_KV_HEADS * npg, LANE), F32)
    ak, av = lax.fori_loop(0, HEAD_DIM, body, (zero, zero), unroll=4)
    ck_ref[0] = ak
    cv_ref[0] = av


def _compress_t(cache_t, page_table, pe, phi):
    bsz, npg = page_table.shape
    eye2 = jnp.eye(2, dtype=F32)
    w = jnp.einsum('ab,klde->kdalbe', eye2, phi).reshape(2, HEAD_DIM, LANE, LANE)
    w_hi = w.astype(BF16)
    w_lo = (w - w_hi.astype(F32)).astype(BF16)
    pe_t = jnp.tile(jnp.transpose(pe, (0, 2, 1)), (1, 1, 2))
    page_specs = [pl.BlockSpec((1, 2 * LANE, PAGE), lambda b, pt, i=i: (pt[b, i], 0, 0)) for i in range(npg)]
    const = lambda shape: pl.BlockSpec(shape, lambda b, pt: (0,) * len(shape))
    out = pl.BlockSpec((1, NSA_KV_HEADS * npg, LANE), lambda b, pt: (b, 0, 0))
    return pl.pallas_call(
        functools.partial(_compress_t_kernel, npg=npg),
        grid_spec=pltpu.PrefetchScalarGridSpec(
            num_scalar_prefetch=1,
            grid=(bsz,),
            in_specs=page_specs + [const((2, HEAD_DIM, LANE)), const((2, HEAD_DIM, LANE, LANE)),
                                   const((2, HEAD_DIM, LANE, LANE))],
            out_specs=[out, out],
            scratch_shapes=[pltpu.VMEM((npg * 2 * LANE, PAGE), F32)]),
        out_shape=[jax.ShapeDtypeStruct((bsz, NSA_KV_HEADS * npg, LANE), F32)] * 2,
        compiler_params=_cparams("parallel"),
        name="compress_t",
    )(page_table, *([cache_t] * npg), pe_t, w_hi, w_lo)


def _nsa_sample_cmp_kernel(q_ref, ck_ref, cv_ref, oc_ref, idx_ref, *, past, npg):
    nbc = 2 * npg
    cur = past // NSA_BLOCK
    q = q_ref[0]
    lane = _iota((1, nbc), 1)
    blk = 2 * (lane % npg) + lane // npg
    mask = ((blk + 1) * NSA_BLOCK - 1) <= past
    ocs, idxs = [], []
    for h in range(NSA_KV_HEADS):
        qh = _group_queries(q, h)
        ckh = ck_ref[0][h * npg:(h + 1) * npg, :]
        cvh = cv_ref[0][h * npg:(h + 1) * npg, :]
        s = jnp.concatenate([_dot_nt(qh, ckh[:, i * HEAD_DIM:(i + 1) * HEAD_DIM], HI) for i in range(2)], axis=1)
        m = jnp.max(jnp.where(mask, s, NEG), axis=1, keepdims=True)
        p = jnp.where(mask, jnp.exp(s - m), 0.0)
        p = p / jnp.maximum(jnp.sum(p, axis=1, keepdims=True), 1e-30)
        oc = _dot_hi(p[:, 0:npg], cvh[:, 0:HEAD_DIM]) + _dot_hi(p[:, npg:nbc], cvh[:, HEAD_DIM:LANE])
        ocs.append(oc[0:NSA_GROUP])
        imp = p[0:1]
        for g in range(1, NSA_GROUP):
            imp = imp + p[g:g + 1]
        forced = (blk == 0) | (blk == cur) | (blk == cur - 1)
        score = jnp.where(forced, jnp.inf, jnp.where(blk <= cur, imp, -jnp.inf))
        avail = jnp.ones((1, nbc), jnp.bool_)
        slot = _iota((1, NSA_TOPK), 1)
        picked = jnp.full((1, NSA_TOPK), cur, jnp.int32)
        for r in range(NSA_TOPK - 1):
            mm = jnp.max(jnp.where(avail, score, -jnp.inf), axis=1, keepdims=True)
            eq = avail & (score == mm)
            idx = jnp.min(jnp.where(eq, blk, nbc), axis=1, keepdims=True)
            avail = avail & (blk != idx)
            picked = jnp.where(slot == r, idx, picked)
        idxs.append(picked)
    oc_ref[0] = jnp.concatenate(ocs, axis=0)
    idx_ref[0] = jnp.concatenate(idxs, axis=1)


def _nsa_sample_cmp(q3, ck, cv, past):
    bsz = q3.shape[0]
    npg = ck.shape[1] // NSA_KV_HEADS
    return pl.pallas_call(
        functools.partial(_nsa_sample_cmp_kernel, past=past, npg=npg),
        grid=(bsz,),
        in_specs=[pl.BlockSpec((1, 1, NSA_Q), lambda b: (b, 0, 0)),
                  pl.BlockSpec((1, NSA_KV_HEADS * npg, LANE), lambda b: (b, 0, 0)),
                  pl.BlockSpec((1, NSA_KV_HEADS * npg, LANE), lambda b: (b, 0, 0))],
        out_specs=[pl.BlockSpec((1, NSA_HEADS, HEAD_DIM), lambda b: (b, 0, 0)),
                   pl.BlockSpec((1, 1, NSA_KV_HEADS * NSA_TOPK), lambda b: (b, 0, 0))],
        out_shape=[jax.ShapeDtypeStruct((bsz, NSA_HEADS, HEAD_DIM), F32),
                   jax.ShapeDtypeStruct((bsz, 1, NSA_KV_HEADS * NSA_TOPK), jnp.int32)],
        compiler_params=_cparams("parallel"),
        name="nsa_sample_cmp",
    )(q3, ck, cv)


def _nsa_sample_attn_kernel(page_ref, half_ref, *refs, past):
    del page_ref
    nsel = NSA_KV_HEADS * NSA_TOPK
    blk_refs = refs[:nsel]
    q_ref, row_ref, nwin_ref, wc_ref, g_ref, oc_ref, o_ref, win_ref = refs[nsel:]
    b = pl.program_id(0)
    wbl = wc_ref.shape[2]
    q = q_ref[0]
    row = row_ref[0]
    nwin = nwin_ref[0]
    gates = g_ref[0]
    oc = oc_ref[0]
    half_of_lane = _iota((1, PAGE), 1) // NSA_BLOCK
    outs = []
    for h in range(NSA_KV_HEADS):
        sl = slice(h * HEAD_DIM, (h + 1) * HEAD_DIM)
        vsl = slice(LANE + h * HEAD_DIM, LANE + (h + 1) * HEAD_DIM)
        qh = _group_queries(q, h)
        ks_new = row[:, 2 * LANE + h * HEAD_DIM:2 * LANE + (h + 1) * HEAD_DIM]
        vs_new = row[:, 3 * LANE + h * HEAD_DIM:3 * LANE + (h + 1) * HEAD_DIM]
        s_new = jnp.sum(qh * ks_new, axis=1, keepdims=True)
        ss, ms = [], []
        for k in range(NSA_TOPK - 1):
            ss.append(_dot_hi(qh, blk_refs[h * NSA_TOPK + k][0, sl, :]))
            ms.append(half_of_lane == half_ref[b, h * NSA_TOPK + k])
        m = s_new
        for s, mk in zip(ss, ms):
            m = jnp.maximum(m, jnp.max(jnp.where(mk, s, NEG), axis=1, keepdims=True))
        p_new = jnp.exp(s_new - m)
        den = p_new
        acc = p_new * vs_new
        for k, (s, mk) in enumerate(zip(ss, ms)):
            p = jnp.where(mk, jnp.exp(s - m), 0.0)
            den = den + jnp.sum(p, axis=1, keepdims=True)
            acc = acc + _dot_nt(p, blk_refs[h * NSA_TOPK + k][0, vsl, :], HI)
        osel = acc / jnp.maximum(den, 1e-30)
        sw = _dot_hi(qh, wc_ref[0, sl, :])
        delta = wbl - _iota((1, wbl), 1)
        mw = (delta >= 0) & (delta < NSA_WINDOW) & ((past - delta) >= 0)
        kw_new = nwin[:, sl]
        vw_new = nwin[:, vsl]
        sw_new = jnp.sum(qh * kw_new, axis=1, keepdims=True)
        m = jnp.maximum(jnp.max(jnp.where(mw, sw, NEG), axis=1, keepdims=True), sw_new)
        pw = jnp.where(mw, jnp.exp(sw - m), 0.0)
        pw_new = jnp.exp(sw_new - m)
        ow = (_dot_nt(pw, wc_ref[0, vsl, :], HI) + pw_new * vw_new) / jnp.maximum(
            jnp.sum(pw, axis=1, keepdims=True) + pw_new, 1e-30)
        for g in range(NSA_GROUP):
            c0 = 12 + (h * NSA_GROUP + g) * 3
            outs.append(gates[:, c0:c0 + 1] * oc[h * NSA_GROUP + g:h * NSA_GROUP + g + 1, :]
                        + gates[:, c0 + 1:c0 + 2] * osel[g:g + 1, :] + gates[:, c0 + 2:c0 + 3] * ow[g:g + 1, :])
    o_ref[0] = jnp.concatenate(outs, axis=1)
    new_col = _dot_sel(jnp.ones((8, wbl), F32), jnp.concatenate([nwin, jnp.zeros((7, 2 * NSA_KV), F32)], axis=0),
                       ((0,), (0,)), sel_right=True)
    win_ref[0] = jnp.where(_iota((1, wbl), 1) == wbl - 1, new_col, pltpu.roll(wc_ref[0], wbl - 1, 1))


def _nsa_sample_attn(cache_t, page_table, idx, q3, row3, nwin3, win_t, win_off, gates3, oc, past):
    bsz = q3.shape[0]
    wbl = win_t.shape[2]
    ncomplete = past // NSA_BLOCK

    n = jnp.clip(idx, 0, ncomplete - 1)
    page_of = jnp.take_along_axis(page_table, n // 2, axis=1)
    half_of = n % 2

    def blk_map(b, pg, hf, j):
        return (pg[b, j], 1, 0)

    blk_specs = [pl.BlockSpec((1, 2 * LANE, PAGE), functools.partial(blk_map, j=j))
                 for j in range(NSA_KV_HEADS * NSA_TOPK)]
    one = lambda w: pl.BlockSpec((1, 1, w), lambda b, pt, ix: (b, 0, 0))
    win_in = pl.BlockSpec((1, 2 * NSA_KV, wbl), lambda b, pt, ix: (b + win_off, 0, 0))
    win_out = pl.BlockSpec((1, 2 * NSA_KV, wbl), lambda b, pt, ix: (b, 0, 0))
    return pl.pallas_call(
        functools.partial(_nsa_sample_attn_kernel, past=past),
        grid_spec=pltpu.PrefetchScalarGridSpec(
            num_scalar_prefetch=2,
            grid=(bsz,),
            in_specs=blk_specs + [one(NSA_Q), one(4 * NSA_KV), one(2 * NSA_KV), win_in, one(SMALL_W),
                                  pl.BlockSpec((1, NSA_HEADS, HEAD_DIM), lambda b, pt, ix: (b, 0, 0))],
            out_specs=[one(NSA_Q), win_out]),
        out_shape=[jax.ShapeDtypeStruct((bsz, 1, NSA_Q), F32),
                   jax.ShapeDtypeStruct((bsz, 2 * NSA_KV, wbl), F32)],
        compiler_params=_cparams("parallel"),
        name="nsa_sample_attn",
    )(page_of, half_of, *([cache_t] * (NSA_KV_HEADS * NSA_TOPK)), q3, row3, nwin3, win_t, gates3, oc)


def _outproj_kernel(og_ref, on_ref, os_ref, x_ref, wo_ref, g_ref, wq_ref, x1_ref, xn_ref, qh_ref):
    x1 = (x_ref[...] + _dot(og_ref[...].astype(BF16), wo_ref[0:GDN_V, :])
          + _dot(on_ref[...].astype(BF16), wo_ref[GDN_V:GDN_V + NSA_Q, :])
          + _dot(os_ref[...].astype(BF16), wo_ref[GDN_V + NSA_Q:, :]))
    x1_ref[...] = x1
    xn = (x1 * lax.rsqrt(jnp.mean(x1 * x1, axis=-1, keepdims=True) + EPS) * g_ref[...]).astype(BF16)
    xn_ref[...] = xn
    qh_ref[...] = _dot(xn, wq_ref[...])


def _outproj(og, on, os_, x, w_out, norm_g, wq):
    n = x.shape[0]
    tm = min(n, 512)
    dq = PEER_HEADS * PEER_DKEY
    tok = lambda w: pl.BlockSpec((tm, w), lambda i: (i, 0))
    const = lambda a, b: pl.BlockSpec((a, b), lambda i: (0, 0))
    return pl.pallas_call(
        _outproj_kernel,
        grid=(n // tm,),
        in_specs=[tok(GDN_V), tok(NSA_Q), tok(SSM_DI), tok(D_MODEL), const(D_MODEL, D_MODEL), const(1, D_MODEL),
                  const(D_MODEL, dq)],
        out_specs=[tok(D_MODEL), tok(D_MODEL), tok(dq)],
        out_shape=[jax.ShapeDtypeStruct((n, D_MODEL), F32), jax.ShapeDtypeStruct((n, D_MODEL), BF16),
                   jax.ShapeDtypeStruct((n, dq), F32)],
        compiler_params=_cparams("parallel"),
        name="outproj",
    )(og, on, os_, x, w_out.astype(BF16), norm_g.reshape(1, D_MODEL), wq.astype(BF16))


PEER_PAIRS = tuple((a, b) for a in range(PEER_TOPK) for b in range(PEER_TOPK) if (a + 1) * (b + 1) <= PEER_TOPK)


def _topk_rows(s, k):
    n = s.shape[0]
    nio = _iota(s.shape, 0).astype(F32)
    work = s
    rank = jnp.full(s.shape, float(k), F32)
    vals = []
    for r in range(k):
        m = _colmax(work)
        idx = jnp.min(jnp.where(work == m, nio, float(n)), axis=0, keepdims=True)
        pick = nio == idx
        rank = jnp.where(pick, float(r), rank)
        work = jnp.where(pick, -jnp.inf, work)
        vals.append(m)
    return rank, vals


def _dup_bf16_bits(x):
    bits = lax.shift_right_logical(pltpu.bitcast(x.astype(BF16).astype(F32), jnp.uint32), jnp.uint32(16))
    return pltpu.bitcast(bits | lax.shift_left(bits, jnp.uint32(16)), jnp.int32)


def _peer_keys_kernel(qh_ref, sk_ref, r2_ref, e2_ref, lx_ref, e1_ref):
    half = PEER_DKEY // 2
    for h in range(PEER_HEADS):
        q1 = qh_ref[:, h * PEER_DKEY:h * PEER_DKEY + half]
        q2 = qh_ref[:, h * PEER_DKEY + half:(h + 1) * PEER_DKEY]
        s1 = _dot_nt(sk_ref[0], q1, HI)
        s2 = _dot_nt(sk_ref[1], q2, HI)
        rank1, v1 = _topk_rows(s1, PEER_TOPK)
        rank2, v2 = _topk_rows(s2, PEER_TOPK)
        npad = -len(PEER_PAIRS) % 8
        cand = jnp.concatenate([v1[a] + v2[b] for a, b in PEER_PAIRS]
                               + [jnp.full((npad, s1.shape[1]), -jnp.inf, F32)], axis=0)
        crank, _ = _topk_rows(cand, PEER_TOPK)
        chosen = crank < PEER_TOPK
        z = _colsum(jnp.where(chosen, jnp.exp(cand - cand[0:1]), 0.0))
        chosen_f = jnp.where(chosen, 1.0, 0.0)
        lx = jnp.zeros(s1.shape, F32)
        for a in range(PEER_TOPK):
            rows = [i for i, (pa, _) in enumerate(PEER_PAIRS) if pa == a]
            cnt = chosen_f[rows[0]:rows[0] + 1]
            for i in rows[1:]:
                cnt = cnt + chosen_f[i:i + 1]
            lx = jnp.where(rank1 == float(a), cnt, lx)
        r2_ref[h] = rank2.astype(BF16)
        e2_ref[h] = (jnp.exp(s2 - v2[0]) / z).astype(BF16)
        lx_ref[h] = _dup_bf16_bits(lx)
        e1_ref[h] = _dup_bf16_bits(jnp.where(rank1 < PEER_TOPK, jnp.exp(s1 - v1[0]), 0.0))


def _peer_keys(qh, subkeys):
    n = qh.shape[0]
    tk = min(n, 256)
    dq = PEER_HEADS * PEER_DKEY
    half = PEER_DKEY // 2
    out = pl.BlockSpec((PEER_HEADS, PEER_KEYS, tk), lambda i: (0, 0, i))
    return pl.pallas_call(
        _peer_keys_kernel,
        grid=(n // tk,),
        in_specs=[pl.BlockSpec((tk, dq), lambda i: (i, 0)),
                  pl.BlockSpec((2, PEER_KEYS, half), lambda i: (0, 0, 0))],
        out_specs=[out] * 4,
        out_shape=[jax.ShapeDtypeStruct((PEER_HEADS, PEER_KEYS, n), dt) for dt in (BF16, BF16, jnp.int32, jnp.int32)],
        compiler_params=_cparams("parallel"),
        name="peer_keys",
    )(qh, subkeys)


PEER_AC = 16


def _gelu_tanh(x):
    c = math.sqrt(2.0 / math.pi)
    return x * (0.5 + 0.5 * jnp.tanh(x * (c + (c * 0.044715) * (x * x))))


def _row_tile_bf16(rows, a):
    t = pltpu.bitcast(jnp.broadcast_to(rows[a:a + 1, :], (8, rows.shape[1])), BF16)
    return jnp.concatenate([t] * (PEER_KEYS // t.shape[0]), axis=0)


PEER_SUBS = (8, 8)


def _peer_kernel(xn_ref, u_ref, vt_ref, r2_ref, e2_ref, lx_ref, e1_ref, x1_ref, o_ref, acc, *, nsteps):
    j = pl.program_id(1)

    @pl.when(j == 0)
    def _():
        acc[...] = jnp.zeros_like(acc)

    xn = xn_ref[...]
    assert sum(PEER_SUBS) == PEER_AC
    nsb = len(PEER_SUBS)
    first = [sum(PEER_SUBS[:i]) for i in range(nsb + 1)]

    def hidden(sb):
        return _dot_nt(u_ref[first[sb] * PEER_KEYS:first[sb + 1] * PEER_KEYS, :], xn)

    lx_rows = [lx_ref[h] for h in range(PEER_HEADS)]
    e1_rows = [e1_ref[h] for h in range(PEER_HEADS)]

    def gates(sb):
        out = []
        for a in range(first[sb], first[sb + 1]):
            gate = None
            for h in range(PEER_HEADS):
                term = jnp.where(r2_ref[h] < _row_tile_bf16(lx_rows[h], a), e2_ref[h],
                                 jnp.zeros((), BF16)) * _row_tile_bf16(e1_rows[h], a)
                gate = term if gate is None else gate + term
            out.append(gate)
        return out

    part = None
    ht_next = hidden(0)
    for sb in range(nsb):
        ht = ht_next
        gs = gates(sb)
        if sb + 1 < nsb:
            ht_next = hidden(sb + 1)
        ws = [gs[a] * _gelu_tanh(ht[a * PEER_KEYS:(a + 1) * PEER_KEYS, :].astype(BF16))
              for a in range(PEER_SUBS[sb])]
        w = ws[0] if len(ws) == 1 else jnp.concatenate(ws, axis=0)
        t = _dot(vt_ref[:, first[sb] * PEER_KEYS:first[sb + 1] * PEER_KEYS], w)
        part = t if part is None else t + part
    acc[...] += part

    @pl.when(j == nsteps - 1)
    def _():
        o_ref[...] = x1_ref[...] + acc[...].T


def _peer(xn, u_bf, vt_bf, r2, e2, lx, e1, x1):
    n = xn.shape[0]
    tm = min(n, 512)
    ec = PEER_AC * PEER_KEYS
    nsteps = PEER_KEYS // PEER_AC
    tab = pl.BlockSpec((PEER_HEADS, PEER_KEYS, tm), lambda i, j: (0, 0, i))
    rows = pl.BlockSpec((PEER_HEADS, PEER_AC, tm), lambda i, j: (0, j, i))
    return pl.pallas_call(
        functools.partial(_peer_kernel, nsteps=nsteps),
        grid=(n // tm, nsteps),
        in_specs=[pl.BlockSpec((tm, D_MODEL), lambda i, j: (i, 0)),
                  pl.BlockSpec((ec, D_MODEL), lambda i, j: (j, 0)),
                  pl.BlockSpec((D_MODEL, ec), lambda i, j: (0, j)),
                  tab, tab, rows, rows,
                  pl.BlockSpec((tm, D_MODEL), lambda i, j: (i, 0))],
        out_specs=pl.BlockSpec((tm, D_MODEL), lambda i, j: (i, 0)),
        out_shape=jax.ShapeDtypeStruct((n, D_MODEL), F32),
        scratch_shapes=[pltpu.VMEM((D_MODEL, tm), F32)],
        compiler_params=_cparams("parallel", "arbitrary"),
        name="peer",
    )(xn, u_bf, vt_bf, r2, e2, lx, e1, x1)


def _pad_axis(a, axis, size):
    if a.shape[axis] == size:
        return a
    pad = [(0, 0)] * a.ndim
    pad[axis] = (0, size - a.shape[axis])
    return jnp.pad(a, pad)


def _mixers(x2d, bsz, t, pos, conv_hist, gdn0, ssm0, lp, tokens_as_time=False):
    conv_in, z, q, kv, small = _inproj(x2d, lp["norm_mix"], lp["w_cat"])
    r3 = lambda a: a.reshape(bsz, t, -1)
    c, new_conv = _conv(r3(conv_in), conv_hist, lp["conv_w"], lp["conv_b"])
    tpad = -(-t // CHUNK) * CHUNK
    t_valid = None if tpad == t else t
    cp, zp, sp = (_pad_axis(a, 1, tpad) for a in (c, r3(z), r3(small)))
    og, s_gdn = _gdn(cp, zp, sp, gdn0, lp["gdn_a_log"], lp["gdn_dt_bias"], lp["gdn_norm"], t_valid=t_valid)
    os_, s_ssm = _ssd(cp, zp, sp, ssm0, lp["ssm_a_log"], lp["ssm_dt_bias"], lp["ssm_d"], lp["ssm_norm"],
                      t_valid=t_valid)
    lead = (1, bsz * t) if tokens_as_time else (bsz, t)
    qn, rows, win, gates, ks, vs, kw, vw = _nsaproj(q.reshape(*lead, -1), kv.reshape(*lead, -1),
                                                    small.reshape(*lead, -1), pos, lp["nsa_qk_norm"])
    return dict(og=og[:, :t], os=os_[:, :t], new_conv=new_conv, s_gdn=s_gdn, s_ssm=s_ssm, q=r3(qn), rows=r3(rows),
                win=r3(win), gates=r3(gates), ks=ks, vs=vs, kw=kw, vw=vw)


def _channel_mix(og, on, os_, x, lp):
    x1, xn, qh = _outproj(og, on, os_, x, lp["w_out"], lp["norm_ffn"], lp["peer_wq"])
    n = x.shape[0]
    npad = -(-n // LANE) * LANE
    x1, xn, qh = (_pad_axis(a, 0, npad) for a in (x1, xn, qh))
    r2, e2, lx, e1 = _peer_keys(qh, lp["peer_subkeys"])
    return _peer(xn, lp["peer_u"], lp["peer_vt"], r2, e2, lx, e1, x1)[:n]


def kernel(x_prompt, x_sample, cache_nsa_kv, cache_nsa_win, state_conv, state_gdn, state_ssm, page_table, norm_mix, w_in, conv_w, conv_b, gdn_a_log, gdn_dt_bias, gdn_norm, ssm_a_log, ssm_dt_bias, ssm_d, ssm_norm, nsa_qk_norm, nsa_pe, nsa_phi, w_out, norm_ffn, peer_wq, peer_subkeys, peer_u, peer_v):
    bp, tp, _ = x_prompt.shape
    bs, ts, _ = x_sample.shape
    past = page_table.shape[1] * cache_nsa_kv.shape[2]
    n_pool = cache_nsa_kv.shape[1]
    yp = x_prompt.reshape(bp * tp, D_MODEL)
    ys = x_sample.reshape(bs * ts, D_MODEL)
    depth = norm_mix.shape[0]
    wbl = cache_nsa_win.shape[2]
    cache_t = jnp.transpose(cache_nsa_kv, (0, 1, 3, 4, 5, 2)).reshape(depth * n_pool, 4 * NSA_KV, PAGE)
    win_t = jnp.transpose(cache_nsa_win, (0, 1, 3, 4, 5, 2)).reshape(depth * bs, 2 * NSA_KV, wbl)
    outs_p, outs_s = [], []
    for l in range(norm_mix.shape[0]):
        lp = dict(norm_mix=norm_mix[l], w_cat=_cat_w_in(w_in[l]), conv_w=conv_w[l], conv_b=conv_b[l],
                  gdn_a_log=gdn_a_log[l], gdn_dt_bias=gdn_dt_bias[l], gdn_norm=gdn_norm[l],
                  ssm_a_log=ssm_a_log[l], ssm_dt_bias=ssm_dt_bias[l], ssm_d=ssm_d[l], ssm_norm=ssm_norm[l],
                  nsa_qk_norm=nsa_qk_norm[l], nsa_pe=nsa_pe[l], nsa_phi=nsa_phi[l], w_out=w_out[l],
                  norm_ffn=norm_ffn[l], peer_wq=peer_wq[l], peer_subkeys=peer_subkeys[l],
                  peer_u=peer_u[l].astype(BF16), peer_vt=peer_v[l].T.astype(BF16))
        m = _mixers(yp, bp, tp, jnp.arange(tp, dtype=jnp.int32), jnp.zeros((bp, CONV_W - 1, CONV_CH), F32),
                    jnp.zeros((bp, GDN_HEADS, GDN_DK, GDN_DV), F32),
                    jnp.zeros((bp, SSM_HEADS, SSM_HEADDIM, SSM_DSTATE), F32), lp)
        pages = m["rows"].reshape(bp * tp // PAGE, PAGE, 4 * NSA_KV)
        ident = jnp.arange(bp * tp // PAGE, dtype=jnp.int32).reshape(bp, tp // PAGE)
        ck, cv = _compress(pages, ident, lp["nsa_pe"], lp["nsa_phi"])
        gates_t = jnp.transpose(m["gates"][..., 12:12 + 3 * NSA_HEADS], (0, 2, 1))
        o_nsa = _nsa_prompt(m["q"], ck, cv, m["ks"], m["vs"], m["kw"], m["vw"], gates_t)
        yp = _channel_mix(m["og"].reshape(bp * tp, -1), o_nsa.reshape(bp * tp, -1), m["os"].reshape(bp * tp, -1),
                          yp, lp)
        keep = min(NSA_WINDOW, tp)
        outs_p.append((m["rows"].reshape(bp, tp, 4, NSA_KV_HEADS, HEAD_DIM),
                       m["win"][:, tp - keep:].reshape(bp, keep, 2, NSA_KV_HEADS, HEAD_DIM),
                       m["new_conv"], m["s_gdn"], m["s_ssm"]))
        m = _mixers(ys, bs, ts, jnp.full((bs * ts,), past, jnp.int32), state_conv[l], state_gdn[l], state_ssm[l], lp,
                    tokens_as_time=True)
        pages_l = page_table + l * n_pool
        ck, cv = _compress_t(cache_t, pages_l, lp["nsa_pe"], lp["nsa_phi"])
        q3 = m["q"].reshape(bs, 1, NSA_Q)
        oc, idx = _nsa_sample_cmp(q3, ck, cv, past)
        o_nsa, new_win_t = _nsa_sample_attn(
            cache_t, pages_l, idx.reshape(bs, NSA_KV_HEADS * NSA_TOPK), q3, m["rows"].reshape(bs, 1, 4 * NSA_KV),
            m["win"].reshape(bs, 1, 2 * NSA_KV), win_t, l * bs, m["gates"].reshape(bs, 1, SMALL_W), oc, past)
        new_win = jnp.transpose(new_win_t.reshape(bs, 2, NSA_KV_HEADS, HEAD_DIM, wbl), (0, 4, 1, 2, 3))
        ys = _channel_mix(m["og"].reshape(bs, -1), o_nsa.reshape(bs, -1), m["os"].reshape(bs, -1), ys, lp)
        outs_s.append((m["rows"].reshape(bs, ts, 4, NSA_KV_HEADS, HEAD_DIM),
                       new_win, m["new_conv"], m["s_gdn"], m["s_ssm"]))
    stack = lambda outs, i: jnp.stack([o[i] for o in outs])
    return (yp.reshape(bp, tp, D_MODEL), ys.reshape(bs, ts, D_MODEL),
            stack(outs_p, 0), stack(outs_p, 1), stack(outs_p, 2), stack(outs_p, 3), stack(outs_p, 4),
            stack(outs_s, 0), stack(outs_s, 1), stack(outs_s, 2), stack(outs_s, 3), stack(outs_s, 4))
```
